```python
import math
import jax
import jax.numpy as jnp
from jax import lax
import numpy as np

D_MODEL = 1024
BATCH = 8
SEQ = 4096
DEPTH = 1

GRID_W = 64
CTX_LEN = 256
DN_HEADS = 4
DN_HEAD_DIM = 128
DN_WIDTH = DN_HEADS * DN_HEAD_DIM
SHORT_CONV = 5
CHUNK = 64
POOL_GROUPS = 4
POOL_WINDOWS = (2, 4, 8, 16)
POOL_WIDTH = D_MODEL - DN_WIDTH
POOL_GROUP_DIM = POOL_WIDTH // POOL_GROUPS
MIX_WIDTH = DN_WIDTH + POOL_WIDTH
D_FF = 128 * ((8 * D_MODEL // 3 + 127) // 128)
FFN_CONV = 3
N_MOD = 6
NORM_EPS = 1e-6
Q_OFF = 0
K_OFF = DN_WIDTH
V_OFF = 2 * DN_WIDTH
Z_OFF = 3 * DN_WIDTH
GATE_OFF = 4 * DN_WIDTH
POOL_OFF = GATE_OFF + 4 * DN_HEADS
IN_COLS = POOL_OFF + POOL_WIDTH

kernel_name = 'hymba_gdn_pool_convffn_dit_block'


def rms_norm(x, g):
    xf = x.astype(jnp.float32)
    y = xf * lax.rsqrt(jnp.mean(xf * xf, axis=-1, keepdims=True) + NORM_EPS)
    return (y * g.astype(jnp.float32)).astype(x.dtype)


def l2_normalize(x):
    xf = x.astype(jnp.float32)
    return xf * lax.rsqrt(jnp.sum(xf * xf, axis=-1, keepdims=True) + NORM_EPS)


def modulation(cond, w_mod, b_mod):
    m = jax.nn.silu(cond) @ w_mod + b_mod
    return jnp.split(m[..., None, :], N_MOD, axis=-1)


def depthwise_conv(x, w):
    pad = w.shape[0] // 2
    return lax.conv_general_dilated(x, w[:, None, :], window_strides=(1,), padding=[(pad, pad)],
                                    dimension_numbers=('NWC', 'WIO', 'NWC'),
                                    feature_group_count=x.shape[-1])


def split_heads(a):
    return a.reshape(*a.shape[:-1], -1, DN_HEAD_DIM)


def gated_delta_rule(q, k, v, g, beta, s0):
    bsz, t, nh, dk = k.shape
    n = t // CHUNK

    def to_chunks(a):
        a = a.astype(jnp.float32).reshape(bsz, n, CHUNK, nh, *a.shape[3:])
        return jnp.moveaxis(a, 3, 1)

    k_c, v_c = to_chunks(k), to_chunks(v)
    gcum = jnp.cumsum(to_chunks(g), axis=-1)
    b_c = to_chunks(beta)[..., None]
    incl = jnp.tril(jnp.ones((CHUNK, CHUNK), dtype=bool))
    strict = jnp.tril(jnp.ones((CHUNK, CHUNK), dtype=bool), k=-1)
    decay = jnp.exp(jnp.where(incl, gcum[..., :, None] - gcum[..., None, :], -jnp.inf))
    kb = k_c * b_c
    lower = jnp.where(strict, jnp.einsum('bhnid,bhnjd->bhnij', kb, k_c) * decay, 0.0)
    eye = jnp.eye(CHUNK, dtype=jnp.float32)
    t_inv = lax.linalg.triangular_solve(lower + eye, jnp.broadcast_to(eye, lower.shape),
                                        left_side=True, lower=True, unit_diagonal=True)
    u_c = jnp.einsum('bhnij,bhnjv->bhniv', t_inv, v_c * b_c)
    w_c = jnp.einsum('bhnij,bhnjk->bhnik', t_inv, kb * jnp.exp(gcum)[..., None])
    g_last = gcum[..., -1]
    k_dec = k_c * jnp.exp(g_last[..., None] - gcum)[..., None]

    def seq_first(a):
        return jnp.moveaxis(a, 2, 0)

    xs = [seq_first(a) for a in (u_c, w_c, k_dec, g_last)]
    with_out = q is not None
    if with_out:
        q_c = to_chunks(q) * dk ** -0.5
        attn = jnp.where(incl, jnp.einsum('bhnik,bhnjk->bhnij', q_c, k_c) * decay, 0.0)
        xs += [seq_first(q_c * jnp.exp(gcum)[..., None]), seq_first(attn)]

    def step(s, xs_n):
        u_n, w_n, kd_n, gl_n = xs_n[:4]
        v_new = u_n - jnp.einsum('bhck,bhkv->bhcv', w_n, s)
        s_new = s * jnp.exp(gl_n)[..., None, None] + jnp.einsum('bhck,bhcv->bhkv', kd_n, v_new)
        if not with_out:
            return s_new, None
        qd_n, a_n = xs_n[4:]
        o_n = (jnp.einsum('bhck,bhkv->bhcv', qd_n, s)
               + jnp.einsum('bhij,bhjv->bhiv', a_n, v_new))
        return s_new, o_n

    s_fin, o = lax.scan(step, s0, tuple(xs))
    if not with_out:
        return None, s_fin
    o = jnp.moveaxis(jnp.moveaxis(o, 0, 2), 1, 3).reshape(bsz, t, nh, -1)
    return o, s_fin


def dn_gates(p_gate, a_log, dt_bias):
    pf = p_gate.astype(jnp.float32)
    b_f, b_b, a_f, a_b = jnp.split(pf, 4, axis=-1)

    def log_decay(a, d):
        return -jnp.exp(a_log[d].astype(jnp.float32)) * jax.nn.softplus(a + dt_bias[d].astype(jnp.float32))

    return (log_decay(a_f, 0), jax.nn.sigmoid(b_f)), (log_decay(a_b, 1), jax.nn.sigmoid(b_b))


def bidirectional_delta(q, k, v, gates, s0_f, s0_b):
    (g_f, beta_f), (g_b, beta_b) = gates

    def flip(a):
        return None if a is None else jnp.flip(a, axis=1)

    o_f, s_f = gated_delta_rule(q, k, v, g_f, beta_f, s0_f)
    o_b, s_b = gated_delta_rule(flip(q), flip(k), flip(v), flip(g_b), flip(beta_b), s0_b)
    o = None if q is None else o_f + flip(o_b)
    return o, s_f, s_b


def box_mean(u, w, axis):
    n = u.shape[axis]
    left = w // 2
    right = w - 1 - left
    cs = jnp.cumsum(u, axis=axis)
    cs = jnp.concatenate([jnp.zeros_like(lax.slice_in_dim(cs, 0, 1, axis=axis)), cs], axis=axis)
    idx = jnp.arange(n)
    lo = jnp.clip(idx - left, 0, n)
    hi = jnp.clip(idx + right + 1, 0, n)
    total = jnp.take(cs, hi, axis=axis) - jnp.take(cs, lo, axis=axis)
    shape = [1] * u.ndim
    shape[axis] = n
    return total / (hi - lo).astype(u.dtype).reshape(shape)


def pool_mixer(u, pool_w, pool_scale, rows):
    bsz, t, _ = u.shape
    if rows is None:
        spatial, axes = (t,), (1,)
    else:
        spatial, axes = (rows, GRID_W), (1, 2)
    uf = u.astype(jnp.float32).reshape(bsz, *spatial, POOL_GROUPS, POOL_GROUP_DIM)
    diffs = []
    for gi, w in enumerate(POOL_WINDOWS):
        ug = uf[..., gi, :]
        m = ug
        for ax in axes:
            m = box_mean(m, w, ax)
        diffs.append(m - ug)
    d = jnp.stack(diffs, axis=-2).reshape(bsz, t, POOL_GROUPS, POOL_GROUP_DIM).astype(u.dtype)
    y = jnp.einsum('btgc,gcd->btgd', d, pool_w).reshape(bsz, t, POOL_WIDTH)
    return y * pool_scale


def token_mixers(h, rows, s0_f, s0_b, w_in, conv_qkv, a_log, dt_bias, o_norm, pool_w, pool_scale, w_out):
    p = h @ w_in
    qkv = jax.nn.silu(depthwise_conv(p[..., Q_OFF:Z_OFF], conv_qkv))
    q, k, v = (split_heads(a) for a in jnp.split(qkv, 3, axis=-1))
    gates = dn_gates(p[..., GATE_OFF:POOL_OFF], a_log, dt_bias)
    o, s_f, s_b = bidirectional_delta(l2_normalize(q), l2_normalize(k), v, gates, s0_f, s0_b)
    z = split_heads(p[..., Z_OFF:GATE_OFF]).astype(jnp.float32)
    o = rms_norm(o, o_norm) * jax.nn.silu(z)
    dn_out = o.reshape(*h.shape[:-1], DN_WIDTH).astype(h.dtype)
    pool_out = pool_mixer(p[..., POOL_OFF:], pool_w, pool_scale, rows)
    y = jnp.concatenate([dn_out, pool_out], axis=-1) @ w_out
    return y, s_f, s_b


def context_states(hc, w_in, conv_qkv, a_log, dt_bias):
    kv = jax.nn.silu(depthwise_conv(hc @ w_in[:, K_OFF:Z_OFF], conv_qkv[:, K_OFF:Z_OFF]))
    k, v = (split_heads(a) for a in jnp.split(kv, 2, axis=-1))
    gates = dn_gates(hc @ w_in[:, GATE_OFF:POOL_OFF], a_log, dt_bias)
    zeros = jnp.zeros((hc.shape[0], DN_HEADS, DN_HEAD_DIM, DN_HEAD_DIM), jnp.float32)
    _, s_f, s_b = bidirectional_delta(None, l2_normalize(k), v, gates, zeros, zeros)
    return s_f, s_b


def conv_glu_ffn(h, w_up, conv_w, w_down):
    gate, up = jnp.split(h @ w_up, 2, axis=-1)
    return (jax.nn.silu(depthwise_conv(gate, conv_w)) * up) @ w_down


def layer(x, ctx, c, c_ctx, rows, update_ctx, w_mod, b_mod, g_pre_mix, g_post_mix, g_pre_ffn,
          g_post_ffn, w_in, conv_qkv, a_log, dt_bias, o_norm, pool_w, pool_scale, w_out,
          w_up, conv_ffn, w_down):
    sh1, sc1, gt1, sh2, sc2, gt2 = modulation(c, w_mod, b_mod)
    csh1, csc1, cgt1, csh2, csc2, cgt2 = modulation(c_ctx, w_mod, b_mod)
    mix_params = (w_in, conv_qkv, a_log, dt_bias, o_norm, pool_w, pool_scale, w_out)
    hx = rms_norm(x, g_pre_mix) * (1 + sc1) + sh1
    hc = rms_norm(ctx, g_pre_mix) * (1 + csc1) + csh1
    if update_ctx:
        zeros = jnp.zeros((ctx.shape[0], DN_HEADS, DN_HEAD_DIM, DN_HEAD_DIM), jnp.float32)
        yc, s_f, s_b = token_mixers(hc, None, zeros, zeros, *mix_params)
    else:
        s_f, s_b = context_states(hc, w_in, conv_qkv, a_log, dt_bias)
    yx, _, _ = token_mixers(hx, rows, s_f, s_b, *mix_params)
    x = x + gt1 * rms_norm(yx, g_post_mix)
    hx = rms_norm(x, g_pre_ffn) * (1 + sc2) + sh2
    x = x + gt2 * rms_norm(conv_glu_ffn(hx, w_up, conv_ffn, w_down), g_post_ffn)
    if update_ctx:
        ctx = ctx + cgt1 * rms_norm(yc, g_post_mix)
        hc = rms_norm(ctx, g_pre_ffn) * (1 + csc2) + csh2
        ctx = ctx + cgt2 * rms_norm(conv_glu_ffn(hc, w_up, conv_ffn, w_down), g_post_ffn)
    return x, ctx


def setup_inputs(seed: int = 0) -> dict:
    key = jax.random.key(seed)
    ks = jax.random.split(key, 21)
    f32 = jnp.float32

    def nrm(k, shape, scale):
        return jax.random.normal(k, shape, f32) * scale

    dt = jnp.exp(jax.random.uniform(ks[13], (DEPTH, 2, DN_HEADS), f32,
                                    minval=math.log(1e-3), maxval=math.log(1e-1)))
    return {
        'x': nrm(ks[0], (BATCH, SEQ, D_MODEL), 1.0),
        'c': nrm(ks[1], (BATCH, D_MODEL), 1.0),
        'ctx': nrm(ks[2], (BATCH, CTX_LEN, D_MODEL), 1.0),
        'c_ctx': nrm(ks[3], (D_MODEL,), 1.0),
        'w_mod': nrm(ks[4], (DEPTH, D_MODEL, N_MOD * D_MODEL), 0.5 * D_MODEL ** -0.5),
        'b_mod': nrm(ks[5], (DEPTH, N_MOD * D_MODEL), 0.02),
        'g_pre_mix': 1.0 + nrm(ks[6], (DEPTH, D_MODEL), 0.05),
        'g_post_mix': 1.0 + nrm(ks[7], (DEPTH, D_MODEL), 0.05),
        'g_pre_ffn': 1.0 + nrm(ks[8], (DEPTH, D_MODEL), 0.05),
        'g_post_ffn': 1.0 + nrm(ks[9], (DEPTH, D_MODEL), 0.05),
        'w_in': nrm(ks[10], (DEPTH, D_MODEL, IN_COLS), D_MODEL ** -0.5),
        'conv_qkv': nrm(ks[11], (DEPTH, SHORT_CONV, 3 * DN_WIDTH), SHORT_CONV ** -0.5),
        'a_log': jnp.log(jax.random.uniform(ks[12], (DEPTH, 2, DN_HEADS), f32, minval=1.0, maxval=16.0)),
        'dt_bias': jnp.log(jnp.expm1(dt)),
        'o_norm': 1.0 + nrm(ks[14], (DEPTH, DN_HEAD_DIM), 0.05),
        'pool_w': nrm(ks[15], (DEPTH, POOL_GROUPS, POOL_GROUP_DIM, POOL_GROUP_DIM), POOL_GROUP_DIM ** -0.5),
        'pool_scale': 1.0 + nrm(ks[16], (DEPTH, POOL_WIDTH), 0.05),
        'w_out': nrm(ks[17], (DEPTH, MIX_WIDTH, D_MODEL), MIX_WIDTH ** -0.5),
        'w_up': nrm(ks[18], (DEPTH, D_MODEL, 2 * D_FF), D_MODEL ** -0.5),
        'conv_ffn': nrm(ks[19], (DEPTH, FFN_CONV, D_FF), FFN_CONV ** -0.5),
        'w_down': nrm(ks[20], (DEPTH, D_FF, D_MODEL), D_FF ** -0.5),
    }


def reference(x, c, ctx, c_ctx, w_mod, b_mod, g_pre_mix, g_post_mix, g_pre_ffn, g_post_ffn,
              w_in, conv_qkv, a_log, dt_bias, o_norm, pool_w, pool_scale, w_out, w_up,
              conv_ffn, w_down):
    rows = x.shape[1] // GRID_W
    for i in range(DEPTH):
        x, ctx = layer(x, ctx, c, c_ctx, rows, i < DEPTH - 1,
                       w_mod[i], b_mod[i], g_pre_mix[i], g_post_mix[i], g_pre_ffn[i],
                       g_post_ffn[i], w_in[i], conv_qkv[i], a_log[i], dt_bias[i], o_norm[i],
                       pool_w[i], pool_scale[i], w_out[i], w_up[i], conv_ffn[i], w_down[i])
    return x
```

```python
import functools

import numpy as np
import jax
import jax.numpy as jnp
from jax import lax
from jax.experimental import pallas as pl
from jax.experimental.pallas import tpu as pltpu

NORM_EPS = 1e-6
GRID_W = 64
POOL_WINDOWS = (2, 4, 8, 16)
N_MOD = 6
CHUNK = 64
HEAD_DIM = 128
VMEM_LIMIT = 56 * 1024 * 1024

F32 = jnp.float32
BF16 = jnp.bfloat16
HIGHEST = lax.Precision.HIGHEST


def _silu(a):
    return a * (1.0 / (1.0 + jnp.exp(-a)))


def _dot(a, b):
    return jnp.dot(a.astype(BF16), b.astype(BF16), preferred_element_type=F32)


def _dot_nt(a, b):
    return lax.dot_general(a.astype(BF16), b.astype(BF16), (((1,), (1,)), ((), ())),
                           preferred_element_type=F32)


def _dot_tn(a, b):
    return lax.dot_general(a.astype(BF16), b.astype(BF16), (((0,), (0,)), ((), ())),
                           preferred_element_type=F32)


def _dot_hi(a, b):
    return jnp.dot(a, b, preferred_element_type=F32, precision=HIGHEST)


def _params(*sem):
    return pltpu.CompilerParams(dimension_semantics=sem, vmem_limit_bytes=VMEM_LIMIT)


def _mod_kernel(c_ref, w_ref, b_ref, o_ref):
    o_ref[...] = _dot_hi(_silu(c_ref[...]), w_ref[...]) + b_ref[...]


def _modulation(c_all, w_mod, b_mod):
    r, d = c_all.shape
    n = w_mod.shape[1]
    tn = 512
    return pl.pallas_call(
        _mod_kernel,
        grid=(n // tn,),
        in_specs=[pl.BlockSpec((r, d), lambda j: (0, 0)),
                  pl.BlockSpec((d, tn), lambda j: (0, j)),
                  pl.BlockSpec((1, tn), lambda j: (0, j))],
        out_specs=pl.BlockSpec((r, tn), lambda j: (0, j)),
        out_shape=jax.ShapeDtypeStruct((r, n), F32),
        compiler_params=_params("parallel"),
        name="modulation",
    )(c_all, w_mod, b_mod.reshape(1, n))


def _inproj_kernel(x_ref, sc_ref, sh_ref, g_ref, w_ref, alog_ref, dtb_ref, tri_ref,
                   qkv_ref, z_ref, pool_ref, gate_ref, *, n_qkv, n_z, n_pool, n_heads):
    x = x_ref[0]
    xn = x * lax.rsqrt(jnp.mean(x * x, axis=-1, keepdims=True) + NORM_EPS) * g_ref[...]
    h = (xn * (1.0 + sc_ref[0]) + sh_ref[0]).astype(BF16)
    o1 = n_qkv
    o2 = o1 + n_z
    o3 = o2 + n_pool
    qkv_ref[0] = jnp.dot(h, w_ref[:, 0:o1], preferred_element_type=F32)
    z_ref[0] = jnp.dot(h, w_ref[:, o1:o2], preferred_element_type=F32)
    pool_ref[0] = jnp.dot(h, w_ref[:, o2:o3], preferred_element_type=F32).astype(pool_ref.dtype)
    pg = jnp.dot(h, w_ref[:, o3:], preferred_element_type=F32)
    nh2 = 2 * n_heads
    beta = 1.0 / (1.0 + jnp.exp(-pg[:, 0:nh2]))
    a = pg[:, nh2:] + dtb_ref[...]
    softplus = jnp.maximum(a, 0.0) + jnp.log(1.0 + jnp.exp(-jnp.abs(a)))
    g = -jnp.exp(alog_ref[...]) * softplus
    tm = g.shape[0]
    lane = lax.broadcasted_iota(jnp.int32, (CHUNK, nh2), 1)
    tri = tri_ref[...]
    for c in range(tm // CHUNK):
        gc = g[c * CHUNK:(c + 1) * CHUNK, :]
        pre = _dot_hi(tri, gc)
        tot = jnp.sum(gc, axis=0, keepdims=True)
        suf = tot - pre + gc
        gate_ref[0, c * CHUNK:(c + 1) * CHUNK, 0:nh2] = beta[c * CHUNK:(c + 1) * CHUNK, :]
        gate_ref[0, c * CHUNK:(c + 1) * CHUNK, nh2:] = jnp.where(lane < n_heads, pre, suf)


def _in_projection(x, sc, sh, g_pre, w_all, a_log, dt_bias, n_heads, tm):
    b, t, d = x.shape
    w = n_heads * HEAD_DIM
    n_qkv, n_z = 3 * w, w
    n_gate = 4 * n_heads
    n_pool = w_all.shape[1] - n_qkv - n_z - n_gate
    tri = jnp.asarray(np.tril(np.ones((CHUNK, CHUNK), np.float32)))
    kern = functools.partial(_inproj_kernel, n_qkv=n_qkv, n_z=n_z, n_pool=n_pool, n_heads=n_heads)
    row = lambda i, j: (i, j, 0)
    per_b = lambda i, j: (i, 0, 0)
    const = lambda i, j: (0, 0)
    return pl.pallas_call(
        kern,
        grid=(b, t // tm),
        in_specs=[pl.BlockSpec((1, tm, d), row),
                  pl.BlockSpec((1, 1, d), per_b),
                  pl.BlockSpec((1, 1, d), per_b),
                  pl.BlockSpec((1, d), const),
                  pl.BlockSpec(w_all.shape, const),
                  pl.BlockSpec((1, 2 * n_heads), const),
                  pl.BlockSpec((1, 2 * n_heads), const),
                  pl.BlockSpec((CHUNK, CHUNK), const)],
        out_specs=[pl.BlockSpec((1, tm, n_qkv), row),
                   pl.BlockSpec((1, tm, n_z), row),
                   pl.BlockSpec((1, tm, n_pool), row),
                   pl.BlockSpec((1, tm, n_gate), row)],
        out_shape=[jax.ShapeDtypeStruct((b, t, n_qkv), F32),
                   jax.ShapeDtypeStruct((b, t, n_z), F32),
                   jax.ShapeDtypeStruct((b, t, n_pool), F32),
                   jax.ShapeDtypeStruct((b, t, n_gate), F32)],
        compiler_params=_params("parallel", "parallel"),
        name="in_projection",
    )(x, sc, sh, g_pre.reshape(1, d), w_all, a_log.reshape(1, -1), dt_bias.reshape(1, -1), tri)


def _conv_silu(ref, cw_ref, c, n_chunks):
    r0 = pl.multiple_of(c * CHUNK, CHUNK)
    cur = ref[0, pl.ds(r0, CHUNK), :]
    prev = ref[0, pl.ds(pl.multiple_of(jnp.maximum(r0 - 8, 0), 8), 8), :]
    nxt = ref[0, pl.ds(pl.multiple_of(jnp.minimum(r0 + CHUNK, (n_chunks - 1) * CHUNK + CHUNK - 8), 8), 8), :]
    prev = jnp.where(c > 0, prev, 0.0)
    nxt = jnp.where(c < n_chunks - 1, nxt, 0.0)
    ext = jnp.concatenate([prev, cur, nxt], axis=0)
    width = cw_ref.shape[0]
    pad = width // 2
    acc = None
    for j in range(width):
        term = ext[8 - pad + j:8 - pad + j + CHUNK, :] * cw_ref[j:j + 1, :]
        acc = term if acc is None else acc + term
    return _silu(acc)


def _l2n(a):
    return a * lax.rsqrt(jnp.sum(a * a, axis=-1, keepdims=True) + NORM_EPS)


def _inv_unit(a):
    n = a.shape[0]
    ii = lax.broadcasted_iota(jnp.int32, (n, n), 0)
    jj = lax.broadcasted_iota(jnp.int32, (n, n), 1)
    m = -a
    p = jnp.where(ii == jj, 1.0, 0.0) + m
    steps = int(np.log2(n)) - 1
    for _ in range(steps):
        m = _dot_hi(m, m)
        p = p + _dot_hi(p, m)
    return p


def _chunk_step(s_ref, d, q, k, v, gcol, grow, with_out):
    c = k.shape[0]
    beta_c = gcol[:, d:d + 1]
    gc_c = gcol[:, 2 + d:3 + d]
    gc_r = grow[2 + d:3 + d, :]
    glast = gc_r[:, c - 1:c] if d == 0 else gc_r[:, 0:1]
    ii = lax.broadcasted_iota(jnp.int32, (c, c), 0)
    jj = lax.broadcasted_iota(jnp.int32, (c, c), 1)
    incl = (ii >= jj) if d == 0 else (ii <= jj)
    strict = (ii > jj) if d == 0 else (ii < jj)
    decay = jnp.where(incl, jnp.exp(jnp.where(incl, gc_c - gc_r, 0.0)), 0.0)
    kk = _dot_nt(k * beta_c, k)
    a = jnp.where(strict, kk * decay, 0.0)
    t = _inv_unit(a)
    e_c = jnp.exp(gc_c)
    rhs = jnp.concatenate([v * beta_c, k * (beta_c * e_c)], axis=1)
    uw = _dot(t, rhs)
    u = uw[:, :HEAD_DIM]
    w = uw[:, HEAD_DIM:]
    s = s_ref[d]
    kd = k * jnp.exp(glast - gc_c)
    v_new = u - _dot(w, s)
    o = None
    if with_out:
        attn = jnp.where(incl, _dot_nt(q, k) * decay, 0.0)
        o = _dot(q * e_c, s) + _dot(attn, v_new)
    s_ref[d] = s * jnp.exp(glast) + _dot_tn(kd, v_new)
    return o


def _delta_kernel(q_ref, k_ref, v_ref, kc_ref, vc_ref, cwq_ref, cwk_ref, cwv_ref,
                  gcol_ref, grow_ref, gcolc_ref, growc_ref, z_ref, onorm_ref,
                  out_ref, qn, kn, vn, of, ob, s_ref, *, n_x, n_c):
    def prep_ctx(c, carry):
        r0 = pl.multiple_of(c * CHUNK, CHUNK)
        kn[pl.ds(r0, CHUNK), :] = _l2n(_conv_silu(kc_ref, cwk_ref, c, n_c))
        vn[pl.ds(r0, CHUNK), :] = _conv_silu(vc_ref, cwv_ref, c, n_c)
        return carry

    def prep_x(c, carry):
        r0 = pl.multiple_of((c + n_c) * CHUNK, CHUNK)
        qn[pl.ds(r0, CHUNK), :] = _l2n(_conv_silu(q_ref, cwq_ref, c, n_x)) * (HEAD_DIM ** -0.5)
        kn[pl.ds(r0, CHUNK), :] = _l2n(_conv_silu(k_ref, cwk_ref, c, n_x))
        vn[pl.ds(r0, CHUNK), :] = _conv_silu(v_ref, cwv_ref, c, n_x)
        return carry

    lax.fori_loop(0, n_c, prep_ctx, 0)
    lax.fori_loop(0, n_x, prep_x, 0)
    s_ref[...] = jnp.zeros_like(s_ref)

    def ctx_body(i, carry):
        for d in (0, 1):
            c = i if d == 0 else n_c - 1 - i
            r0 = pl.multiple_of(c * CHUNK, CHUNK)
            _chunk_step(s_ref, d, None, kn[pl.ds(r0, CHUNK), :], vn[pl.ds(r0, CHUNK), :],
                        gcolc_ref[0, 0, pl.ds(r0, CHUNK), :], growc_ref[0, 0, c], False)
        return carry

    lax.fori_loop(0, n_c, ctx_body, 0)

    def x_body(i, carry):
        for d in (0, 1):
            c = i if d == 0 else n_x - 1 - i
            r0 = pl.multiple_of(c * CHUNK, CHUNK)
            rs = pl.multiple_of((c + n_c) * CHUNK, CHUNK)
            o = _chunk_step(s_ref, d, qn[pl.ds(rs, CHUNK), :], kn[pl.ds(rs, CHUNK), :],
                            vn[pl.ds(rs, CHUNK), :], gcol_ref[0, 0, pl.ds(r0, CHUNK), :],
                            grow_ref[0, 0, c], True)
            (of if d == 0 else ob)[pl.ds(r0, CHUNK), :] = o
        return carry

    lax.fori_loop(0, n_x, x_body, 0)

    def fin(c, carry):
        r0 = pl.multiple_of(c * CHUNK, CHUNK)
        o = of[pl.ds(r0, CHUNK), :] + ob[pl.ds(r0, CHUNK), :]
        on = o * lax.rsqrt(jnp.mean(o * o, axis=-1, keepdims=True) + NORM_EPS) * onorm_ref[...]
        out_ref[0, pl.ds(r0, CHUNK), :] = (on * _silu(z_ref[0, pl.ds(r0, CHUNK), :])).astype(out_ref.dtype)
        return carry

    lax.fori_loop(0, n_x, fin, 0)


def _head_gates(gates, n_heads):
    b, t, _ = gates.shape
    g = gates.reshape(b, t, 4, n_heads)
    col = jnp.transpose(g, (0, 3, 1, 2))
    row = jnp.transpose(g.reshape(b, t // CHUNK, CHUNK, 4, n_heads), (0, 4, 1, 3, 2))
    return col, row


def _delta_mixer(qkv, qkv_c, gates, gates_c, z, conv_qkv, o_norm, n_heads):
    b, t, _ = qkv.shape
    tc = qkv_c.shape[1]
    n_x, n_c = t // CHUNK, tc // CHUNK
    hd = HEAD_DIM
    width = conv_qkv.shape[0]
    gcol, grow = _head_gates(gates, n_heads)
    gcolc, growc = _head_gates(gates_c, n_heads)
    kern = functools.partial(_delta_kernel, n_x=n_x, n_c=n_c)
    nh = n_heads

    def colblk(off):
        return lambda i, h: (i, 0, off + h)

    def cwblk(off):
        return lambda i, h: (0, off + h)

    head4 = lambda i, h: (i, h, 0, 0)
    head5 = lambda i, h: (i, h, 0, 0, 0)
    return pl.pallas_call(
        kern,
        grid=(b, nh),
        in_specs=[pl.BlockSpec((1, t, hd), colblk(0)),
                  pl.BlockSpec((1, t, hd), colblk(nh)),
                  pl.BlockSpec((1, t, hd), colblk(2 * nh)),
                  pl.BlockSpec((1, tc, hd), colblk(nh)),
                  pl.BlockSpec((1, tc, hd), colblk(2 * nh)),
                  pl.BlockSpec((width, hd), cwblk(0)),
                  pl.BlockSpec((width, hd), cwblk(nh)),
                  pl.BlockSpec((width, hd), cwblk(2 * nh)),
                  pl.BlockSpec((1, 1, t, 4), head4),
                  pl.BlockSpec((1, 1, n_x, 4, CHUNK), head5),
                  pl.BlockSpec((1, 1, tc, 4), head4),
                  pl.BlockSpec((1, 1, n_c, 4, CHUNK), head5),
                  pl.BlockSpec((1, t, hd), colblk(0)),
                  pl.BlockSpec((1, hd), lambda i, h: (0, 0))],
        out_specs=pl.BlockSpec((1, t, hd), colblk(0)),
        out_shape=jax.ShapeDtypeStruct((b, t, nh * hd), BF16),
        scratch_shapes=[pltpu.VMEM((t + tc, hd), F32),
                        pltpu.VMEM((t + tc, hd), F32),
                        pltpu.VMEM((t + tc, hd), F32),
                        pltpu.VMEM((t, hd), F32),
                        pltpu.VMEM((t, hd), F32),
                        pltpu.VMEM((2, hd, hd), F32)],
        compiler_params=_params("parallel", "parallel"),
        name="delta_mixer",
    )(qkv, qkv, qkv, qkv_c, qkv_c, conv_qkv, conv_qkv, conv_qkv,
      gcol, grow, gcolc, growc, z, o_norm.reshape(1, hd))


def _pool_kernel(u_ref, band_ref, icnt_ref, w_ref, sc_ref, out_ref, pad, rs, *, rows, blk):
    g = pl.program_id(1)
    left = lax.shift_left(jnp.int32(1), g)
    win = 2 * left
    right = left - 1
    gw = GRID_W
    halo = 16 * gw
    t = rows * gw
    pad[0:halo, :] = jnp.zeros((halo, pad.shape[1]), F32)
    pad[halo + t:halo + t + halo, :] = jnp.zeros((halo, pad.shape[1]), F32)
    pad[halo:halo + t, :] = u_ref[0]

    def row_body(r, carry):
        def acc_body(j, acc):
            off = pl.multiple_of(halo + (r - left + j) * gw, gw)
            return acc + pad[pl.ds(off, gw), :]

        acc = lax.fori_loop(0, win, acc_body, jnp.zeros((gw, pad.shape[1]), F32))
        cnt = jnp.minimum(r + right, rows - 1) - jnp.maximum(r - left, 0) + 1
        rs[pl.ds(pl.multiple_of(r * gw, gw), gw), :] = acc / jnp.full(acc.shape, cnt, jnp.int32).astype(F32)
        return carry

    lax.fori_loop(0, rows, row_body, 0)

    def col_body(i, carry):
        r0 = pl.multiple_of(i * blk, blk)
        m = jnp.dot(band_ref[0], rs[pl.ds(r0, blk), :].astype(BF16),
                    preferred_element_type=F32) * icnt_ref[0]
        dlt = m - u_ref[0, pl.ds(r0, blk), :]
        y = jnp.dot(dlt.astype(BF16), w_ref[0], preferred_element_type=F32) * sc_ref[0]
        out_ref[0, pl.ds(r0, blk), :] = y.astype(out_ref.dtype)
        return carry

    lax.fori_loop(0, t // blk, col_body, 0)


def _pool_consts(blk):
    band = np.zeros((len(POOL_WINDOWS), blk, blk), np.float32)
    icnt = np.zeros((len(POOL_WINDOWS), blk, HEAD_DIM), np.float32)
    for gi, w in enumerate(POOL_WINDOWS):
        left = w // 2
        right = w - 1 - left
        for i in range(blk):
            base, col = (i // GRID_W) * GRID_W, i % GRID_W
            lo, hi = max(col - left, 0), min(col + right, GRID_W - 1)
            band[gi, i, base + lo:base + hi + 1] = 1.0
            icnt[gi, i, :] = 1.0 / (hi - lo + 1)
    return jnp.asarray(band, BF16), jnp.asarray(icnt)


def _pool_mixer(u, pool_w, pool_scale):
    b, t, p = u.shape
    n_g, gd = pool_w.shape[0], pool_w.shape[1]
    rows = t // GRID_W
    blk = 4 * GRID_W
    band, icnt = _pool_consts(blk)
    kern = functools.partial(_pool_kernel, rows=rows, blk=blk)
    grp = lambda i, g: (g, 0, 0)
    return pl.pallas_call(
        kern,
        grid=(b, n_g),
        in_specs=[pl.BlockSpec((1, t, gd), lambda i, g: (i, 0, g)),
                  pl.BlockSpec((1, blk, blk), grp),
                  pl.BlockSpec((1, blk, gd), grp),
                  pl.BlockSpec((1, gd, gd), grp),
                  pl.BlockSpec((1, 1, gd), grp)],
        out_specs=pl.BlockSpec((1, t, gd), lambda i, g: (i, 0, g)),
        out_shape=jax.ShapeDtypeStruct((b, t, p), BF16),
        scratch_shapes=[pltpu.VMEM((t + 32 * GRID_W, gd), F32),
                        pltpu.VMEM((t, gd), F32)],
        compiler_params=_params("parallel", "parallel"),
        name="pool_mixer",
    )(u, band, icnt, pool_w.astype(BF16), pool_scale.reshape(n_g, 1, gd))


def _rms(a, g):
    return a * lax.rsqrt(jnp.mean(a * a, axis=-1, keepdims=True) + NORM_EPS) * g


def _outproj_kernel(dn_ref, pool_ref, x_ref, w1_ref, w2_ref, gt_ref, sc_ref, sh_ref,
                    gpost_ref, gpre_ref, x1_ref, h_ref):
    y = (jnp.dot(dn_ref[0], w1_ref[...], preferred_element_type=F32)
         + jnp.dot(pool_ref[0], w2_ref[...], preferred_element_type=F32))
    x1 = x_ref[0] + gt_ref[0] * _rms(y, gpost_ref[...])
    x1_ref[0] = x1
    h_ref[0] = (_rms(x1, gpre_ref[...]) * (1.0 + sc_ref[0]) + sh_ref[0]).astype(h_ref.dtype)


def _out_projection(dn, pool, x, w_out, gt1, sc2, sh2, g_post, g_pre, tm):
    b, t, d = x.shape
    wd = dn.shape[-1]
    w1 = w_out[:wd].astype(BF16)
    w2 = w_out[wd:].astype(BF16)
    row = lambda i, j: (i, j, 0)
    per_b = lambda i, j: (i, 0, 0)
    const = lambda i, j: (0, 0)
    return pl.pallas_call(
        _outproj_kernel,
        grid=(b, t // tm),
        in_specs=[pl.BlockSpec((1, tm, wd), row),
                  pl.BlockSpec((1, tm, pool.shape[-1]), row),
                  pl.BlockSpec((1, tm, d), row),
                  pl.BlockSpec(w1.shape, const),
                  pl.BlockSpec(w2.shape, const),
                  pl.BlockSpec((1, 1, d), per_b),
                  pl.BlockSpec((1, 1, d), per_b),
                  pl.BlockSpec((1, 1, d), per_b),
                  pl.BlockSpec((1, d), const),
                  pl.BlockSpec((1, d), const)],
        out_specs=[pl.BlockSpec((1, tm, d), row), pl.BlockSpec((1, tm, d), row)],
        out_shape=[jax.ShapeDtypeStruct((b, t, d), F32), jax.ShapeDtypeStruct((b, t, d), BF16)],
        compiler_params=_params("parallel", "parallel"),
        name="out_projection",
    )(dn, pool, x, w1, w2, gt1, sc2, sh2, g_post.reshape(1, d), g_pre.reshape(1, d))


FFN_HALO = 16


def _ffn_kernel(h_ref, hp_ref, hn_ref, x1_ref, wgu_ref, cw_ref, wd_ref, gt_ref, gpost_ref,
                out_ref, hext, acc, *, fc, n_fc):
    j = pl.program_id(1)
    nj = pl.num_programs(1)
    tm = h_ref.shape[1]
    hal = FFN_HALO
    hext[0:hal, :] = jnp.where(j > 0, hp_ref[0], jnp.zeros_like(hp_ref[0]))
    hext[hal:hal + tm, :] = h_ref[0]
    hext[hal + tm:, :] = jnp.where(j < nj - 1, hn_ref[0], jnp.zeros_like(hn_ref[0]))
    acc[...] = jnp.zeros_like(acc)
    for c in range(n_fc):
        gu = jnp.dot(hext[...], wgu_ref[:, c * 2 * fc:(c + 1) * 2 * fc], preferred_element_type=F32)
        gate = gu[:, :fc]
        up = gu[hal:hal + tm, fc:]
        cw = cw_ref[:, c * fc:(c + 1) * fc]
        conv = (gate[hal - 1:hal - 1 + tm] * cw[0:1] + gate[hal:hal + tm] * cw[1:2]
                + gate[hal + 1:hal + 1 + tm] * cw[2:3])
        act = (_silu(conv) * up).astype(BF16)
        acc[...] += jnp.dot(act, wd_ref[c * fc:(c + 1) * fc, :], preferred_element_type=F32)
    out_ref[0] = x1_ref[0] + gt_ref[0] * _rms(acc[...], gpost_ref[...])


def _conv_ffn(h, x1, w_up, conv_ffn, w_down, gt2, g_post, tm, fc):
    b, t, d = x1.shape
    dff = w_down.shape[0]
    n_fc = dff // fc
    wg = w_up[:, :dff].reshape(d, n_fc, fc)
    wu = w_up[:, dff:].reshape(d, n_fc, fc)
    wgu = jnp.concatenate([wg, wu], axis=-1).reshape(d, 2 * dff).astype(BF16)
    hal = FFN_HALO
    nb = tm // hal
    last = t // hal - 1
    kern = functools.partial(_ffn_kernel, fc=fc, n_fc=n_fc)
    row = lambda i, j: (i, j, 0)
    per_b = lambda i, j: (i, 0, 0)
    const = lambda i, j: (0, 0)
    return pl.pallas_call(
        kern,
        grid=(b, t // tm),
        in_specs=[pl.BlockSpec((1, tm, d), row),
                  pl.BlockSpec((1, hal, d), lambda i, j: (i, jnp.maximum(j * nb - 1, 0), 0)),
                  pl.BlockSpec((1, hal, d), lambda i, j: (i, jnp.minimum((j + 1) * nb, last), 0)),
                  pl.BlockSpec((1, tm, d), row),
                  pl.BlockSpec(wgu.shape, const),
                  pl.BlockSpec(conv_ffn.shape, const),
                  pl.BlockSpec(w_down.shape, const),
                  pl.BlockSpec((1, 1, d), per_b),
                  pl.BlockSpec((1, d), const)],
        out_specs=pl.BlockSpec((1, tm, d), row),
        out_shape=jax.ShapeDtypeStruct((b, t, d), F32),
        scratch_shapes=[pltpu.VMEM((tm + 2 * hal, d), BF16), pltpu.VMEM((tm, d), F32)],
        compiler_params=_params("parallel", "parallel"),
        name="conv_ffn",
    )(h, h, h, x1, wgu, conv_ffn, w_down.astype(BF16), gt2, g_post.reshape(1, d))


def _layer(x, c, ctx, c_ctx, w_mod, b_mod, g_pre_mix, g_post_mix, g_pre_ffn, g_post_ffn,
           w_in, conv_qkv, a_log, dt_bias, o_norm, pool_w, pool_scale, w_out, w_up,
           conv_ffn, w_down):
    b, t, d = x.shape
    n_heads = a_log.shape[-1]
    w = n_heads * HEAD_DIM
    n_gate = 4 * n_heads

    rows = 16
    c_all = jnp.zeros((rows, d), F32).at[:b].set(c).at[b].set(c_ctx)
    m = _modulation(c_all, w_mod, b_mod)
    sh1, sc1, gt1, sh2, sc2, gt2 = [m[:b, None, i * d:(i + 1) * d] for i in range(N_MOD)]
    csh1 = jnp.broadcast_to(m[b, 0 * d:1 * d], (b, 1, d))
    csc1 = jnp.broadcast_to(m[b, 1 * d:2 * d], (b, 1, d))

    off_g = 4 * w
    w_all = jnp.concatenate([w_in[:, :off_g], w_in[:, off_g + n_gate:],
                             w_in[:, off_g:off_g + n_gate]], axis=1).astype(BF16)
    tm = min(512, t)
    qkv, z, pool_u, gates = _in_projection(x, sc1, sh1, g_pre_mix, w_all, a_log, dt_bias,
                                           n_heads, tm)
    qkv_c, _, _, gates_c = _in_projection(ctx, csc1, csh1, g_pre_mix, w_all, a_log, dt_bias,
                                          n_heads, ctx.shape[1])
    dn = _delta_mixer(qkv, qkv_c, gates, gates_c, z, conv_qkv, o_norm, n_heads)
    pool = _pool_mixer(pool_u, pool_w, pool_scale)
    x1, h2 = _out_projection(dn, pool, x, w_out, gt1, sc2, sh2, g_post_mix, g_pre_ffn, tm)
    return _conv_ffn(h2, x1, w_up, conv_ffn, w_down, gt2, g_post_ffn, tm, 256)


def kernel(x, c, ctx, c_ctx, w_mod, b_mod, g_pre_mix, g_post_mix, g_pre_ffn, g_post_ffn, w_in,
           conv_qkv, a_log, dt_bias, o_norm, pool_w, pool_scale, w_out, w_up, conv_ffn, w_down):
    depth = w_mod.shape[0]
    assert depth == 1, "context-stream update between layers is not implemented"
    return _layer(x, c, ctx, c_ctx, w_mod[0], b_mod[0], g_pre_mix[0], g_post_mix[0],
                  g_pre_ffn[0], g_post_ffn[0], w_in[0], conv_qkv[0], a_log[0], dt_bias[0],
                  o_norm[0], pool_w[0], pool_scale[0], w_out[0], w_up[0], conv_ffn[0], w_down[0])
```

```python
import functools

import numpy as np
import jax
import jax.numpy as jnp
from jax import lax
from jax.experimental import pallas as pl
from jax.experimental.pallas import tpu as pltpu

NORM_EPS = 1e-6
GRID_W = 64
POOL_WINDOWS = (2, 4, 8, 16)
N_MOD = 6
CHUNK = 128
HEAD_DIM = 128
VMEM_LIMIT = 56 * 1024 * 1024

F32 = jnp.float32
BF16 = jnp.bfloat16
HIGHEST = lax.Precision.HIGHEST


def _silu(a):
    return a * (1.0 / (1.0 + jnp.exp(-a)))


def _dot(a, b):
    return jnp.dot(a.astype(BF16), b.astype(BF16), preferred_element_type=F32)


def _dot_hi(a, b):
    return jnp.dot(a, b, preferred_element_type=F32, precision=HIGHEST)


def _params(*sem):
    return pltpu.CompilerParams(dimension_semantics=sem, vmem_limit_bytes=VMEM_LIMIT)


def _mod_kernel(c_ref, w_ref, b_ref, o_ref):
    o_ref[...] = _dot_hi(_silu(c_ref[...]), w_ref[...]) + b_ref[...]


def _modulation(c_all, w_mod, b_mod):
    r, d = c_all.shape
    n = w_mod.shape[1]
    tn = 512
    return pl.pallas_call(
        _mod_kernel,
        grid=(n // tn,),
        in_specs=[pl.BlockSpec((r, d), lambda j: (0, 0)),
                  pl.BlockSpec((d, tn), lambda j: (0, j)),
                  pl.BlockSpec((1, tn), lambda j: (0, j))],
        out_specs=pl.BlockSpec((r, tn), lambda j: (0, j)),
        out_shape=jax.ShapeDtypeStruct((r, n), F32),
        compiler_params=_params("parallel"),
        name="modulation",
    )(c_all, w_mod, b_mod.reshape(1, n))


def _inproj_kernel(x_ref, sc_ref, sh_ref, g_ref, w_ref, alog_ref, dtb_ref, tri_ref,
                   qkv_ref, z_ref, pool_ref, gate_ref, *, n_qkv, n_z, n_pool, n_heads):
    x = x_ref[0]
    xn = x * lax.rsqrt(jnp.mean(x * x, axis=-1, keepdims=True) + NORM_EPS) * g_ref[...]
    h = (xn * (1.0 + sc_ref[0]) + sh_ref[0]).astype(BF16)
    o1 = n_qkv
    o2 = o1 + n_z
    o3 = o2 + n_pool
    qkv_ref[0] = jnp.dot(h, w_ref[:, 0:o1], preferred_element_type=F32).astype(qkv_ref.dtype)
    z_ref[0] = jnp.dot(h, w_ref[:, o1:o2], preferred_element_type=F32).astype(z_ref.dtype)
    pool_ref[0] = jnp.dot(h, w_ref[:, o2:o3], preferred_element_type=F32).astype(pool_ref.dtype)
    pg = jnp.dot(h, w_ref[:, o3:], preferred_element_type=F32)
    nh2 = 2 * n_heads
    beta = 1.0 / (1.0 + jnp.exp(-pg[:, 0:nh2]))
    a = pg[:, nh2:] + dtb_ref[...]
    softplus = jnp.maximum(a, 0.0) + jnp.log(1.0 + jnp.exp(-jnp.abs(a)))
    g = -jnp.exp(alog_ref[...]) * softplus
    tm = g.shape[0]
    lane = lax.broadcasted_iota(jnp.int32, (CHUNK, nh2), 1)
    tri = tri_ref[...]
    for c in range(tm // CHUNK):
        gc = g[c * CHUNK:(c + 1) * CHUNK, :]
        pre = _dot_hi(tri, gc)
        tot = jnp.sum(gc, axis=0, keepdims=True)
        suf = tot - pre + gc
        gate_ref[0, c * CHUNK:(c + 1) * CHUNK, 0:nh2] = beta[c * CHUNK:(c + 1) * CHUNK, :]
        gate_ref[0, c * CHUNK:(c + 1) * CHUNK, nh2:] = jnp.where(lane < n_heads, pre, suf)


def _in_projection(x, sc, sh, g_pre, w_all, a_log, dt_bias, n_heads, tm):
    b, t, d = x.shape
    w = n_heads * HEAD_DIM
    n_qkv, n_z = 3 * w, w
    n_gate = 4 * n_heads
    n_pool = w_all.shape[1] - n_qkv - n_z - n_gate
    tri = jnp.asarray(np.tril(np.ones((CHUNK, CHUNK), np.float32)))
    kern = functools.partial(_inproj_kernel, n_qkv=n_qkv, n_z=n_z, n_pool=n_pool, n_heads=n_heads)
    row = lambda i, j: (i, j, 0)
    per_b = lambda i, j: (i, 0, 0)
    const = lambda i, j: (0, 0)
    return pl.pallas_call(
        kern,
        grid=(b, t // tm),
        in_specs=[pl.BlockSpec((1, tm, d), row),
                  pl.BlockSpec((1, 1, d), per_b),
                  pl.BlockSpec((1, 1, d), per_b),
                  pl.BlockSpec((1, d), const),
                  pl.BlockSpec(w_all.shape, const),
                  pl.BlockSpec((1, 2 * n_heads), const),
                  pl.BlockSpec((1, 2 * n_heads), const),
                  pl.BlockSpec((CHUNK, CHUNK), const)],
        out_specs=[pl.BlockSpec((1, tm, n_qkv), row),
                   pl.BlockSpec((1, tm, n_z), row),
                   pl.BlockSpec((1, tm, n_pool), row),
                   pl.BlockSpec((1, tm, n_gate), row)],
        out_shape=[jax.ShapeDtypeStruct((b, t, n_qkv), BF16),
                   jax.ShapeDtypeStruct((b, t, n_z), BF16),
                   jax.ShapeDtypeStruct((b, t, n_pool), BF16),
                   jax.ShapeDtypeStruct((b, t, n_gate), F32)],
        compiler_params=_params("parallel", "parallel"),
        name="in_projection",
    )(x, sc, sh, g_pre.reshape(1, d), w_all, a_log.reshape(1, -1), dt_bias.reshape(1, -1), tri)


CONV_HALO = 16


def _conv_silu(ref, cw_ref, c, n_chunks):
    hal = CONV_HALO
    r0 = pl.multiple_of(c * CHUNK, CHUNK)
    cur = ref[0, pl.ds(r0, CHUNK), :].astype(F32)
    p0 = pl.multiple_of(jnp.maximum(r0 - hal, 0), hal)
    n0 = pl.multiple_of(jnp.minimum(r0 + CHUNK, n_chunks * CHUNK - hal), hal)
    prev = jnp.where(c > 0, ref[0, pl.ds(p0, hal), :].astype(F32), 0.0)
    nxt = jnp.where(c < n_chunks - 1, ref[0, pl.ds(n0, hal), :].astype(F32), 0.0)
    ext = jnp.concatenate([prev, cur, nxt], axis=0)
    width = cw_ref.shape[0]
    pad = width // 2
    acc = None
    for j in range(width):
        term = ext[hal - pad + j:hal - pad + j + CHUNK, :] * cw_ref[j:j + 1, :]
        acc = term if acc is None else acc + term
    return _silu(acc)


def _l2n(a):
    return a * lax.rsqrt(jnp.sum(a * a, axis=-1, keepdims=True) + NORM_EPS)


INV_BASE = 16
LOCAL_UNROLL = 4


def _inv_unit(mats):
    n = mats[0].shape[0]
    ii = lax.broadcasted_iota(jnp.int32, (n, n), 0)
    jj = lax.broadcasted_iota(jnp.int32, (n, n), 1)

    def same_block(bits):
        return lax.shift_right_logical(ii, bits) == lax.shift_right_logical(jj, bits)

    bits = int(np.log2(INV_BASE))
    base = same_block(bits)
    eye = jnp.where(ii == jj, 1.0, 0.0)
    ms = [jnp.where(base, -a, 0.0) for a in mats]
    ps = [eye + m for m in ms]
    for _ in range(bits - 1):
        ms = [_dot(m, m) for m in ms]
        ps = [p + _dot(p, m) for p, m in zip(ps, ms)]
    while (1 << bits) < n:
        lvl = same_block(bits + 1) & jnp.logical_not(same_block(bits))
        tmp = [_dot(jnp.where(lvl, a, 0.0), p) for a, p in zip(mats, ps)]
        ps = [p - _dot(p, t) for p, t in zip(ps, tmp)]
        bits += 1
    return ps


def _chunk_local(chunks):
    c = chunks[0][1].shape[0]
    ii = lax.broadcasted_iota(jnp.int32, (c, c), 0)
    jj = lax.broadcasted_iota(jnp.int32, (c, c), 1)
    mats, rhs, ktd, attn, qe = [], [], [], [], []
    for q, k, kt, v, kk, qk, gcol, grow in chunks:
        for d in (0, 1):
            incl = (ii >= jj) if d == 0 else (ii <= jj)
            strict = (ii > jj) if d == 0 else (ii < jj)
            beta_c = gcol[:, d:d + 1]
            gc_c = gcol[:, 2 + d:3 + d]
            gc_r = grow[2 + d:3 + d, :]
            glast = gc_r[:, c - 1:c] if d == 0 else gc_r[:, 0:1]
            decay = jnp.where(incl, jnp.exp(jnp.where(incl, gc_c - gc_r, 0.0)), 0.0)
            e_c = jnp.exp(gc_c)
            mats.append(jnp.where(strict, kk * decay * beta_c, 0.0))
            rhs.append(jnp.concatenate([k * (beta_c * e_c), v * beta_c], axis=1).astype(BF16))
            ktd.append((kt * jnp.exp(glast - gc_r)).astype(BF16))
            attn.append(None if q is None else jnp.where(incl, qk * decay, 0.0).astype(BF16))
            qe.append(None if q is None else q * e_c)
    ts = _inv_unit(mats)
    wus = [_dot(t, r) for t, r in zip(ts, rhs)]
    kns = [_dot(kd, wu) for kd, wu in zip(ktd, wus)]
    qos = [None if a is None else _dot(a, wu) for a, wu in zip(attn, wus)]
    out = []
    for kn_, qo, qe_ in zip(kns, qos, qe):
        if qo is None:
            out.append((kn_[:, :HEAD_DIM], kn_[:, HEAD_DIM:], None, None))
        else:
            out.append((kn_[:, :HEAD_DIM], kn_[:, HEAD_DIM:], qe_ - qo[:, :HEAD_DIM], qo[:, HEAD_DIM:]))
    return out


def _delta_kernel(q_ref, k_ref, v_ref, kc_ref, vc_ref, cwq_ref, cwk_ref, cwv_ref,
                  gcol_ref, grow_ref, gcolc_ref, growc_ref, z_ref, onorm_ref,
                  out_ref, kq_scr, n_scr, of, ob, s_ref, *, n_x, n_c):
    n_tot = n_x + n_c
    hd = HEAD_DIM

    def local(src_q, src_k, src_v, gcol_r, grow_r, c0, count, n_seq, slot0):
        chunks = []
        for j in range(count):
            c = c0 + j
            r0 = pl.multiple_of(c * CHUNK, CHUNK)
            k = _l2n(_conv_silu(src_k, cwk_ref, c, n_seq))
            v = _conv_silu(src_v, cwv_ref, c, n_seq)
            kt = k.T
            q = None if src_q is None else _l2n(_conv_silu(src_q, cwq_ref, c, n_seq)) * (hd ** -0.5)
            chunks.append([q, k, kt, v, None, None, gcol_r[0, 0, pl.ds(r0, CHUNK), :], grow_r[0, 0, c]])
        for ch in chunks:
            ch[4] = _dot(ch[1], ch[2])
        for ch in chunks:
            ch[5] = None if ch[0] is None else _dot(ch[0], ch[2])
        res = _chunk_local(chunks)
        for j in range(count):
            r0 = pl.multiple_of((c0 + j) * CHUNK, CHUNK)
            for d in (0, 1):
                kmat, nmat, qmat, omat = res[2 * j + d]
                idx = d * n_tot + slot0 + j
                kq_scr[idx, 0:hd, :] = kmat.astype(BF16)
                n_scr[idx] = nmat.astype(BF16)
                if qmat is not None:
                    kq_scr[idx, hd:, :] = qmat.astype(BF16)
                    (of if d == 0 else ob)[pl.ds(r0, CHUNK), :] = omat

    u_c = min(LOCAL_UNROLL, n_c)
    u_x = min(LOCAL_UNROLL, n_x)

    def local_ctx(g, carry):
        local(None, kc_ref, vc_ref, gcolc_ref, growc_ref, g * u_c, u_c, n_c, g * u_c)
        return carry

    def local_x(g, carry):
        local(q_ref, k_ref, v_ref, gcol_ref, grow_ref, g * u_x, u_x, n_x, n_c + g * u_x)
        return carry

    lax.fori_loop(0, n_c // u_c, local_ctx, 0)
    lax.fori_loop(0, n_x // u_x, local_x, 0)
    s_ref[...] = jnp.zeros_like(s_ref)

    def state_decay(grow, d):
        tot = grow[2 + d:3 + d, CHUNK - 1:CHUNK] if d == 0 else grow[2 + d:3 + d, 0:1]
        return jnp.exp(tot)

    def seq_ctx(i, carry):
        for d in (0, 1):
            c = i if d == 0 else n_c - 1 - i
            idx = d * n_tot + c
            s = s_ref[d]
            ks = jnp.dot(kq_scr[idx, 0:hd, :], s.astype(BF16), preferred_element_type=F32)
            s_ref[d] = s * state_decay(growc_ref[0, 0, c], d) - ks + n_scr[idx].astype(F32)
        return carry

    def seq_x(i, carry):
        for d in (0, 1):
            c = i if d == 0 else n_x - 1 - i
            r0 = pl.multiple_of(c * CHUNK, CHUNK)
            idx = d * n_tot + n_c + c
            s = s_ref[d]
            kqs = jnp.dot(kq_scr[idx], s.astype(BF16), preferred_element_type=F32)
            o_ref = of if d == 0 else ob
            o_ref[pl.ds(r0, CHUNK), :] = o_ref[pl.ds(r0, CHUNK), :] + kqs[hd:, :]
            s_ref[d] = s * state_decay(grow_ref[0, 0, c], d) - kqs[0:hd, :] + n_scr[idx].astype(F32)
        return carry

    lax.fori_loop(0, n_c, seq_ctx, 0)
    lax.fori_loop(0, n_x, seq_x, 0)

    def fin(c, carry):
        r0 = pl.multiple_of(c * CHUNK, CHUNK)
        o = of[pl.ds(r0, CHUNK), :] + ob[pl.ds(r0, CHUNK), :]
        on = o * lax.rsqrt(jnp.mean(o * o, axis=-1, keepdims=True) + NORM_EPS) * onorm_ref[...]
        zg = _silu(z_ref[0, pl.ds(r0, CHUNK), :].astype(F32))
        out_ref[0, pl.ds(r0, CHUNK), :] = (on * zg).astype(out_ref.dtype)
        return carry

    lax.fori_loop(0, n_x, fin, 0)


def _head_gates(gates, n_heads):
    b, t, _ = gates.shape
    g = gates.reshape(b, t, 4, n_heads)
    col = jnp.transpose(g, (0, 3, 1, 2))
    row = jnp.transpose(g.reshape(b, t // CHUNK, CHUNK, 4, n_heads), (0, 4, 1, 3, 2))
    return col, row


def _delta_mixer(qkv, qkv_c, gates, gates_c, z, conv_qkv, o_norm, n_heads):
    b, t, _ = qkv.shape
    tc = qkv_c.shape[1]
    n_x, n_c = t // CHUNK, tc // CHUNK
    hd = HEAD_DIM
    width = conv_qkv.shape[0]
    gcol, grow = _head_gates(gates, n_heads)
    gcolc, growc = _head_gates(gates_c, n_heads)
    kern = functools.partial(_delta_kernel, n_x=n_x, n_c=n_c)
    nh = n_heads

    def colblk(off):
        return lambda i, h: (i, 0, off + h)

    def cwblk(off):
        return lambda i, h: (0, off + h)

    head4 = lambda i, h: (i, h, 0, 0)
    head5 = lambda i, h: (i, h, 0, 0, 0)
    return pl.pallas_call(
        kern,
        grid=(b, nh),
        in_specs=[pl.BlockSpec((1, t, hd), colblk(0)),
                  pl.BlockSpec((1, t, hd), colblk(nh)),
                  pl.BlockSpec((1, t, hd), colblk(2 * nh)),
                  pl.BlockSpec((1, tc, hd), colblk(nh)),
                  pl.BlockSpec((1, tc, hd), colblk(2 * nh)),
                  pl.BlockSpec((width, hd), cwblk(0)),
                  pl.BlockSpec((width, hd), cwblk(nh)),
                  pl.BlockSpec((width, hd), cwblk(2 * nh)),
                  pl.BlockSpec((1, 1, t, 4), head4),
                  pl.BlockSpec((1, 1, n_x, 4, CHUNK), head5),
                  pl.BlockSpec((1, 1, tc, 4), head4),
                  pl.BlockSpec((1, 1, n_c, 4, CHUNK), head5),
                  pl.BlockSpec((1, t, hd), colblk(0)),
                  pl.BlockSpec((1, hd), lambda i, h: (0, 0))],
        out_specs=pl.BlockSpec((1, t, hd), colblk(0)),
        out_shape=jax.ShapeDtypeStruct((b, t, nh * hd), BF16),
        scratch_shapes=[pltpu.VMEM((2 * (n_x + n_c), 2 * hd, hd), BF16),
                        pltpu.VMEM((2 * (n_x + n_c), hd, hd), BF16),
                        pltpu.VMEM((t, hd), F32),
                        pltpu.VMEM((t, hd), F32),
                        pltpu.VMEM((2, hd, hd), F32)],
        compiler_params=_params("parallel", "parallel"),
        name="delta_mixer",
    )(qkv, qkv, qkv, qkv_c, qkv_c, conv_qkv, conv_qkv, conv_qkv,
      gcol, grow, gcolc, growc, z, o_norm.reshape(1, hd))


def _pool_kernel(u_ref, band_ref, icnt_ref, w_ref, sc_ref, out_ref, pad, rs, *, rows, blk):
    g = pl.program_id(1)
    left = lax.shift_left(jnp.int32(1), g)
    win = 2 * left
    right = left - 1
    gw = GRID_W
    halo = 16 * gw
    t = rows * gw
    pad[0:halo, :] = jnp.zeros((halo, pad.shape[1]), F32)
    pad[halo + t:halo + t + halo, :] = jnp.zeros((halo, pad.shape[1]), F32)
    pad[halo:halo + t, :] = u_ref[0].astype(F32)

    def row_body(r, carry):
        def acc_body(j, acc):
            off = pl.multiple_of(halo + (r - left + j) * gw, gw)
            return acc + pad[pl.ds(off, gw), :]

        acc = lax.fori_loop(0, win, acc_body, jnp.zeros((gw, pad.shape[1]), F32))
        cnt = jnp.minimum(r + right, rows - 1) - jnp.maximum(r - left, 0) + 1
        rs[pl.ds(pl.multiple_of(r * gw, gw), gw), :] = acc / jnp.full(acc.shape, cnt, jnp.int32).astype(F32)
        return carry

    lax.fori_loop(0, rows, row_body, 0)

    def col_body(i, carry):
        r0 = pl.multiple_of(i * blk, blk)
        m = jnp.dot(band_ref[0], rs[pl.ds(r0, blk), :].astype(BF16),
                    preferred_element_type=F32) * icnt_ref[0]
        dlt = m - u_ref[0, pl.ds(r0, blk), :].astype(F32)
        y = jnp.dot(dlt.astype(BF16), w_ref[0], preferred_element_type=F32) * sc_ref[0]
        out_ref[0, pl.ds(r0, blk), :] = y.astype(out_ref.dtype)
        return carry

    lax.fori_loop(0, t // blk, col_body, 0)


def _pool_consts(blk):
    band = np.zeros((len(POOL_WINDOWS), blk, blk), np.float32)
    icnt = np.zeros((len(POOL_WINDOWS), blk, HEAD_DIM), np.float32)
    for gi, w in enumerate(POOL_WINDOWS):
        left = w // 2
        right = w - 1 - left
        for i in range(blk):
            base, col = (i // GRID_W) * GRID_W, i % GRID_W
            lo, hi = max(col - left, 0), min(col + right, GRID_W - 1)
            band[gi, i, base + lo:base + hi + 1] = 1.0
            icnt[gi, i, :] = 1.0 / (hi - lo + 1)
    return jnp.asarray(band, BF16), jnp.asarray(icnt)


def _pool_mixer(u, pool_w, pool_scale):
    b, t, p = u.shape
    n_g, gd = pool_w.shape[0], pool_w.shape[1]
    rows = t // GRID_W
    blk = 4 * GRID_W
    band, icnt = _pool_consts(blk)
    kern = functools.partial(_pool_kernel, rows=rows, blk=blk)
    grp = lambda i, g: (g, 0, 0)
    return pl.pallas_call(
        kern,
        grid=(b, n_g),
        in_specs=[pl.BlockSpec((1, t, gd), lambda i, g: (i, 0, g)),
                  pl.BlockSpec((1, blk, blk), grp),
                  pl.BlockSpec((1, blk, gd), grp),
                  pl.BlockSpec((1, gd, gd), grp),
                  pl.BlockSpec((1, 1, gd), grp)],
        out_specs=pl.BlockSpec((1, t, gd), lambda i, g: (i, 0, g)),
        out_shape=jax.ShapeDtypeStruct((b, t, p), BF16),
        scratch_shapes=[pltpu.VMEM((t + 32 * GRID_W, gd), F32),
                        pltpu.VMEM((t, gd), F32)],
        compiler_params=_params("parallel", "parallel"),
        name="pool_mixer",
    )(u, band, icnt, pool_w.astype(BF16), pool_scale.reshape(n_g, 1, gd))


def _rms(a, g):
    return a * lax.rsqrt(jnp.mean(a * a, axis=-1, keepdims=True) + NORM_EPS) * g


def _outproj_kernel(dn_ref, pool_ref, x_ref, w1_ref, w2_ref, gt_ref, sc_ref, sh_ref,
                    gpost_ref, gpre_ref, x1_ref, h_ref):
    y = (jnp.dot(dn_ref[0], w1_ref[...], preferred_element_type=F32)
         + jnp.dot(pool_ref[0], w2_ref[...], preferred_element_type=F32))
    x1 = x_ref[0] + gt_ref[0] * _rms(y, gpost_ref[...])
    x1_ref[0] = x1
    h_ref[0] = (_rms(x1, gpre_ref[...]) * (1.0 + sc_ref[0]) + sh_ref[0]).astype(h_ref.dtype)


def _out_projection(dn, pool, x, w_out, gt1, sc2, sh2, g_post, g_pre, tm):
    b, t, d = x.shape
    wd = dn.shape[-1]
    w1 = w_out[:wd].astype(BF16)
    w2 = w_out[wd:].astype(BF16)
    row = lambda i, j: (i, j, 0)
    per_b = lambda i, j: (i, 0, 0)
    const = lambda i, j: (0, 0)
    return pl.pallas_call(
        _outproj_kernel,
        grid=(b, t // tm),
        in_specs=[pl.BlockSpec((1, tm, wd), row),
                  pl.BlockSpec((1, tm, pool.shape[-1]), row),
                  pl.BlockSpec((1, tm, d), row),
                  pl.BlockSpec(w1.shape, const),
                  pl.BlockSpec(w2.shape, const),
                  pl.BlockSpec((1, 1, d), per_b),
                  pl.BlockSpec((1, 1, d), per_b),
                  pl.BlockSpec((1, 1, d), per_b),
                  pl.BlockSpec((1, d), const),
                  pl.BlockSpec((1, d), const)],
        out_specs=[pl.BlockSpec((1, tm, d), row), pl.BlockSpec((1, tm, d), row)],
        out_shape=[jax.ShapeDtypeStruct((b, t, d), F32), jax.ShapeDtypeStruct((b, t, d), BF16)],
        compiler_params=_params("parallel", "parallel"),
        name="out_projection",
    )(dn, pool, x, w1, w2, gt1, sc2, sh2, g_post.reshape(1, d), g_pre.reshape(1, d))


FFN_HALO = 16


def _ffn_kernel(h_ref, hp_ref, hn_ref, x1_ref, wgu_ref, cw_ref, wd_ref, gt_ref, gpost_ref,
                out_ref, hext, acc, *, fc, n_fc):
    j = pl.program_id(1)
    nj = pl.num_programs(1)
    tm = h_ref.shape[1]
    hal = FFN_HALO
    hext[0:hal, :] = jnp.where(j > 0, hp_ref[0], jnp.zeros_like(hp_ref[0]))
    hext[hal:hal + tm, :] = h_ref[0]
    hext[hal + tm:, :] = jnp.where(j < nj - 1, hn_ref[0], jnp.zeros_like(hn_ref[0]))
    acc[...] = jnp.zeros_like(acc)
    for c in range(n_fc):
        gu = jnp.dot(hext[...], wgu_ref[:, c * 2 * fc:(c + 1) * 2 * fc], preferred_element_type=F32)
        gate = gu[:, :fc]
        up = gu[hal:hal + tm, fc:]
        cw = cw_ref[:, c * fc:(c + 1) * fc]
        conv = (gate[hal - 1:hal - 1 + tm] * cw[0:1] + gate[hal:hal + tm] * cw[1:2]
                + gate[hal + 1:hal + 1 + tm] * cw[2:3])
        act = (_silu(conv) * up).astype(BF16)
        acc[...] += jnp.dot(act, wd_ref[c * fc:(c + 1) * fc, :], preferred_element_type=F32)
    out_ref[0] = x1_ref[0] + gt_ref[0] * _rms(acc[...], gpost_ref[...])


def _conv_ffn(h, x1, w_up, conv_ffn, w_down, gt2, g_post, tm, fc):
    b, t, d = x1.shape
    dff = w_down.shape[0]
    n_fc = dff // fc
    wg = w_up[:, :dff].reshape(d, n_fc, fc)
    wu = w_up[:, dff:].reshape(d, n_fc, fc)
    wgu = jnp.concatenate([wg, wu], axis=-1).reshape(d, 2 * dff).astype(BF16)
    hal = FFN_HALO
    nb = tm // hal
    last = t // hal - 1
    kern = functools.partial(_ffn_kernel, fc=fc, n_fc=n_fc)
    row = lambda i, j: (i, j, 0)
    per_b = lambda i, j: (i, 0, 0)
    const = lambda i, j: (0, 0)
    return pl.pallas_call(
        kern,
        grid=(b, t // tm),
        in_specs=[pl.BlockSpec((1, tm, d), row),
                  pl.BlockSpec((1, hal, d), lambda i, j: (i, jnp.maximum(j * nb - 1, 0), 0)),
                  pl.BlockSpec((1, hal, d), lambda i, j: (i, jnp.minimum((j + 1) * nb, last), 0)),
                  pl.BlockSpec((1, tm, d), row),
                  pl.BlockSpec(wgu.shape, const),
                  pl.BlockSpec(conv_ffn.shape, const),
                  pl.BlockSpec(w_down.shape, const),
                  pl.BlockSpec((1, 1, d), per_b),
                  pl.BlockSpec((1, d), const)],
        out_specs=pl.BlockSpec((1, tm, d), row),
        out_shape=jax.ShapeDtypeStruct((b, t, d), F32),
        scratch_shapes=[pltpu.VMEM((tm + 2 * hal, d), BF16), pltpu.VMEM((tm, d), F32)],
        compiler_params=_params("parallel", "parallel"),
        name="conv_ffn",
    )(h, h, h, x1, wgu, conv_ffn, w_down.astype(BF16), gt2, g_post.reshape(1, d))


def _layer(x, c, ctx, c_ctx, w_mod, b_mod, g_pre_mix, g_post_mix, g_pre_ffn, g_post_ffn,
           w_in, conv_qkv, a_log, dt_bias, o_norm, pool_w, pool_scale, w_out, w_up,
           conv_ffn, w_down):
    b, t, d = x.shape
    n_heads = a_log.shape[-1]
    w = n_heads * HEAD_DIM
    n_gate = 4 * n_heads

    rows = 16
    c_all = jnp.zeros((rows, d), F32).at[:b].set(c).at[b].set(c_ctx)
    m = _modulation(c_all, w_mod, b_mod)
    sh1, sc1, gt1, sh2, sc2, gt2 = [m[:b, None, i * d:(i + 1) * d] for i in range(N_MOD)]
    csh1 = jnp.broadcast_to(m[b, 0 * d:1 * d], (b, 1, d))
    csc1 = jnp.broadcast_to(m[b, 1 * d:2 * d], (b, 1, d))

    off_g = 4 * w
    w_all = jnp.concatenate([w_in[:, :off_g], w_in[:, off_g + n_gate:],
                             w_in[:, off_g:off_g + n_gate]], axis=1).astype(BF16)
    tm = min(512, t)
    qkv, z, pool_u, gates = _in_projection(x, sc1, sh1, g_pre_mix, w_all, a_log, dt_bias,
                                           n_heads, tm)
    qkv_c, _, _, gates_c = _in_projection(ctx, csc1, csh1, g_pre_mix, w_all, a_log, dt_bias,
                                          n_heads, ctx.shape[1])
    dn = _delta_mixer(qkv, qkv_c, gates, gates_c, z, conv_qkv, o_norm, n_heads)
    pool = _pool_mixer(pool_u, pool_w, pool_scale)
    x1, h2 = _out_projection(dn, pool, x, w_out, gt1, sc2, sh2, g_post_mix, g_pre_ffn, tm)
    return _conv_ffn(h2, x1, w_up, conv_ffn, w_down, gt2, g_post_ffn, tm, 256)


def kernel(x, c, ctx, c_ctx, w_mod, b_mod, g_pre_mix, g_post_mix, g_pre_ffn, g_post_ffn, w_in,
           conv_qkv, a_log, dt_bias, o_norm, pool_w, pool_scale, w_out, w_up, conv_ffn, w_down):
    depth = w_mod.shape[0]
    assert depth == 1, "context-stream update between layers is not implemented"
    return _layer(x, c, ctx, c_ctx, w_mod[0], b_mod[0], g_pre_mix[0], g_post_mix[0],
                  g_pre_ffn[0], g_post_ffn[0], w_in[0], conv_qkv[0], a_log[0], dt_bias[0],
                  o_norm[0], pool_w[0], pool_scale[0], w_out[0], w_up[0], conv_ffn[0], w_down[0])
```

```python
import functools

import numpy as np
import jax
import jax.numpy as jnp
from jax import lax
from jax.experimental import pallas as pl
from jax.experimental.pallas import tpu as pltpu

NORM_EPS = 1e-6
GRID_W = 64
POOL_WINDOWS = (2, 4, 8, 16)
N_MOD = 6
CHUNK = 128
HEAD_DIM = 128
VMEM_LIMIT = 56 * 1024 * 1024

F32 = jnp.float32
BF16 = jnp.bfloat16
HIGHEST = lax.Precision.HIGHEST


def _silu(a):
    return a * (1.0 / (1.0 + jnp.exp(-a)))


def _dot(a, b):
    return jnp.dot(a.astype(BF16), b.astype(BF16), preferred_element_type=F32)


def _dot_hi(a, b):
    return jnp.dot(a, b, preferred_element_type=F32, precision=HIGHEST)


def _params(*sem):
    return pltpu.CompilerParams(dimension_semantics=sem, vmem_limit_bytes=VMEM_LIMIT)


def _mod_kernel(c_ref, w_ref, b_ref, o_ref):
    o_ref[...] = _dot_hi(_silu(c_ref[...]), w_ref[...]) + b_ref[...]


def _modulation(c_all, w_mod, b_mod):
    r, d = c_all.shape
    n = w_mod.shape[1]
    tn = 512
    return pl.pallas_call(
        _mod_kernel,
        grid=(n // tn,),
        in_specs=[pl.BlockSpec((r, d), lambda j: (0, 0)),
                  pl.BlockSpec((d, tn), lambda j: (0, j)),
                  pl.BlockSpec((1, tn), lambda j: (0, j))],
        out_specs=pl.BlockSpec((r, tn), lambda j: (0, j)),
        out_shape=jax.ShapeDtypeStruct((r, n), F32),
        compiler_params=_params("parallel"),
        name="modulation",
    )(c_all, w_mod, b_mod.reshape(1, n))


def _inproj_kernel(x_ref, sc_ref, sh_ref, g_ref, w_ref, alog_ref, dtb_ref, tri_ref,
                   qkv_ref, z_ref, pool_ref, gate_ref, *, n_qkv, n_z, n_pool, n_heads):
    x = x_ref[0]
    xn = x * lax.rsqrt(jnp.mean(x * x, axis=-1, keepdims=True) + NORM_EPS) * g_ref[...]
    h = (xn * (1.0 + sc_ref[0]) + sh_ref[0]).astype(BF16)
    o1 = n_qkv
    o2 = o1 + n_z
    o3 = o2 + n_pool
    qkv_ref[0] = jnp.dot(h, w_ref[:, 0:o1], preferred_element_type=F32).astype(qkv_ref.dtype)
    z_ref[0] = jnp.dot(h, w_ref[:, o1:o2], preferred_element_type=F32).astype(z_ref.dtype)
    pool_ref[0] = jnp.dot(h, w_ref[:, o2:o3], preferred_element_type=F32).astype(pool_ref.dtype)
    pg = jnp.dot(h, w_ref[:, o3:], preferred_element_type=F32)
    nh2 = 2 * n_heads
    beta = 1.0 / (1.0 + jnp.exp(-pg[:, 0:nh2]))
    a = pg[:, nh2:] + dtb_ref[...]
    softplus = jnp.maximum(a, 0.0) + jnp.log(1.0 + jnp.exp(-jnp.abs(a)))
    g = -jnp.exp(alog_ref[...]) * softplus
    tm = g.shape[0]
    lane = lax.broadcasted_iota(jnp.int32, (CHUNK, nh2), 1)
    tri = tri_ref[...]
    for c in range(tm // CHUNK):
        gc = g[c * CHUNK:(c + 1) * CHUNK, :]
        pre = _dot_hi(tri, gc)
        tot = jnp.sum(gc, axis=0, keepdims=True)
        suf = tot - pre + gc
        gate_ref[0, c * CHUNK:(c + 1) * CHUNK, 0:nh2] = beta[c * CHUNK:(c + 1) * CHUNK, :]
        gate_ref[0, c * CHUNK:(c + 1) * CHUNK, nh2:] = jnp.where(lane < n_heads, pre, suf)


def _in_projection(x, sc, sh, g_pre, w_all, a_log, dt_bias, n_heads, tm):
    b, t, d = x.shape
    w = n_heads * HEAD_DIM
    n_qkv, n_z = 3 * w, w
    n_gate = 4 * n_heads
    n_pool = w_all.shape[1] - n_qkv - n_z - n_gate
    tri = jnp.asarray(np.tril(np.ones((CHUNK, CHUNK), np.float32)))
    kern = functools.partial(_inproj_kernel, n_qkv=n_qkv, n_z=n_z, n_pool=n_pool, n_heads=n_heads)
    row = lambda i, j: (i, j, 0)
    per_b = lambda i, j: (i, 0, 0)
    const = lambda i, j: (0, 0)
    return pl.pallas_call(
        kern,
        grid=(b, t // tm),
        in_specs=[pl.BlockSpec((1, tm, d), row),
                  pl.BlockSpec((1, 1, d), per_b),
                  pl.BlockSpec((1, 1, d), per_b),
                  pl.BlockSpec((1, d), const),
                  pl.BlockSpec(w_all.shape, const),
                  pl.BlockSpec((1, 2 * n_heads), const),
                  pl.BlockSpec((1, 2 * n_heads), const),
                  pl.BlockSpec((CHUNK, CHUNK), const)],
        out_specs=[pl.BlockSpec((1, tm, n_qkv), row),
                   pl.BlockSpec((1, tm, n_z), row),
                   pl.BlockSpec((1, tm, n_pool), row),
                   pl.BlockSpec((1, tm, n_gate), row)],
        out_shape=[jax.ShapeDtypeStruct((b, t, n_qkv), BF16),
                   jax.ShapeDtypeStruct((b, t, n_z), BF16),
                   jax.ShapeDtypeStruct((b, t, n_pool), BF16),
                   jax.ShapeDtypeStruct((b, t, n_gate), F32)],
        compiler_params=_params("parallel", "parallel"),
        name="in_projection",
    )(x, sc, sh, g_pre.reshape(1, d), w_all, a_log.reshape(1, -1), dt_bias.reshape(1, -1), tri)


CONV_HALO = 16


SUBLANES = 8


def _conv_silu(ref, cw_ref, win, c, n_chunks, post=None):
    hal = CONV_HALO
    r0 = pl.multiple_of(c * CHUNK, CHUNK)
    p0 = pl.multiple_of(jnp.maximum(r0 - hal, 0), hal)
    n0 = pl.multiple_of(jnp.minimum(r0 + CHUNK, n_chunks * CHUNK - hal), hal)
    win[0:hal, :] = jnp.where(c > 0, ref[0, pl.ds(p0, hal), :].astype(F32), 0.0)
    win[hal:hal + CHUNK, :] = ref[0, pl.ds(r0, CHUNK), :].astype(F32)
    win[hal + CHUNK:, :] = jnp.where(c < n_chunks - 1, ref[0, pl.ds(n0, hal), :].astype(F32), 0.0)
    width = cw_ref.shape[0]
    pad = width // 2
    n_t = CHUNK // SUBLANES
    taps = [cw_ref[j:j + 1, :] for j in range(width)]
    xs = [win[pl.ds(hal - pad + s, n_t, stride=SUBLANES), :] for s in range(SUBLANES + width - 1)]
    outs = []
    for r in range(SUBLANES):
        acc = xs[r] * taps[0]
        for j in range(1, width):
            acc = acc + xs[r + j] * taps[j]
        acc = _silu(acc)
        outs.append(acc if post is None else post(acc))
    for r in range(SUBLANES):
        win[pl.ds(hal + r, n_t, stride=SUBLANES), :] = outs[r]
    return win[hal:hal + CHUNK, :]


def _l2n(a):
    return a * lax.rsqrt(jnp.sum(a * a, axis=-1, keepdims=True) + NORM_EPS)


INV_BASE = 16
LOCAL_UNROLL = 4


def _inv_unit(mats):
    n = mats[0].shape[0]
    ii = lax.broadcasted_iota(jnp.int32, (n, n), 0)
    jj = lax.broadcasted_iota(jnp.int32, (n, n), 1)

    def same_block(bits):
        return lax.shift_right_logical(ii, bits) == lax.shift_right_logical(jj, bits)

    bits = int(np.log2(INV_BASE))
    base = same_block(bits)
    eye = jnp.where(ii == jj, 1.0, 0.0)
    ms = [jnp.where(base, -a, 0.0) for a in mats]
    ps = [eye + m for m in ms]
    for _ in range(bits - 1):
        ms = [_dot(m, m) for m in ms]
        ps = [p + _dot(p, m) for p, m in zip(ps, ms)]
    while (1 << bits) < n:
        lvl = same_block(bits + 1) & jnp.logical_not(same_block(bits))
        tmp = [_dot(jnp.where(lvl, a, 0.0), p) for a, p in zip(mats, ps)]
        ps = [p - _dot(p, t) for p, t in zip(ps, tmp)]
        bits += 1
    return ps


def _chunk_local(chunks):
    c = chunks[0][1].shape[0]
    ii = lax.broadcasted_iota(jnp.int32, (c, c), 0)
    jj = lax.broadcasted_iota(jnp.int32, (c, c), 1)
    mats, rhs, ktd, attn, qe = [], [], [], [], []
    for q, k, kt, v, kk, qk, gcol, grow in chunks:
        for d in (0, 1):
            incl = (ii >= jj) if d == 0 else (ii <= jj)
            strict = (ii > jj) if d == 0 else (ii < jj)
            beta_c = gcol[:, d:d + 1]
            gc_c = gcol[:, 2 + d:3 + d]
            gc_r = grow[2 + d:3 + d, :]
            glast = gc_r[:, c - 1:c] if d == 0 else gc_r[:, 0:1]
            decay = jnp.where(incl, jnp.exp(jnp.where(incl, gc_c - gc_r, 0.0)), 0.0)
            e_c = jnp.exp(gc_c)
            mats.append(jnp.where(strict, kk * decay * beta_c, 0.0))
            rhs.append(jnp.concatenate([k * (beta_c * e_c), v * beta_c], axis=1).astype(BF16))
            ktd.append((kt * jnp.exp(glast - gc_r)).astype(BF16))
            attn.append(None if q is None else jnp.where(incl, qk * decay, 0.0).astype(BF16))
            qe.append(None if q is None else q * e_c)
    ts = _inv_unit(mats)
    wus = [_dot(t, r) for t, r in zip(ts, rhs)]
    kns = [_dot(kd, wu) for kd, wu in zip(ktd, wus)]
    qos = [None if a is None else _dot(a, wu) for a, wu in zip(attn, wus)]
    out = []
    for kn_, qo, qe_ in zip(kns, qos, qe):
        if qo is None:
            out.append((kn_[:, :HEAD_DIM], kn_[:, HEAD_DIM:], None, None))
        else:
            out.append((kn_[:, :HEAD_DIM], kn_[:, HEAD_DIM:], qe_ - qo[:, :HEAD_DIM], qo[:, HEAD_DIM:]))
    return out


def _delta_kernel(q_ref, k_ref, v_ref, kc_ref, vc_ref, cwq_ref, cwk_ref, cwv_ref,
                  gcol_ref, grow_ref, gcolc_ref, growc_ref, z_ref, onorm_ref,
                  out_ref, kq_scr, n_scr, of, ob, s_ref, win_scr, stage_a, stage_b, *, n_x, n_c):
    n_tot = n_x + n_c
    hd = HEAD_DIM

    def prep(src_q, src_k, src_v, c0, count, n_seq, stage):
        for j in range(count):
            c = c0 + j
            k = _conv_silu(src_k, cwk_ref, win_scr.at[3 * j], c, n_seq, _l2n)
            stage[j, 0] = k
            stage[j, 1] = k.T
            stage[j, 2] = _conv_silu(src_v, cwv_ref, win_scr.at[3 * j + 1], c, n_seq)
            if src_q is not None:
                stage[j, 3] = _conv_silu(src_q, cwq_ref, win_scr.at[3 * j + 2], c, n_seq,
                                         lambda a: _l2n(a) * (hd ** -0.5))

    def local(stage, with_q, gcol_r, grow_r, c0, count, slot0):
        chunks = []
        for j in range(count):
            c = c0 + j
            r0 = pl.multiple_of(c * CHUNK, CHUNK)
            chunks.append([stage[j, 3] if with_q else None, stage[j, 0], stage[j, 1], stage[j, 2],
                           None, None, gcol_r[0, 0, pl.ds(r0, CHUNK), :], grow_r[0, 0, c]])
        for ch in chunks:
            ch[4] = _dot(ch[1], ch[2])
        for ch in chunks:
            ch[5] = None if ch[0] is None else _dot(ch[0], ch[2])
        res = _chunk_local(chunks)
        for j in range(count):
            r0 = pl.multiple_of((c0 + j) * CHUNK, CHUNK)
            for d in (0, 1):
                kmat, nmat, qmat, omat = res[2 * j + d]
                idx = d * n_tot + slot0 + j
                kq_scr[idx, 0:hd, :] = kmat.astype(BF16)
                n_scr[idx] = nmat.astype(BF16)
                if qmat is not None:
                    kq_scr[idx, hd:, :] = qmat.astype(BF16)
                    (of if d == 0 else ob)[pl.ds(r0, CHUNK), :] = omat

    u_c = min(LOCAL_UNROLL, n_c)
    u_x = min(LOCAL_UNROLL, n_x)
    g_c, g_x = n_c // u_c, n_x // u_x
    stages = (stage_a, stage_b)

    def prep_x(g, stage):
        prep(q_ref, k_ref, v_ref, g * u_x, u_x, n_x, stage)

    def local_x(g, stage):
        local(stage, True, gcol_ref, grow_ref, g * u_x, u_x, n_c + g * u_x)

    prep(None, kc_ref, vc_ref, 0, u_c, n_c, stages[0])
    for g in range(g_c):
        local(stages[g % 2], False, gcolc_ref, growc_ref, g * u_c, u_c, g * u_c)
        if g + 1 < g_c:
            prep(None, kc_ref, vc_ref, (g + 1) * u_c, u_c, n_c, stages[(g + 1) % 2])
        else:
            prep_x(0, stages[(g + 1) % 2])

    if g_x % 2 == 0:
        def local_pair(p, carry):
            for h in (0, 1):
                g = 2 * p + h
                local_x(g, stages[(g_c + h) % 2])
                prep_x(jnp.minimum(g + 1, g_x - 1), stages[(g_c + h + 1) % 2])
            return carry

        lax.fori_loop(0, g_x // 2, local_pair, 0)
    else:
        for g in range(g_x):
            local_x(g, stages[(g_c + g) % 2])
            if g + 1 < g_x:
                prep_x(g + 1, stages[(g_c + g + 1) % 2])
    s_ref[...] = jnp.zeros_like(s_ref)

    def state_decay(grow, d):
        tot = grow[2 + d:3 + d, CHUNK - 1:CHUNK] if d == 0 else grow[2 + d:3 + d, 0:1]
        return jnp.exp(tot)

    def seq_ctx(i, carry):
        for d in (0, 1):
            c = i if d == 0 else n_c - 1 - i
            idx = d * n_tot + c
            s = s_ref[d]
            ks = jnp.dot(kq_scr[idx, 0:hd, :], s.astype(BF16), preferred_element_type=F32)
            s_ref[d] = s * state_decay(growc_ref[0, 0, c], d) - ks + n_scr[idx].astype(F32)
        return carry

    def seq_x(i, carry):
        for d in (0, 1):
            c = i if d == 0 else n_x - 1 - i
            r0 = pl.multiple_of(c * CHUNK, CHUNK)
            idx = d * n_tot + n_c + c
            s = s_ref[d]
            kqs = jnp.dot(kq_scr[idx], s.astype(BF16), preferred_element_type=F32)
            o_ref = of if d == 0 else ob
            o_ref[pl.ds(r0, CHUNK), :] = o_ref[pl.ds(r0, CHUNK), :] + kqs[hd:, :]
            s_ref[d] = s * state_decay(grow_ref[0, 0, c], d) - kqs[0:hd, :] + n_scr[idx].astype(F32)
        return carry

    lax.fori_loop(0, n_c, seq_ctx, 0)
    lax.fori_loop(0, n_x, seq_x, 0)

    def fin(c, carry):
        r0 = pl.multiple_of(c * CHUNK, CHUNK)
        o = of[pl.ds(r0, CHUNK), :] + ob[pl.ds(r0, CHUNK), :]
        on = o * lax.rsqrt(jnp.mean(o * o, axis=-1, keepdims=True) + NORM_EPS) * onorm_ref[...]
        zg = _silu(z_ref[0, pl.ds(r0, CHUNK), :].astype(F32))
        out_ref[0, pl.ds(r0, CHUNK), :] = (on * zg).astype(out_ref.dtype)
        return carry

    lax.fori_loop(0, n_x, fin, 0)


def _head_gates(gates, n_heads):
    b, t, _ = gates.shape
    g = gates.reshape(b, t, 4, n_heads)
    col = jnp.transpose(g, (0, 3, 1, 2))
    row = jnp.transpose(g.reshape(b, t // CHUNK, CHUNK, 4, n_heads), (0, 4, 1, 3, 2))
    return col, row


def _delta_mixer(qkv, qkv_c, gates, gates_c, z, conv_qkv, o_norm, n_heads):
    b, t, _ = qkv.shape
    tc = qkv_c.shape[1]
    n_x, n_c = t // CHUNK, tc // CHUNK
    hd = HEAD_DIM
    assert CHUNK == hd and t % CHUNK == 0 and tc % CHUNK == 0
    width = conv_qkv.shape[0]
    gcol, grow = _head_gates(gates, n_heads)
    gcolc, growc = _head_gates(gates_c, n_heads)
    kern = functools.partial(_delta_kernel, n_x=n_x, n_c=n_c)
    nh = n_heads

    def colblk(off):
        return lambda i, h: (i, 0, off + h)

    def cwblk(off):
        return lambda i, h: (0, off + h)

    head4 = lambda i, h: (i, h, 0, 0)
    head5 = lambda i, h: (i, h, 0, 0, 0)
    return pl.pallas_call(
        kern,
        grid=(b, nh),
        in_specs=[pl.BlockSpec((1, t, hd), colblk(0)),
                  pl.BlockSpec((1, t, hd), colblk(nh)),
                  pl.BlockSpec((1, t, hd), colblk(2 * nh)),
                  pl.BlockSpec((1, tc, hd), colblk(nh)),
                  pl.BlockSpec((1, tc, hd), colblk(2 * nh)),
                  pl.BlockSpec((width, hd), cwblk(0)),
                  pl.BlockSpec((width, hd), cwblk(nh)),
                  pl.BlockSpec((width, hd), cwblk(2 * nh)),
                  pl.BlockSpec((1, 1, t, 4), head4),
                  pl.BlockSpec((1, 1, n_x, 4, CHUNK), head5),
                  pl.BlockSpec((1, 1, tc, 4), head4),
                  pl.BlockSpec((1, 1, n_c, 4, CHUNK), head5),
                  pl.BlockSpec((1, t, hd), colblk(0)),
                  pl.BlockSpec((1, hd), lambda i, h: (0, 0))],
        out_specs=pl.BlockSpec((1, t, hd), colblk(0)),
        out_shape=jax.ShapeDtypeStruct((b, t, nh * hd), BF16),
        scratch_shapes=[pltpu.VMEM((2 * (n_x + n_c), 2 * hd, hd), BF16),
                        pltpu.VMEM((2 * (n_x + n_c), hd, hd), BF16),
                        pltpu.VMEM((t, hd), F32),
                        pltpu.VMEM((t, hd), F32),
                        pltpu.VMEM((2, hd, hd), F32),
                        pltpu.VMEM((3 * LOCAL_UNROLL, CHUNK + 2 * CONV_HALO, hd), F32),
                        pltpu.VMEM((LOCAL_UNROLL, 4, CHUNK, hd), F32),
                        pltpu.VMEM((LOCAL_UNROLL, 4, CHUNK, hd), F32)],
        compiler_params=_params("parallel", "parallel"),
        name="delta_mixer",
    )(qkv, qkv, qkv, qkv_c, qkv_c, conv_qkv, conv_qkv, conv_qkv,
      gcol, grow, gcolc, growc, z, o_norm.reshape(1, hd))


POOL_UNROLL = 2


def _pool_kernel(u_ref, band_ref, icnt_ref, w_ref, sc_ref, out_ref, cs, *, rows, blk):
    g = pl.program_id(1)
    left = lax.shift_left(jnp.int32(1), g)
    right = left - 1
    gw = GRID_W
    gd = cs.shape[1]
    t = rows * gw

    cs[0:gw, :] = jnp.zeros((gw, gd), F32)

    def prefix(r, acc):
        acc = acc + u_ref[0, pl.ds(pl.multiple_of(r * gw, gw), gw), :].astype(F32)
        cs[pl.ds(pl.multiple_of((r + 1) * gw, gw), gw), :] = acc
        return acc

    lax.fori_loop(0, rows, prefix, jnp.zeros((gw, gd), F32))

    def row_mean(r):
        lo = jnp.maximum(r - left, 0)
        hi = jnp.minimum(r + right + 1, rows)
        tot = cs[pl.ds(pl.multiple_of(hi * gw, gw), gw), :] - cs[pl.ds(pl.multiple_of(lo * gw, gw), gw), :]
        return tot / jnp.full((gw, gd), hi - lo, jnp.int32).astype(F32)

    def col_body(i, carry):
        blocks = [i * POOL_UNROLL + j for j in range(POOL_UNROLL)]
        starts = [pl.multiple_of(bi * blk, blk) for bi in blocks]
        rs = [jnp.concatenate([row_mean(bi * (blk // gw) + rr) for rr in range(blk // gw)], axis=0)
              for bi in blocks]
        ms = [jnp.dot(band_ref[0], r.astype(BF16), preferred_element_type=F32) for r in rs]
        ds = [(m * icnt_ref[0] - u_ref[0, pl.ds(r0, blk), :].astype(F32)).astype(BF16)
              for m, r0 in zip(ms, starts)]
        ys = [jnp.dot(dlt, w_ref[0], preferred_element_type=F32) for dlt in ds]
        for y, r0 in zip(ys, starts):
            out_ref[0, pl.ds(r0, blk), :] = (y * sc_ref[0]).astype(out_ref.dtype)
        return carry

    lax.fori_loop(0, t // (blk * POOL_UNROLL), col_body, 0)


def _pool_consts(blk):
    band = np.zeros((len(POOL_WINDOWS), blk, blk), np.float32)
    icnt = np.zeros((len(POOL_WINDOWS), blk, HEAD_DIM), np.float32)
    for gi, w in enumerate(POOL_WINDOWS):
        left = w // 2
        right = w - 1 - left
        for i in range(blk):
            base, col = (i // GRID_W) * GRID_W, i % GRID_W
            lo, hi = max(col - left, 0), min(col + right, GRID_W - 1)
            band[gi, i, base + lo:base + hi + 1] = 1.0
            icnt[gi, i, :] = 1.0 / (hi - lo + 1)
    return jnp.asarray(band, BF16), jnp.asarray(icnt)


def _pool_mixer(u, pool_w, pool_scale):
    b, t, p = u.shape
    n_g, gd = pool_w.shape[0], pool_w.shape[1]
    rows = t // GRID_W
    blk = 4 * GRID_W
    band, icnt = _pool_consts(blk)
    kern = functools.partial(_pool_kernel, rows=rows, blk=blk)
    grp = lambda i, g: (g, 0, 0)
    return pl.pallas_call(
        kern,
        grid=(b, n_g),
        in_specs=[pl.BlockSpec((1, t, gd), lambda i, g: (i, 0, g)),
                  pl.BlockSpec((1, blk, blk), grp),
                  pl.BlockSpec((1, blk, gd), grp),
                  pl.BlockSpec((1, gd, gd), grp),
                  pl.BlockSpec((1, 1, gd), grp)],
        out_specs=pl.BlockSpec((1, t, gd), lambda i, g: (i, 0, g)),
        out_shape=jax.ShapeDtypeStruct((b, t, p), BF16),
        scratch_shapes=[pltpu.VMEM((t + GRID_W, gd), F32)],
        compiler_params=_params("parallel", "parallel"),
        name="pool_mixer",
    )(u, band, icnt, pool_w.astype(BF16), pool_scale.reshape(n_g, 1, gd))


def _rms(a, g):
    return a * lax.rsqrt(jnp.mean(a * a, axis=-1, keepdims=True) + NORM_EPS) * g


def _outproj_kernel(dn_ref, pool_ref, x_ref, w1_ref, w2_ref, gt_ref, sc_ref, sh_ref,
                    gpost_ref, gpre_ref, x1_ref, h_ref):
    y = (jnp.dot(dn_ref[0], w1_ref[...], preferred_element_type=F32)
         + jnp.dot(pool_ref[0], w2_ref[...], preferred_element_type=F32))
    x1 = x_ref[0] + gt_ref[0] * _rms(y, gpost_ref[...])
    x1_ref[0] = x1
    h_ref[0] = (_rms(x1, gpre_ref[...]) * (1.0 + sc_ref[0]) + sh_ref[0]).astype(h_ref.dtype)


def _out_projection(dn, pool, x, w_out, gt1, sc2, sh2, g_post, g_pre, tm):
    b, t, d = x.shape
    wd = dn.shape[-1]
    w1 = w_out[:wd].astype(BF16)
    w2 = w_out[wd:].astype(BF16)
    row = lambda i, j: (i, j, 0)
    per_b = lambda i, j: (i, 0, 0)
    const = lambda i, j: (0, 0)
    return pl.pallas_call(
        _outproj_kernel,
        grid=(b, t // tm),
        in_specs=[pl.BlockSpec((1, tm, wd), row),
                  pl.BlockSpec((1, tm, pool.shape[-1]), row),
                  pl.BlockSpec((1, tm, d), row),
                  pl.BlockSpec(w1.shape, const),
                  pl.BlockSpec(w2.shape, const),
                  pl.BlockSpec((1, 1, d), per_b),
                  pl.BlockSpec((1, 1, d), per_b),
                  pl.BlockSpec((1, 1, d), per_b),
                  pl.BlockSpec((1, d), const),
                  pl.BlockSpec((1, d), const)],
        out_specs=[pl.BlockSpec((1, tm, d), row), pl.BlockSpec((1, tm, d), row)],
        out_shape=[jax.ShapeDtypeStruct((b, t, d), F32), jax.ShapeDtypeStruct((b, t, d), BF16)],
        compiler_params=_params("parallel", "parallel"),
        name="out_projection",
    )(dn, pool, x, w1, w2, gt1, sc2, sh2, g_post.reshape(1, d), g_pre.reshape(1, d))


FFN_HALO = 16


def _ffn_kernel(h_ref, hp_ref, hn_ref, x1_ref, wgu_ref, cw_ref, wd_ref, gt_ref, gpost_ref,
                out_ref, hext, act, *, fc, n_fc):
    j = pl.program_id(1)
    nj = pl.num_programs(1)
    tm = h_ref.shape[1]
    hal = FFN_HALO
    dff = fc * n_fc
    hext[0:hal, :] = jnp.where(j > 0, hp_ref[0], jnp.zeros_like(hp_ref[0]))
    hext[hal:hal + tm, :] = h_ref[0]
    hext[hal + tm:, :] = jnp.where(j < nj - 1, hn_ref[0], jnp.zeros_like(hn_ref[0]))
    for c in range(n_fc):
        gate = jnp.dot(hext[...], wgu_ref[:, c * fc:(c + 1) * fc], preferred_element_type=F32)
        up = jnp.dot(h_ref[0], wgu_ref[:, dff + c * fc:dff + (c + 1) * fc], preferred_element_type=F32)
        cw = cw_ref[:, c * fc:(c + 1) * fc]
        conv = (gate[hal - 1:hal - 1 + tm] * cw[0:1] + gate[hal:hal + tm] * cw[1:2]
                + gate[hal + 1:hal + 1 + tm] * cw[2:3])
        act[:, c * fc:(c + 1) * fc] = (_silu(conv) * up).astype(BF16)
    y = jnp.dot(act[...], wd_ref[...], preferred_element_type=F32)
    out_ref[0] = x1_ref[0] + gt_ref[0] * _rms(y, gpost_ref[...])


def _conv_ffn(h, x1, w_up, conv_ffn, w_down, gt2, g_post, tm, fc):
    b, t, d = x1.shape
    dff = w_down.shape[0]
    n_fc = dff // fc
    wgu = w_up.astype(BF16)
    hal = FFN_HALO
    nb = tm // hal
    last = t // hal - 1
    kern = functools.partial(_ffn_kernel, fc=fc, n_fc=n_fc)
    row = lambda i, j: (i, j, 0)
    per_b = lambda i, j: (i, 0, 0)
    const = lambda i, j: (0, 0)
    return pl.pallas_call(
        kern,
        grid=(b, t // tm),
        in_specs=[pl.BlockSpec((1, tm, d), row),
                  pl.BlockSpec((1, hal, d), lambda i, j: (i, jnp.maximum(j * nb - 1, 0), 0)),
                  pl.BlockSpec((1, hal, d), lambda i, j: (i, jnp.minimum((j + 1) * nb, last), 0)),
                  pl.BlockSpec((1, tm, d), row),
                  pl.BlockSpec(wgu.shape, const),
                  pl.BlockSpec(conv_ffn.shape, const),
                  pl.BlockSpec(w_down.shape, const),
                  pl.BlockSpec((1, 1, d), per_b),
                  pl.BlockSpec((1, d), const)],
        out_specs=pl.BlockSpec((1, tm, d), row),
        out_shape=jax.ShapeDtypeStruct((b, t, d), F32),
        scratch_shapes=[pltpu.VMEM((tm + 2 * hal, d), BF16), pltpu.VMEM((tm, dff), BF16)],
        compiler_params=_params("parallel", "parallel"),
        name="conv_ffn",
    )(h, h, h, x1, wgu, conv_ffn, w_down.astype(BF16), gt2, g_post.reshape(1, d))


def _layer(x, c, ctx, c_ctx, w_mod, b_mod, g_pre_mix, g_post_mix, g_pre_ffn, g_post_ffn,
           w_in, conv_qkv, a_log, dt_bias, o_norm, pool_w, pool_scale, w_out, w_up,
           conv_ffn, w_down):
    b, t, d = x.shape
    n_heads = a_log.shape[-1]
    w = n_heads * HEAD_DIM
    n_gate = 4 * n_heads

    rows = 16
    c_all = jnp.zeros((rows, d), F32).at[:b].set(c).at[b].set(c_ctx)
    m = _modulation(c_all, w_mod, b_mod)
    sh1, sc1, gt1, sh2, sc2, gt2 = [m[:b, None, i * d:(i + 1) * d] for i in range(N_MOD)]
    csh1 = jnp.broadcast_to(m[b, 0 * d:1 * d], (b, 1, d))
    csc1 = jnp.broadcast_to(m[b, 1 * d:2 * d], (b, 1, d))

    off_g = 4 * w
    w_all = jnp.concatenate([w_in[:, :off_g], w_in[:, off_g + n_gate:],
                             w_in[:, off_g:off_g + n_gate]], axis=1).astype(BF16)
    tm = min(512, t)
    qkv, z, pool_u, gates = _in_projection(x, sc1, sh1, g_pre_mix, w_all, a_log, dt_bias,
                                           n_heads, tm)
    qkv_c, _, _, gates_c = _in_projection(ctx, csc1, csh1, g_pre_mix, w_all, a_log, dt_bias,
                                          n_heads, ctx.shape[1])
    dn = _delta_mixer(qkv, qkv_c, gates, gates_c, z, conv_qkv, o_norm, n_heads)
    pool = _pool_mixer(pool_u, pool_w, pool_scale)
    x1, h2 = _out_projection(dn, pool, x, w_out, gt1, sc2, sh2, g_post_mix, g_pre_ffn, tm)
    return _conv_ffn(h2, x1, w_up, conv_ffn, w_down, gt2, g_post_ffn, tm, 256)


def kernel(x, c, ctx, c_ctx, w_mod, b_mod, g_pre_mix, g_post_mix, g_pre_ffn, g_post_ffn, w_in,
           conv_qkv, a_log, dt_bias, o_norm, pool_w, pool_scale, w_out, w_up, conv_ffn, w_down):
    depth = w_mod.shape[0]
    assert depth == 1, "context-stream update between layers is not implemented"
    return _layer(x, c, ctx, c_ctx, w_mod[0], b_mod[0], g_pre_mix[0], g_post_mix[0],
                  g_pre_ffn[0], g_post_ffn[0], w_in[0], conv_qkv[0], a_log[0], dt_bias[0],
                  o_norm[0], pool_w[0], pool_scale[0], w_out[0], w_up[0], conv_ffn[0], w_down[0])
```

```python
import functools

import numpy as np
import jax
import jax.numpy as jnp
from jax import lax
from jax.experimental import pallas as pl
from jax.experimental.pallas import tpu as pltpu

NORM_EPS = 1e-6
GRID_W = 64
POOL_WINDOWS = (2, 4, 8, 16)
N_MOD = 6
CHUNK = 128
HEAD_DIM = 128
VMEM_LIMIT = 56 * 1024 * 1024

F32 = jnp.float32
BF16 = jnp.bfloat16
HIGHEST = lax.Precision.HIGHEST


def _silu(a):
    return a * (1.0 / (1.0 + jnp.exp(-a)))


def _dot(a, b):
    return jnp.dot(a.astype(BF16), b.astype(BF16), preferred_element_type=F32)


def _dot_hi(a, b):
    return jnp.dot(a, b, preferred_element_type=F32, precision=HIGHEST)


def _params(*sem):
    return pltpu.CompilerParams(dimension_semantics=sem, vmem_limit_bytes=VMEM_LIMIT)


def _mod_kernel(c_ref, w_ref, b_ref, o_ref):
    o_ref[...] = _dot_hi(_silu(c_ref[...]), w_ref[...]) + b_ref[...]


def _modulation(c_all, w_mod, b_mod):
    r, d = c_all.shape
    n = w_mod.shape[1]
    tn = 512
    return pl.pallas_call(
        _mod_kernel,
        grid=(n // tn,),
        in_specs=[pl.BlockSpec((r, d), lambda j: (0, 0)),
                  pl.BlockSpec((d, tn), lambda j: (0, j)),
                  pl.BlockSpec((1, tn), lambda j: (0, j))],
        out_specs=pl.BlockSpec((r, tn), lambda j: (0, j)),
        out_shape=jax.ShapeDtypeStruct((r, n), F32),
        compiler_params=_params("parallel"),
        name="modulation",
    )(c_all, w_mod, b_mod.reshape(1, n))


def _inproj_kernel(x_ref, sc_ref, sh_ref, g_ref, w_ref, alog_ref, dtb_ref, tri_ref,
                   qkv_ref, z_ref, pool_ref, gate_ref, *, n_qkv, n_z, n_pool, n_heads):
    x = x_ref[0]
    xn = x * lax.rsqrt(jnp.mean(x * x, axis=-1, keepdims=True) + NORM_EPS) * g_ref[...]
    h = (xn * (1.0 + sc_ref[0]) + sh_ref[0]).astype(BF16)
    o1 = n_qkv
    o2 = o1 + n_z
    o3 = o2 + n_pool
    qkv_ref[0] = jnp.dot(h, w_ref[:, 0:o1], preferred_element_type=F32).astype(qkv_ref.dtype)
    z_ref[0] = jnp.dot(h, w_ref[:, o1:o2], preferred_element_type=F32).astype(z_ref.dtype)
    pool_ref[0] = jnp.dot(h, w_ref[:, o2:o3], preferred_element_type=F32).astype(pool_ref.dtype)
    pg = jnp.dot(h, w_ref[:, o3:], preferred_element_type=F32)
    nh2 = 2 * n_heads
    beta = 1.0 / (1.0 + jnp.exp(-pg[:, 0:nh2]))
    a = pg[:, nh2:] + dtb_ref[...]
    softplus = jnp.maximum(a, 0.0) + jnp.log(1.0 + jnp.exp(-jnp.abs(a)))
    g = -jnp.exp(alog_ref[...]) * softplus
    tm = g.shape[0]
    lane = lax.broadcasted_iota(jnp.int32, (CHUNK, nh2), 1)
    tri = tri_ref[...]
    for c in range(tm // CHUNK):
        gc = g[c * CHUNK:(c + 1) * CHUNK, :]
        pre = _dot_hi(tri, gc)
        tot = jnp.sum(gc, axis=0, keepdims=True)
        suf = tot - pre + gc
        gate_ref[0, c * CHUNK:(c + 1) * CHUNK, 0:nh2] = beta[c * CHUNK:(c + 1) * CHUNK, :]
        gate_ref[0, c * CHUNK:(c + 1) * CHUNK, nh2:] = jnp.where(lane < n_heads, pre, suf)


def _in_projection(x, sc, sh, g_pre, w_all, a_log, dt_bias, n_heads, tm):
    b, t, d = x.shape
    w = n_heads * HEAD_DIM
    n_qkv, n_z = 3 * w, w
    n_gate = 4 * n_heads
    n_pool = w_all.shape[1] - n_qkv - n_z - n_gate
    tri = jnp.asarray(np.tril(np.ones((CHUNK, CHUNK), np.float32)))
    kern = functools.partial(_inproj_kernel, n_qkv=n_qkv, n_z=n_z, n_pool=n_pool, n_heads=n_heads)
    row = lambda i, j: (i, j, 0)
    per_b = lambda i, j: (i, 0, 0)
    const = lambda i, j: (0, 0)
    return pl.pallas_call(
        kern,
        grid=(b, t // tm),
        in_specs=[pl.BlockSpec((1, tm, d), row),
                  pl.BlockSpec((1, 1, d), per_b),
                  pl.BlockSpec((1, 1, d), per_b),
                  pl.BlockSpec((1, d), const),
                  pl.BlockSpec(w_all.shape, const),
                  pl.BlockSpec((1, 2 * n_heads), const),
                  pl.BlockSpec((1, 2 * n_heads), const),
                  pl.BlockSpec((CHUNK, CHUNK), const)],
        out_specs=[pl.BlockSpec((1, tm, n_qkv), row),
                   pl.BlockSpec((1, tm, n_z), row),
                   pl.BlockSpec((1, tm, n_pool), row),
                   pl.BlockSpec((1, tm, n_gate), row)],
        out_shape=[jax.ShapeDtypeStruct((b, t, n_qkv), BF16),
                   jax.ShapeDtypeStruct((b, t, n_z), BF16),
                   jax.ShapeDtypeStruct((b, t, n_pool), BF16),
                   jax.ShapeDtypeStruct((b, t, n_gate), F32)],
        compiler_params=_params("parallel", "parallel"),
        name="in_projection",
    )(x, sc, sh, g_pre.reshape(1, d), w_all, a_log.reshape(1, -1), dt_bias.reshape(1, -1), tri)


CONV_HALO = 16


SUBLANES = 8


def _conv_silu(ref, cw_ref, win, c, n_chunks, post=None):
    hal = CONV_HALO
    r0 = pl.multiple_of(c * CHUNK, CHUNK)
    p0 = pl.multiple_of(jnp.maximum(r0 - hal, 0), hal)
    n0 = pl.multiple_of(jnp.minimum(r0 + CHUNK, n_chunks * CHUNK - hal), hal)
    win[0:hal, :] = jnp.where(c > 0, ref[0, pl.ds(p0, hal), :].astype(F32), 0.0)
    win[hal:hal + CHUNK, :] = ref[0, pl.ds(r0, CHUNK), :].astype(F32)
    win[hal + CHUNK:, :] = jnp.where(c < n_chunks - 1, ref[0, pl.ds(n0, hal), :].astype(F32), 0.0)
    width = cw_ref.shape[0]
    pad = width // 2
    n_t = CHUNK // SUBLANES
    taps = [cw_ref[j:j + 1, :] for j in range(width)]
    xs = [win[pl.ds(hal - pad + s, n_t, stride=SUBLANES), :] for s in range(SUBLANES + width - 1)]
    outs = []
    for r in range(SUBLANES):
        acc = xs[r] * taps[0]
        for j in range(1, width):
            acc = acc + xs[r + j] * taps[j]
        acc = _silu(acc)
        outs.append(acc if post is None else post(acc))
    for r in range(SUBLANES):
        win[pl.ds(hal + r, n_t, stride=SUBLANES), :] = outs[r]
    return win[hal:hal + CHUNK, :]


def _l2n(a):
    return a * lax.rsqrt(jnp.sum(a * a, axis=-1, keepdims=True) + NORM_EPS)


INV_BASE = 16
LOCAL_UNROLL = 4


def _inv_unit(mats):
    n = mats[0].shape[0]
    ii = lax.broadcasted_iota(jnp.int32, (n, n), 0)
    jj = lax.broadcasted_iota(jnp.int32, (n, n), 1)

    def same_block(bits):
        return lax.shift_right_logical(ii, bits) == lax.shift_right_logical(jj, bits)

    bits = int(np.log2(INV_BASE))
    base = same_block(bits)
    eye = jnp.where(ii == jj, 1.0, 0.0)
    ms = [jnp.where(base, -a, 0.0) for a in mats]
    ps = [eye + m for m in ms]
    ms = [_dot(m, m) for m in ms]
    for _ in range(bits - 2):
        pm = [_dot(jnp.concatenate([p, m], axis=0), m) for p, m in zip(ps, ms)]
        ps = [p + x[:n] for p, x in zip(ps, pm)]
        ms = [x[n:] for x in pm]
    ps = [p + _dot(p, m) for p, m in zip(ps, ms)]
    while (1 << bits) < n:
        lvl = same_block(bits + 1) & jnp.logical_not(same_block(bits))
        tmp = [_dot(jnp.where(lvl, a, 0.0), p) for a, p in zip(mats, ps)]
        ps = [p - _dot(p, t) for p, t in zip(ps, tmp)]
        bits += 1
    return ps


def _chunk_local(chunks):
    c = chunks[0][1].shape[0]
    ii = lax.broadcasted_iota(jnp.int32, (c, c), 0)
    jj = lax.broadcasted_iota(jnp.int32, (c, c), 1)
    mats, rhs, ktd, attn, qe = [], [], [], [], []
    for q, k, kt, v, kk, qk, gcol, grow in chunks:
        for d in (0, 1):
            incl = (ii >= jj) if d == 0 else (ii <= jj)
            strict = (ii > jj) if d == 0 else (ii < jj)
            beta_c = gcol[:, d:d + 1]
            gc_c = gcol[:, 2 + d:3 + d]
            gc_r = grow[2 + d:3 + d, :]
            glast = gc_r[:, c - 1:c] if d == 0 else gc_r[:, 0:1]
            decay = jnp.where(incl, jnp.exp(jnp.where(incl, gc_c - gc_r, 0.0)), 0.0)
            e_c = jnp.exp(gc_c)
            mats.append(jnp.where(strict, kk * decay * beta_c, 0.0))
            rhs.append(jnp.concatenate([k * (beta_c * e_c), v * beta_c], axis=1).astype(BF16))
            ktd.append((kt * jnp.exp(glast - gc_r)).astype(BF16))
            attn.append(None if q is None else jnp.where(incl, qk * decay, 0.0).astype(BF16))
            qe.append(None if q is None else q * e_c)
    ts = _inv_unit(mats)
    wus = [_dot(t, r) for t, r in zip(ts, rhs)]
    hd = HEAD_DIM
    out = []
    for kd, a, wu, qe_ in zip(ktd, attn, wus, qe):
        if a is None:
            kn_ = _dot(kd, wu)
            out.append((kn_[:, :hd], kn_[:, hd:], None, None))
        else:
            x = _dot(jnp.concatenate([kd, a], axis=0), wu)
            out.append((x[:hd, :hd], x[:hd, hd:], qe_ - x[hd:, :hd], x[hd:, hd:]))
    return out


def _delta_kernel(q_ref, k_ref, v_ref, kc_ref, vc_ref, cwq_ref, cwk_ref, cwv_ref,
                  gcol_ref, grow_ref, gcolc_ref, growc_ref, z_ref, onorm_ref,
                  out_ref, kq_scr, n_scr, of, ob, s_ref, win_scr, stage_a, stage_b, *, n_x, n_c):
    n_tot = n_x + n_c
    hd = HEAD_DIM

    def prep(src_q, src_k, src_v, c0, count, n_seq, stage):
        for j in range(count):
            c = c0 + j
            k = _conv_silu(src_k, cwk_ref, win_scr.at[3 * j], c, n_seq, _l2n)
            stage[j, 0] = k
            stage[j, 1] = k.T
            stage[j, 2] = _conv_silu(src_v, cwv_ref, win_scr.at[3 * j + 1], c, n_seq)
            if src_q is not None:
                stage[j, 3] = _conv_silu(src_q, cwq_ref, win_scr.at[3 * j + 2], c, n_seq,
                                         lambda a: _l2n(a) * (hd ** -0.5))

    def local(stage, with_q, gcol_r, grow_r, c0, count, slot0):
        chunks = []
        for j in range(count):
            c = c0 + j
            r0 = pl.multiple_of(c * CHUNK, CHUNK)
            chunks.append([stage[j, 3] if with_q else None, stage[j, 0], stage[j, 1], stage[j, 2],
                           None, None, gcol_r[0, 0, pl.ds(r0, CHUNK), :], grow_r[0, 0, c]])
        for ch in chunks:
            ch[4] = _dot(ch[1], ch[2])
        for ch in chunks:
            ch[5] = None if ch[0] is None else _dot(ch[0], ch[2])
        res = _chunk_local(chunks)
        for j in range(count):
            r0 = pl.multiple_of((c0 + j) * CHUNK, CHUNK)
            for d in (0, 1):
                kmat, nmat, qmat, omat = res[2 * j + d]
                idx = d * n_tot + slot0 + j
                kq_scr[idx, 0:hd, :] = kmat.astype(BF16)
                n_scr[idx] = nmat.astype(BF16)
                if qmat is not None:
                    kq_scr[idx, hd:, :] = qmat.astype(BF16)
                    (of if d == 0 else ob)[pl.ds(r0, CHUNK), :] = omat

    u_c = min(LOCAL_UNROLL, n_c)
    u_x = min(LOCAL_UNROLL, n_x)
    g_c, g_x = n_c // u_c, n_x // u_x
    stages = (stage_a, stage_b)

    def prep_x(g, stage):
        prep(q_ref, k_ref, v_ref, g * u_x, u_x, n_x, stage)

    def local_x(g, stage):
        local(stage, True, gcol_ref, grow_ref, g * u_x, u_x, n_c + g * u_x)

    prep(None, kc_ref, vc_ref, 0, u_c, n_c, stages[0])
    for g in range(g_c):
        local(stages[g % 2], False, gcolc_ref, growc_ref, g * u_c, u_c, g * u_c)
        if g + 1 < g_c:
            prep(None, kc_ref, vc_ref, (g + 1) * u_c, u_c, n_c, stages[(g + 1) % 2])
        else:
            prep_x(0, stages[(g + 1) % 2])

    if g_x % 2 == 0:
        def local_pair(p, carry):
            for h in (0, 1):
                g = 2 * p + h
                local_x(g, stages[(g_c + h) % 2])
                prep_x(jnp.minimum(g + 1, g_x - 1), stages[(g_c + h + 1) % 2])
            return carry

        lax.fori_loop(0, g_x // 2, local_pair, 0)
    else:
        for g in range(g_x):
            local_x(g, stages[(g_c + g) % 2])
            if g + 1 < g_x:
                prep_x(g + 1, stages[(g_c + g + 1) % 2])
    s_ref[...] = jnp.zeros_like(s_ref)

    def state_decay(grow, d):
        tot = grow[2 + d:3 + d, CHUNK - 1:CHUNK] if d == 0 else grow[2 + d:3 + d, 0:1]
        return jnp.exp(tot)

    def seq_ctx(i, carry):
        for d in (0, 1):
            c = i if d == 0 else n_c - 1 - i
            idx = d * n_tot + c
            s = s_ref[d]
            ks = jnp.dot(kq_scr[idx, 0:hd, :], s.astype(BF16), preferred_element_type=F32)
            s_ref[d] = s * state_decay(growc_ref[0, 0, c], d) - ks + n_scr[idx].astype(F32)
        return carry

    def finish(r0, o):
        on = o * lax.rsqrt(jnp.mean(o * o, axis=-1, keepdims=True) + NORM_EPS) * onorm_ref[...]
        zg = _silu(z_ref[0, pl.ds(r0, CHUNK), :].astype(F32))
        out_ref[0, pl.ds(r0, CHUNK), :] = (on * zg).astype(out_ref.dtype)

    def seq_x(i, carry, second_visit):
        for d in (0, 1):
            c = i if d == 0 else n_x - 1 - i
            r0 = pl.multiple_of(c * CHUNK, CHUNK)
            idx = d * n_tot + n_c + c
            s = s_ref[d]
            kqs = jnp.dot(kq_scr[idx], s.astype(BF16), preferred_element_type=F32)
            o_ref, other = (of, ob) if d == 0 else (ob, of)
            o = o_ref[pl.ds(r0, CHUNK), :] + kqs[hd:, :]
            if second_visit:
                finish(r0, o + other[pl.ds(r0, CHUNK), :])
            else:
                o_ref[pl.ds(r0, CHUNK), :] = o
            s_ref[d] = s * state_decay(grow_ref[0, 0, c], d) - kqs[0:hd, :] + n_scr[idx].astype(F32)
        return carry

    lax.fori_loop(0, n_c, seq_ctx, 0)
    half = (n_x + 1) // 2
    lax.fori_loop(0, half, functools.partial(seq_x, second_visit=False), 0)
    if n_x % 2:
        r_mid = (half - 1) * CHUNK
        finish(r_mid, of[r_mid:r_mid + CHUNK, :] + ob[r_mid:r_mid + CHUNK, :])
    lax.fori_loop(half, n_x, functools.partial(seq_x, second_visit=True), 0)


def _head_gates(gates, n_heads):
    b, t, _ = gates.shape
    g = gates.reshape(b, t, 4, n_heads)
    col = jnp.transpose(g, (0, 3, 1, 2))
    row = jnp.transpose(g.reshape(b, t // CHUNK, CHUNK, 4, n_heads), (0, 4, 1, 3, 2))
    return col, row


def _delta_mixer(qkv, qkv_c, gates, gates_c, z, conv_qkv, o_norm, n_heads):
    b, t, _ = qkv.shape
    tc = qkv_c.shape[1]
    n_x, n_c = t // CHUNK, tc // CHUNK
    hd = HEAD_DIM
    assert CHUNK == hd and t % CHUNK == 0 and tc % CHUNK == 0
    width = conv_qkv.shape[0]
    gcol, grow = _head_gates(gates, n_heads)
    gcolc, growc = _head_gates(gates_c, n_heads)
    kern = functools.partial(_delta_kernel, n_x=n_x, n_c=n_c)
    nh = n_heads

    def colblk(off):
        return lambda i, h: (i, 0, off + h)

    def cwblk(off):
        return lambda i, h: (0, off + h)

    head4 = lambda i, h: (i, h, 0, 0)
    head5 = lambda i, h: (i, h, 0, 0, 0)
    return pl.pallas_call(
        kern,
        grid=(b, nh),
        in_specs=[pl.BlockSpec((1, t, hd), colblk(0)),
                  pl.BlockSpec((1, t, hd), colblk(nh)),
                  pl.BlockSpec((1, t, hd), colblk(2 * nh)),
                  pl.BlockSpec((1, tc, hd), colblk(nh)),
                  pl.BlockSpec((1, tc, hd), colblk(2 * nh)),
                  pl.BlockSpec((width, hd), cwblk(0)),
                  pl.BlockSpec((width, hd), cwblk(nh)),
                  pl.BlockSpec((width, hd), cwblk(2 * nh)),
                  pl.BlockSpec((1, 1, t, 4), head4),
                  pl.BlockSpec((1, 1, n_x, 4, CHUNK), head5),
                  pl.BlockSpec((1, 1, tc, 4), head4),
                  pl.BlockSpec((1, 1, n_c, 4, CHUNK), head5),
                  pl.BlockSpec((1, t, hd), colblk(0)),
                  pl.BlockSpec((1, hd), lambda i, h: (0, 0))],
        out_specs=pl.BlockSpec((1, t, hd), colblk(0)),
        out_shape=jax.ShapeDtypeStruct((b, t, nh * hd), BF16),
        scratch_shapes=[pltpu.VMEM((2 * (n_x + n_c), 2 * hd, hd), BF16),
                        pltpu.VMEM((2 * (n_x + n_c), hd, hd), BF16),
                        pltpu.VMEM((t, hd), F32),
                        pltpu.VMEM((t, hd), F32),
                        pltpu.VMEM((2, hd, hd), F32),
                        pltpu.VMEM((3 * LOCAL_UNROLL, CHUNK + 2 * CONV_HALO, hd), F32),
                        pltpu.VMEM((LOCAL_UNROLL, 4, CHUNK, hd), F32),
                        pltpu.VMEM((LOCAL_UNROLL, 4, CHUNK, hd), F32)],
        compiler_params=_params("parallel", "parallel"),
        name="delta_mixer",
    )(qkv, qkv, qkv, qkv_c, qkv_c, conv_qkv, conv_qkv, conv_qkv,
      gcol, grow, gcolc, growc, z, o_norm.reshape(1, hd))


POOL_UNROLL = 2


def _pool_kernel(u_ref, band_ref, icnt_ref, w_ref, sc_ref, out_ref, cs, *, rows, blk):
    g = pl.program_id(1)
    left = lax.shift_left(jnp.int32(1), g)
    right = left - 1
    gw = GRID_W
    gd = cs.shape[1]
    t = rows * gw

    cs[0:gw, :] = jnp.zeros((gw, gd), F32)

    def prefix(r, acc):
        acc = acc + u_ref[0, pl.ds(pl.multiple_of(r * gw, gw), gw), :].astype(F32)
        cs[pl.ds(pl.multiple_of((r + 1) * gw, gw), gw), :] = acc
        return acc

    lax.fori_loop(0, rows, prefix, jnp.zeros((gw, gd), F32))

    def row_mean(r):
        lo = jnp.maximum(r - left, 0)
        hi = jnp.minimum(r + right + 1, rows)
        tot = cs[pl.ds(pl.multiple_of(hi * gw, gw), gw), :] - cs[pl.ds(pl.multiple_of(lo * gw, gw), gw), :]
        return tot / jnp.full((gw, gd), hi - lo, jnp.int32).astype(F32)

    def col_body(i, carry):
        blocks = [i * POOL_UNROLL + j for j in range(POOL_UNROLL)]
        starts = [pl.multiple_of(bi * blk, blk) for bi in blocks]
        rs = [jnp.concatenate([row_mean(bi * (blk // gw) + rr) for rr in range(blk // gw)], axis=0)
              for bi in blocks]
        ms = [jnp.dot(band_ref[0], r.astype(BF16), preferred_element_type=F32) for r in rs]
        ds = [(m * icnt_ref[0] - u_ref[0, pl.ds(r0, blk), :].astype(F32)).astype(BF16)
              for m, r0 in zip(ms, starts)]
        ys = [jnp.dot(dlt, w_ref[0], preferred_element_type=F32) for dlt in ds]
        for y, r0 in zip(ys, starts):
            out_ref[0, pl.ds(r0, blk), :] = (y * sc_ref[0]).astype(out_ref.dtype)
        return carry

    lax.fori_loop(0, t // (blk * POOL_UNROLL), col_body, 0)


def _pool_consts(blk):
    band = np.zeros((len(POOL_WINDOWS), blk, blk), np.float32)
    icnt = np.zeros((len(POOL_WINDOWS), blk, HEAD_DIM), np.float32)
    for gi, w in enumerate(POOL_WINDOWS):
        left = w // 2
        right = w - 1 - left
        for i in range(blk):
            base, col = (i // GRID_W) * GRID_W, i % GRID_W
            lo, hi = max(col - left, 0), min(col + right, GRID_W - 1)
            band[gi, i, base + lo:base + hi + 1] = 1.0
            icnt[gi, i, :] = 1.0 / (hi - lo + 1)
    return jnp.asarray(band, BF16), jnp.asarray(icnt)


def _pool_mixer(u, pool_w, pool_scale):
    b, t, p = u.shape
    n_g, gd = pool_w.shape[0], pool_w.shape[1]
    rows = t // GRID_W
    blk = 4 * GRID_W
    band, icnt = _pool_consts(blk)
    kern = functools.partial(_pool_kernel, rows=rows, blk=blk)
    grp = lambda i, g: (g, 0, 0)
    return pl.pallas_call(
        kern,
        grid=(b, n_g),
        in_specs=[pl.BlockSpec((1, t, gd), lambda i, g: (i, 0, g)),
                  pl.BlockSpec((1, blk, blk), grp),
                  pl.BlockSpec((1, blk, gd), grp),
                  pl.BlockSpec((1, gd, gd), grp),
                  pl.BlockSpec((1, 1, gd), grp)],
        out_specs=pl.BlockSpec((1, t, gd), lambda i, g: (i, 0, g)),
        out_shape=jax.ShapeDtypeStruct((b, t, p), BF16),
        scratch_shapes=[pltpu.VMEM((t + GRID_W, gd), F32)],
        compiler_params=_params("parallel", "parallel"),
        name="pool_mixer",
    )(u, band, icnt, pool_w.astype(BF16), pool_scale.reshape(n_g, 1, gd))


def _rms(a, g):
    return a * lax.rsqrt(jnp.mean(a * a, axis=-1, keepdims=True) + NORM_EPS) * g


def _outproj_kernel(dn_ref, pool_ref, x_ref, w1_ref, w2_ref, gt_ref, sc_ref, sh_ref,
                    gpost_ref, gpre_ref, x1_ref, h_ref):
    y = (jnp.dot(dn_ref[0], w1_ref[...], preferred_element_type=F32)
         + jnp.dot(pool_ref[0], w2_ref[...], preferred_element_type=F32))
    x1 = x_ref[0] + gt_ref[0] * _rms(y, gpost_ref[...])
    x1_ref[0] = x1
    h_ref[0] = (_rms(x1, gpre_ref[...]) * (1.0 + sc_ref[0]) + sh_ref[0]).astype(h_ref.dtype)


def _out_projection(dn, pool, x, w_out, gt1, sc2, sh2, g_post, g_pre, tm):
    b, t, d = x.shape
    wd = dn.shape[-1]
    w1 = w_out[:wd].astype(BF16)
    w2 = w_out[wd:].astype(BF16)
    row = lambda i, j: (i, j, 0)
    per_b = lambda i, j: (i, 0, 0)
    const = lambda i, j: (0, 0)
    return pl.pallas_call(
        _outproj_kernel,
        grid=(b, t // tm),
        in_specs=[pl.BlockSpec((1, tm, wd), row),
                  pl.BlockSpec((1, tm, pool.shape[-1]), row),
                  pl.BlockSpec((1, tm, d), row),
                  pl.BlockSpec(w1.shape, const),
                  pl.BlockSpec(w2.shape, const),
                  pl.BlockSpec((1, 1, d), per_b),
                  pl.BlockSpec((1, 1, d), per_b),
                  pl.BlockSpec((1, 1, d), per_b),
                  pl.BlockSpec((1, d), const),
                  pl.BlockSpec((1, d), const)],
        out_specs=[pl.BlockSpec((1, tm, d), row), pl.BlockSpec((1, tm, d), row)],
        out_shape=[jax.ShapeDtypeStruct((b, t, d), F32), jax.ShapeDtypeStruct((b, t, d), BF16)],
        compiler_params=_params("parallel", "parallel"),
        name="out_projection",
    )(dn, pool, x, w1, w2, gt1, sc2, sh2, g_post.reshape(1, d), g_pre.reshape(1, d))


FFN_HALO = 16


def _ffn_kernel(h_ref, hp_ref, hn_ref, x1_ref, wgu_ref, cw_ref, wd_ref, gt_ref, gpost_ref,
                out_ref, hext, act, *, fc, n_fc):
    j = pl.program_id(1)
    nj = pl.num_programs(1)
    tm = h_ref.shape[1]
    hal = FFN_HALO
    dff = fc * n_fc
    hext[0:hal, :] = jnp.where(j > 0, hp_ref[0], jnp.zeros_like(hp_ref[0]))
    hext[hal:hal + tm, :] = h_ref[0]
    hext[hal + tm:, :] = jnp.where(j < nj - 1, hn_ref[0], jnp.zeros_like(hn_ref[0]))
    for c in range(n_fc):
        gate = jnp.dot(hext[...], wgu_ref[:, c * fc:(c + 1) * fc], preferred_element_type=F32)
        up = jnp.dot(h_ref[0], wgu_ref[:, dff + c * fc:dff + (c + 1) * fc], preferred_element_type=F32)
        cw = cw_ref[:, c * fc:(c + 1) * fc]
        conv = (gate[hal - 1:hal - 1 + tm] * cw[0:1] + gate[hal:hal + tm] * cw[1:2]
                + gate[hal + 1:hal + 1 + tm] * cw[2:3])
        act[:, c * fc:(c + 1) * fc] = (_silu(conv) * up).astype(BF16)
    y = jnp.dot(act[...], wd_ref[...], preferred_element_type=F32)
    out_ref[0] = x1_ref[0] + gt_ref[0] * _rms(y, gpost_ref[...])


def _conv_ffn(h, x1, w_up, conv_ffn, w_down, gt2, g_post, tm, fc):
    b, t, d = x1.shape
    dff = w_down.shape[0]
    n_fc = dff // fc
    wgu = w_up.astype(BF16)
    hal = FFN_HALO
    nb = tm // hal
    last = t // hal - 1
    kern = functools.partial(_ffn_kernel, fc=fc, n_fc=n_fc)
    row = lambda i, j: (i, j, 0)
    per_b = lambda i, j: (i, 0, 0)
    const = lambda i, j: (0, 0)
    return pl.pallas_call(
        kern,
        grid=(b, t // tm),
        in_specs=[pl.BlockSpec((1, tm, d), row),
                  pl.BlockSpec((1, hal, d), lambda i, j: (i, jnp.maximum(j * nb - 1, 0), 0)),
                  pl.BlockSpec((1, hal, d), lambda i, j: (i, jnp.minimum((j + 1) * nb, last), 0)),
                  pl.BlockSpec((1, tm, d), row),
                  pl.BlockSpec(wgu.shape, const),
                  pl.BlockSpec(conv_ffn.shape, const),
                  pl.BlockSpec(w_down.shape, const),
                  pl.BlockSpec((1, 1, d), per_b),
                  pl.BlockSpec((1, d), const)],
        out_specs=pl.BlockSpec((1, tm, d), row),
        out_shape=jax.ShapeDtypeStruct((b, t, d), F32),
        scratch_shapes=[pltpu.VMEM((tm + 2 * hal, d), BF16), pltpu.VMEM((tm, dff), BF16)],
        compiler_params=_params("parallel", "parallel"),
        name="conv_ffn",
    )(h, h, h, x1, wgu, conv_ffn, w_down.astype(BF16), gt2, g_post.reshape(1, d))


def _layer(x, c, ctx, c_ctx, w_mod, b_mod, g_pre_mix, g_post_mix, g_pre_ffn, g_post_ffn,
           w_in, conv_qkv, a_log, dt_bias, o_norm, pool_w, pool_scale, w_out, w_up,
           conv_ffn, w_down):
    b, t, d = x.shape
    n_heads = a_log.shape[-1]
    w = n_heads * HEAD_DIM
    n_gate = 4 * n_heads

    rows = 16
    c_all = jnp.zeros((rows, d), F32).at[:b].set(c).at[b].set(c_ctx)
    m = _modulation(c_all, w_mod, b_mod)
    sh1, sc1, gt1, sh2, sc2, gt2 = [m[:b, None, i * d:(i + 1) * d] for i in range(N_MOD)]
    csh1 = jnp.broadcast_to(m[b, 0 * d:1 * d], (b, 1, d))
    csc1 = jnp.broadcast_to(m[b, 1 * d:2 * d], (b, 1, d))

    off_g = 4 * w
    w_all = jnp.concatenate([w_in[:, :off_g], w_in[:, off_g + n_gate:],
                             w_in[:, off_g:off_g + n_gate]], axis=1).astype(BF16)
    tm = min(512, t)
    qkv, z, pool_u, gates = _in_projection(x, sc1, sh1, g_pre_mix, w_all, a_log, dt_bias,
                                           n_heads, tm)
    qkv_c, _, _, gates_c = _in_projection(ctx, csc1, csh1, g_pre_mix, w_all, a_log, dt_bias,
                                          n_heads, ctx.shape[1])
    dn = _delta_mixer(qkv, qkv_c, gates, gates_c, z, conv_qkv, o_norm, n_heads)
    pool = _pool_mixer(pool_u, pool_w, pool_scale)
    x1, h2 = _out_projection(dn, pool, x, w_out, gt1, sc2, sh2, g_post_mix, g_pre_ffn, tm)
    return _conv_ffn(h2, x1, w_up, conv_ffn, w_down, gt2, g_post_ffn, tm, 256)


def kernel(x, c, ctx, c_ctx, w_mod, b_mod, g_pre_mix, g_post_mix, g_pre_ffn, g_post_ffn, w_in,
           conv_qkv, a_log, dt_bias, o_norm, pool_w, pool_scale, w_out, w_up, conv_ffn, w_down):
    depth = w_mod.shape[0]
    assert depth == 1, "context-stream update between layers is not implemented"
    return _layer(x, c, ctx, c_ctx, w_mod[0], b_mod[0], g_pre_mix[0], g_post_mix[0],
                  g_pre_ffn[0], g_post_ffn[0], w_in[0], conv_qkv[0], a_log[0], dt_bias[0],
                  o_norm[0], pool_w[0], pool_scale[0], w_out[0], w_up[0], conv_ffn[0], w_down[0])
```

```python
import functools

import numpy as np
import jax
import jax.numpy as jnp
from jax import lax
from jax.experimental import pallas as pl
from jax.experimental.pallas import tpu as pltpu

NORM_EPS = 1e-6
GRID_W = 64
POOL_WINDOWS = (2, 4, 8, 16)
N_MOD = 6
CHUNK = 128
HEAD_DIM = 128
VMEM_LIMIT = 56 * 1024 * 1024

F32 = jnp.float32
BF16 = jnp.bfloat16
HIGHEST = lax.Precision.HIGHEST


def _silu(a):
    return a * (1.0 / (1.0 + jnp.exp(-a)))


def _dot(a, b):
    return jnp.dot(a.astype(BF16), b.astype(BF16), preferred_element_type=F32)


def _dot_hi(a, b):
    return jnp.dot(a, b, preferred_element_type=F32, precision=HIGHEST)


def _params(*sem):
    return pltpu.CompilerParams(dimension_semantics=sem, vmem_limit_bytes=VMEM_LIMIT)


def _mod_kernel(c_ref, w_ref, b_ref, o_ref):
    o_ref[...] = _dot_hi(_silu(c_ref[...]), w_ref[...]) + b_ref[...]


def _modulation(c_all, w_mod, b_mod):
    r, d = c_all.shape
    n = w_mod.shape[1]
    tn = 512
    return pl.pallas_call(
        _mod_kernel,
        grid=(n // tn,),
        in_specs=[pl.BlockSpec((r, d), lambda j: (0, 0)),
                  pl.BlockSpec((d, tn), lambda j: (0, j)),
                  pl.BlockSpec((1, tn), lambda j: (0, j))],
        out_specs=pl.BlockSpec((r, tn), lambda j: (0, j)),
        out_shape=jax.ShapeDtypeStruct((r, n), F32),
        compiler_params=_params("parallel"),
        name="modulation",
    )(c_all, w_mod, b_mod.reshape(1, n))


LANES = 128


def _split3(a):
    hi = a.astype(BF16)
    r1 = a - hi.astype(F32)
    mid = r1.astype(BF16)
    lo = (r1 - mid.astype(F32)).astype(BF16)
    return hi, mid, lo


def _inproj_kernel(x_ref, sc_ref, sh_ref, g_ref, w_ref, alog_ref, dtb_ref, tri_ref,
                   qkv_ref, z_ref, pool_ref, gcol_ref, grow_ref, *, n_qkv, n_z, n_pool, n_heads):
    x = x_ref[0]
    xn = x * lax.rsqrt(jnp.mean(x * x, axis=-1, keepdims=True) + NORM_EPS) * g_ref[...]
    h = (xn * (1.0 + sc_ref[0]) + sh_ref[0]).astype(BF16)
    o1 = n_qkv
    o2 = o1 + n_z
    o3 = o2 + n_pool
    qkv_ref[0] = jnp.dot(h, w_ref[:, 0:o1], preferred_element_type=F32).astype(qkv_ref.dtype)
    z_ref[0] = jnp.dot(h, w_ref[:, o1:o2], preferred_element_type=F32).astype(z_ref.dtype)
    pool_ref[0] = jnp.dot(h, w_ref[:, o2:o3], preferred_element_type=F32).astype(pool_ref.dtype)
    pg = jnp.dot(h, w_ref[:, o3:], preferred_element_type=F32)
    beta = 1.0 / (1.0 + jnp.exp(-pg))
    a = pg + dtb_ref[...]
    softplus = jnp.maximum(a, 0.0) + jnp.log(1.0 + jnp.exp(-jnp.abs(a)))
    g = -jnp.exp(alog_ref[...]) * softplus
    tm = g.shape[0]
    n_ch = tm // CHUNK
    rows = 4 * n_heads
    beta_t = [beta[c * CHUNK:(c + 1) * CHUNK, :].T[0:rows, :] for c in range(n_ch)]
    g_t = [g[c * CHUNK:(c + 1) * CHUNK, :].T[0:rows, :] for c in range(n_ch)]
    parts = _split3(jnp.concatenate(g_t, axis=0))
    r = jnp.dot(jnp.concatenate(parts, axis=0), tri_ref[...], preferred_element_type=F32)
    m = n_ch * rows
    pre_all = r[0:m] + r[m:2 * m] + r[2 * m:3 * m]
    kind = lax.broadcasted_iota(jnp.int32, (rows, CHUNK), 0) & 3
    zpad = jnp.zeros((LANES - rows, CHUNK), F32)
    for c in range(n_ch):
        pre = pre_all[c * rows:(c + 1) * rows, :]
        suf = pre[:, CHUNK - 1:CHUNK] - pre + g_t[c]
        row = jnp.where(kind < 2, beta_t[c], jnp.where(kind == 2, pre, suf))
        grow_ref[0, :, c * CHUNK:(c + 1) * CHUNK] = row
        gcol_ref[0, c * CHUNK:(c + 1) * CHUNK, :] = jnp.concatenate([row, zpad], axis=0).T


def _in_projection(x, sc, sh, g_pre, w_all, a_log, dt_bias, n_heads, tm):
    b, t, d = x.shape
    w = n_heads * HEAD_DIM
    n_qkv, n_z = 3 * w, w
    n_pool = w_all.shape[1] - n_qkv - n_z - LANES
    assert CHUNK == LANES
    tri = jnp.asarray(np.triu(np.ones((CHUNK, CHUNK), np.float32)), BF16)
    lane_par = jnp.zeros((2, n_heads, 4), F32)
    lane_par = lane_par.at[0, :, 2:].set(a_log.T).at[1, :, 2:].set(dt_bias.T)
    lane_par = jnp.pad(lane_par.reshape(2, 4 * n_heads), ((0, 0), (0, LANES - 4 * n_heads)))
    kern = functools.partial(_inproj_kernel, n_qkv=n_qkv, n_z=n_z, n_pool=n_pool, n_heads=n_heads)
    row = lambda i, j: (i, j, 0)
    per_b = lambda i, j: (i, 0, 0)
    const = lambda i, j: (0, 0)
    return pl.pallas_call(
        kern,
        grid=(b, t // tm),
        in_specs=[pl.BlockSpec((1, tm, d), row),
                  pl.BlockSpec((1, 1, d), per_b),
                  pl.BlockSpec((1, 1, d), per_b),
                  pl.BlockSpec((1, d), const),
                  pl.BlockSpec(w_all.shape, const),
                  pl.BlockSpec((1, LANES), const),
                  pl.BlockSpec((1, LANES), const),
                  pl.BlockSpec((CHUNK, CHUNK), const)],
        out_specs=[pl.BlockSpec((1, tm, n_qkv), row),
                   pl.BlockSpec((1, tm, n_z), row),
                   pl.BlockSpec((1, tm, n_pool), row),
                   pl.BlockSpec((1, tm, LANES), row),
                   pl.BlockSpec((1, 4 * n_heads, tm), lambda i, j: (i, 0, j))],
        out_shape=[jax.ShapeDtypeStruct((b, t, n_qkv), BF16),
                   jax.ShapeDtypeStruct((b, t, n_z), BF16),
                   jax.ShapeDtypeStruct((b, t, n_pool), BF16),
                   jax.ShapeDtypeStruct((b, t, LANES), F32),
                   jax.ShapeDtypeStruct((b, 4 * n_heads, t), F32)],
        compiler_params=_params("parallel", "parallel"),
        name="in_projection",
    )(x, sc, sh, g_pre.reshape(1, d), w_all, lane_par[0:1], lane_par[1:2], tri)


CONV_HALO = 16


SUBLANES = 8


def _conv_silu(ref, cw_ref, win, c, n_chunks, post=None):
    hal = CONV_HALO
    r0 = pl.multiple_of(c * CHUNK, CHUNK)
    p0 = pl.multiple_of(jnp.maximum(r0 - hal, 0), hal)
    n0 = pl.multiple_of(jnp.minimum(r0 + CHUNK, n_chunks * CHUNK - hal), hal)
    win[0:hal, :] = jnp.where(c > 0, ref[0, pl.ds(p0, hal), :].astype(F32), 0.0)
    win[hal:hal + CHUNK, :] = ref[0, pl.ds(r0, CHUNK), :].astype(F32)
    win[hal + CHUNK:, :] = jnp.where(c < n_chunks - 1, ref[0, pl.ds(n0, hal), :].astype(F32), 0.0)
    width = cw_ref.shape[0]
    pad = width // 2
    n_t = CHUNK // SUBLANES
    taps = [cw_ref[j:j + 1, :] for j in range(width)]
    xs = [win[pl.ds(hal - pad + s, n_t, stride=SUBLANES), :] for s in range(SUBLANES + width - 1)]
    outs = []
    for r in range(SUBLANES):
        acc = xs[r] * taps[0]
        for j in range(1, width):
            acc = acc + xs[r + j] * taps[j]
        acc = _silu(acc)
        outs.append(acc if post is None else post(acc))
    for r in range(SUBLANES):
        win[pl.ds(hal + r, n_t, stride=SUBLANES), :] = outs[r]
    return win[hal:hal + CHUNK, :]


def _l2n(a):
    return a * lax.rsqrt(jnp.sum(a * a, axis=-1, keepdims=True) + NORM_EPS)


INV_BASE = 16
LOCAL_UNROLL = 4
HEAD_PAIR = 2


def _inv_unit(mats):
    n = mats[0].shape[0]
    ii = lax.broadcasted_iota(jnp.int32, (n, n), 0)
    jj = lax.broadcasted_iota(jnp.int32, (n, n), 1)

    def same_block(bits):
        return lax.shift_right_logical(ii, bits) == lax.shift_right_logical(jj, bits)

    bits = int(np.log2(INV_BASE))
    base = same_block(bits)
    eye = jnp.where(ii == jj, 1.0, 0.0)
    ms = [jnp.where(base, -a, 0.0) for a in mats]
    ps = [eye + m for m in ms]
    ms = [_dot(m, m) for m in ms]
    for _ in range(bits - 2):
        pm = [_dot(jnp.concatenate([p, m], axis=0), m) for p, m in zip(ps, ms)]
        ps = [p + x[:n] for p, x in zip(ps, pm)]
        ms = [x[n:] for x in pm]
    ps = [p + _dot(p, m) for p, m in zip(ps, ms)]
    while (1 << bits) < n:
        lvl = same_block(bits + 1) & jnp.logical_not(same_block(bits))
        tmp = [_dot(jnp.where(lvl, a, 0.0), p) for a, p in zip(mats, ps)]
        ps = [p - _dot(p, t) for p, t in zip(ps, tmp)]
        bits += 1
    return ps


def _chunk_local(chunks):
    c = chunks[0][1].shape[0]
    ii = lax.broadcasted_iota(jnp.int32, (c, c), 0)
    jj = lax.broadcasted_iota(jnp.int32, (c, c), 1)
    mats, rhs, ktd, attn, qe = [], [], [], [], []
    for q, k, kt, v, kk, qk, gcol, grow in chunks:
        for d in (0, 1):
            incl = (ii >= jj) if d == 0 else (ii <= jj)
            strict = (ii > jj) if d == 0 else (ii < jj)
            beta_c = gcol[:, d:d + 1]
            gc_c = gcol[:, 2 + d:3 + d]
            gc_r = grow[2 + d:3 + d, :]
            glast = gc_r[:, c - 1:c] if d == 0 else gc_r[:, 0:1]
            decay = jnp.where(incl, jnp.exp(jnp.where(incl, gc_c - gc_r, 0.0)), 0.0)
            e_c = jnp.exp(gc_c)
            mats.append(jnp.where(strict, kk * decay * beta_c, 0.0))
            rhs.append(jnp.concatenate([k * (beta_c * e_c), v * beta_c], axis=1).astype(BF16))
            ktd.append((kt * jnp.exp(glast - gc_r)).astype(BF16))
            attn.append(None if q is None else jnp.where(incl, qk * decay, 0.0).astype(BF16))
            qe.append(None if q is None else q * e_c)
    ts = _inv_unit(mats)
    wus = [_dot(t, r) for t, r in zip(ts, rhs)]
    hd = HEAD_DIM
    out = []
    for kd, a, wu, qe_ in zip(ktd, attn, wus, qe):
        if a is None:
            kn_ = _dot(kd, wu)
            out.append((kn_[:, :hd], kn_[:, hd:], None, None))
        else:
            x = _dot(jnp.concatenate([kd, a], axis=0), wu)
            out.append((x[:hd, :hd], x[:hd, hd:], qe_ - x[hd:, :hd], x[hd:, hd:]))
    return out


def _delta_kernel(q_ref, k_ref, v_ref, kc_ref, vc_ref, cwq_ref, cwk_ref, cwv_ref,
                  gcol_ref, grow_ref, gcolc_ref, growc_ref, z_ref, onorm_ref,
                  out_ref, kq_scr, n_scr, o_scr, s_ref, win_scr, stage_a, stage_b, *, n_x, n_c):
    n_tot = n_x + n_c
    hd = HEAD_DIM
    head_shift = lax.rem(LANES - 4 * pl.program_id(1), LANES)
    par = lax.rem(pl.program_id(1), HEAD_PAIR)

    def prep(src_q, src_k, src_v, c0, count, n_seq, stage):
        for j in range(count):
            c = c0 + j
            k = _conv_silu(src_k, cwk_ref, win_scr.at[3 * j], c, n_seq, _l2n)
            stage[j, 0] = k
            stage[j, 1] = k.T
            stage[j, 2] = _conv_silu(src_v, cwv_ref, win_scr.at[3 * j + 1], c, n_seq)
            if src_q is not None:
                stage[j, 3] = _conv_silu(src_q, cwq_ref, win_scr.at[3 * j + 2], c, n_seq,
                                         lambda a: _l2n(a) * (hd ** -0.5))

    def local(stage, with_q, gcol_r, grow_r, c0, count, slot0):
        chunks = []
        for j in range(count):
            c = c0 + j
            r0 = pl.multiple_of(c * CHUNK, CHUNK)
            gcol = pltpu.roll(gcol_r[0, pl.ds(r0, CHUNK), :], head_shift, axis=1)[:, 0:4]
            chunks.append([stage[j, 3] if with_q else None, stage[j, 0], stage[j, 1], stage[j, 2],
                           None, None, gcol, grow_r[0, par, c]])
        for ch in chunks:
            ch[4] = _dot(ch[1], ch[2])
        for ch in chunks:
            ch[5] = None if ch[0] is None else _dot(ch[0], ch[2])
        res = _chunk_local(chunks)
        for j in range(count):
            r0 = pl.multiple_of((c0 + j) * CHUNK, CHUNK)
            for d in (0, 1):
                kmat, nmat, qmat, omat = res[2 * j + d]
                idx = (par * 2 + d) * n_tot + slot0 + j
                kq_scr[idx, 0:hd, :] = kmat.astype(BF16)
                n_scr[idx] = nmat.astype(BF16)
                if qmat is not None:
                    kq_scr[idx, hd:, :] = qmat.astype(BF16)
                    o_scr[par * 2 + d, pl.ds(r0, CHUNK), :] = omat

    u_c = min(LOCAL_UNROLL, n_c)
    u_x = min(LOCAL_UNROLL, n_x)
    g_c, g_x = n_c // u_c, n_x // u_x
    stages = (stage_a, stage_b)

    def prep_x(g, stage):
        prep(q_ref, k_ref, v_ref, g * u_x, u_x, n_x, stage)

    def local_x(g, stage):
        local(stage, True, gcol_ref, grow_ref, g * u_x, u_x, n_c + g * u_x)

    prep(None, kc_ref, vc_ref, 0, u_c, n_c, stages[0])
    for g in range(g_c):
        local(stages[g % 2], False, gcolc_ref, growc_ref, g * u_c, u_c, g * u_c)
        if g + 1 < g_c:
            prep(None, kc_ref, vc_ref, (g + 1) * u_c, u_c, n_c, stages[(g + 1) % 2])
        else:
            prep_x(0, stages[(g + 1) % 2])

    if g_x % 2 == 0:
        def local_pair(p, carry):
            for h in (0, 1):
                g = 2 * p + h
                local_x(g, stages[(g_c + h) % 2])
                prep_x(jnp.minimum(g + 1, g_x - 1), stages[(g_c + h + 1) % 2])
            return carry

        lax.fori_loop(0, g_x // 2, local_pair, 0)
    else:
        for g in range(g_x):
            local_x(g, stages[(g_c + g) % 2])
            if g + 1 < g_x:
                prep_x(g + 1, stages[(g_c + g + 1) % 2])
    @pl.when(par == HEAD_PAIR - 1)
    def _sequential():
        s_ref[...] = jnp.zeros_like(s_ref)
        chains = [(pp, d) for pp in range(HEAD_PAIR) for d in (0, 1)]

        def state_decay(grow, d):
            tot = grow[2 + d:3 + d, CHUNK - 1:CHUNK] if d == 0 else grow[2 + d:3 + d, 0:1]
            return jnp.exp(tot)

        def seq_ctx(i, carry):
            for pp, d in chains:
                c = i if d == 0 else n_c - 1 - i
                idx = (pp * 2 + d) * n_tot + c
                s = s_ref[pp * 2 + d]
                ks = jnp.dot(kq_scr[idx, 0:hd, :], s.astype(BF16), preferred_element_type=F32)
                s_ref[pp * 2 + d] = (s * state_decay(growc_ref[0, pp, c], d) - ks
                                     + n_scr[idx].astype(F32))
            return carry

        def finish(pp, r0, o):
            lanes = slice(pp * hd, (pp + 1) * hd)
            on = o * lax.rsqrt(jnp.mean(o * o, axis=-1, keepdims=True) + NORM_EPS) * onorm_ref[...]
            zg = _silu(z_ref[0, pl.ds(r0, CHUNK), lanes].astype(F32))
            out_ref[0, pl.ds(r0, CHUNK), lanes] = (on * zg).astype(out_ref.dtype)

        def seq_x(i, carry, second_visit):
            for pp, d in chains:
                c = i if d == 0 else n_x - 1 - i
                r0 = pl.multiple_of(c * CHUNK, CHUNK)
                idx = (pp * 2 + d) * n_tot + n_c + c
                s = s_ref[pp * 2 + d]
                kqs = jnp.dot(kq_scr[idx], s.astype(BF16), preferred_element_type=F32)
                o = o_scr[pp * 2 + d, pl.ds(r0, CHUNK), :] + kqs[hd:, :]
                if second_visit:
                    finish(pp, r0, o + o_scr[pp * 2 + 1 - d, pl.ds(r0, CHUNK), :])
                else:
                    o_scr[pp * 2 + d, pl.ds(r0, CHUNK), :] = o
                s_ref[pp * 2 + d] = (s * state_decay(grow_ref[0, pp, c], d) - kqs[0:hd, :]
                                     + n_scr[idx].astype(F32))
            return carry

        lax.fori_loop(0, n_c, seq_ctx, 0)
        half = (n_x + 1) // 2
        lax.fori_loop(0, half, functools.partial(seq_x, second_visit=False), 0)
        if n_x % 2:
            r_mid = (half - 1) * CHUNK
            for pp in range(HEAD_PAIR):
                finish(pp, r_mid, o_scr[pp * 2, r_mid:r_mid + CHUNK, :] + o_scr[pp * 2 + 1, r_mid:r_mid + CHUNK, :])
        lax.fori_loop(half, n_x, functools.partial(seq_x, second_visit=True), 0)


def _row_gates(grow, n_heads):
    b, _, t = grow.shape
    return jnp.transpose(grow.reshape(b, n_heads, 4, t // CHUNK, CHUNK), (0, 1, 3, 2, 4))


def _delta_mixer(qkv, qkv_c, gcol, grow, gcolc, growc, z, conv_qkv, o_norm, n_heads):
    b, t, _ = qkv.shape
    tc = qkv_c.shape[1]
    n_x, n_c = t // CHUNK, tc // CHUNK
    hd = HEAD_DIM
    assert CHUNK == hd and t % CHUNK == 0 and tc % CHUNK == 0
    width = conv_qkv.shape[0]
    grow = _row_gates(grow, n_heads)
    growc = _row_gates(growc, n_heads)
    kern = functools.partial(_delta_kernel, n_x=n_x, n_c=n_c)
    nh = n_heads

    def colblk(off):
        return lambda i, h: (i, 0, off + h)

    def cwblk(off):
        return lambda i, h: (0, off + h)

    assert nh % HEAD_PAIR == 0
    pair5 = lambda i, h: (i, h // HEAD_PAIR, 0, 0, 0)
    pair3 = lambda i, h: (i, 0, h // HEAD_PAIR)
    return pl.pallas_call(
        kern,
        grid=(b, nh),
        in_specs=[pl.BlockSpec((1, t, hd), colblk(0)),
                  pl.BlockSpec((1, t, hd), colblk(nh)),
                  pl.BlockSpec((1, t, hd), colblk(2 * nh)),
                  pl.BlockSpec((1, tc, hd), colblk(nh)),
                  pl.BlockSpec((1, tc, hd), colblk(2 * nh)),
                  pl.BlockSpec((width, hd), cwblk(0)),
                  pl.BlockSpec((width, hd), cwblk(nh)),
                  pl.BlockSpec((width, hd), cwblk(2 * nh)),
                  pl.BlockSpec((1, t, LANES), lambda i, h: (i, 0, 0)),
                  pl.BlockSpec((1, HEAD_PAIR, n_x, 4, CHUNK), pair5),
                  pl.BlockSpec((1, tc, LANES), lambda i, h: (i, 0, 0)),
                  pl.BlockSpec((1, HEAD_PAIR, n_c, 4, CHUNK), pair5),
                  pl.BlockSpec((1, t, HEAD_PAIR * hd), pair3),
                  pl.BlockSpec((1, hd), lambda i, h: (0, 0))],
        out_specs=pl.BlockSpec((1, t, HEAD_PAIR * hd), pair3),
        out_shape=jax.ShapeDtypeStruct((b, t, nh * hd), BF16),
        scratch_shapes=[pltpu.VMEM((2 * HEAD_PAIR * (n_x + n_c), 2 * hd, hd), BF16),
                        pltpu.VMEM((2 * HEAD_PAIR * (n_x + n_c), hd, hd), BF16),
                        pltpu.VMEM((2 * HEAD_PAIR, t, hd), F32),
                        pltpu.VMEM((2 * HEAD_PAIR, hd, hd), F32),
                        pltpu.VMEM((3 * LOCAL_UNROLL, CHUNK + 2 * CONV_HALO, hd), F32),
                        pltpu.VMEM((LOCAL_UNROLL, 4, CHUNK, hd), F32),
                        pltpu.VMEM((LOCAL_UNROLL, 4, CHUNK, hd), F32)],
        compiler_params=_params("parallel", "arbitrary"),
        name="delta_mixer",
    )(qkv, qkv, qkv, qkv_c, qkv_c, conv_qkv, conv_qkv, conv_qkv,
      gcol, grow, gcolc, growc, z, o_norm.reshape(1, hd))


POOL_UNROLL = 2


def _pool_kernel(u_ref, band_ref, icnt_ref, w_ref, sc_ref, out_ref, cs, *, rows, blk):
    g = pl.program_id(1)
    left = lax.shift_left(jnp.int32(1), g)
    right = left - 1
    gw = GRID_W
    gd = cs.shape[1]
    t = rows * gw

    cs[0:gw, :] = jnp.zeros((gw, gd), F32)

    def prefix(r, acc):
        acc = acc + u_ref[0, pl.ds(pl.multiple_of(r * gw, gw), gw), :].astype(F32)
        cs[pl.ds(pl.multiple_of((r + 1) * gw, gw), gw), :] = acc
        return acc

    lax.fori_loop(0, rows, prefix, jnp.zeros((gw, gd), F32))

    def row_mean(r):
        lo = jnp.maximum(r - left, 0)
        hi = jnp.minimum(r + right + 1, rows)
        tot = cs[pl.ds(pl.multiple_of(hi * gw, gw), gw), :] - cs[pl.ds(pl.multiple_of(lo * gw, gw), gw), :]
        return tot / jnp.full((gw, gd), hi - lo, jnp.int32).astype(F32)

    def col_body(i, carry):
        blocks = [i * POOL_UNROLL + j for j in range(POOL_UNROLL)]
        starts = [pl.multiple_of(bi * blk, blk) for bi in blocks]
        rs = [jnp.concatenate([row_mean(bi * (blk // gw) + rr) for rr in range(blk // gw)], axis=0)
              for bi in blocks]
        ms = [jnp.dot(band_ref[0], r.astype(BF16), preferred_element_type=F32) for r in rs]
        ds = [(m * icnt_ref[0] - u_ref[0, pl.ds(r0, blk), :].astype(F32)).astype(BF16)
              for m, r0 in zip(ms, starts)]
        ys = [jnp.dot(dlt, w_ref[0], preferred_element_type=F32) for dlt in ds]
        for y, r0 in zip(ys, starts):
            out_ref[0, pl.ds(r0, blk), :] = (y * sc_ref[0]).astype(out_ref.dtype)
        return carry

    lax.fori_loop(0, t // (blk * POOL_UNROLL), col_body, 0)


def _pool_consts(blk):
    band = np.zeros((len(POOL_WINDOWS), blk, blk), np.float32)
    icnt = np.zeros((len(POOL_WINDOWS), blk, HEAD_DIM), np.float32)
    for gi, w in enumerate(POOL_WINDOWS):
        left = w // 2
        right = w - 1 - left
        for i in range(blk):
            base, col = (i // GRID_W) * GRID_W, i % GRID_W
            lo, hi = max(col - left, 0), min(col + right, GRID_W - 1)
            band[gi, i, base + lo:base + hi + 1] = 1.0
            icnt[gi, i, :] = 1.0 / (hi - lo + 1)
    return jnp.asarray(band, BF16), jnp.asarray(icnt)


def _pool_mixer(u, pool_w, pool_scale):
    b, t, p = u.shape
    n_g, gd = pool_w.shape[0], pool_w.shape[1]
    rows = t // GRID_W
    blk = 4 * GRID_W
    band, icnt = _pool_consts(blk)
    kern = functools.partial(_pool_kernel, rows=rows, blk=blk)
    grp = lambda i, g: (g, 0, 0)
    return pl.pallas_call(
        kern,
        grid=(b, n_g),
        in_specs=[pl.BlockSpec((1, t, gd), lambda i, g: (i, 0, g)),
                  pl.BlockSpec((1, blk, blk), grp),
                  pl.BlockSpec((1, blk, gd), grp),
                  pl.BlockSpec((1, gd, gd), grp),
                  pl.BlockSpec((1, 1, gd), grp)],
        out_specs=pl.BlockSpec((1, t, gd), lambda i, g: (i, 0, g)),
        out_shape=jax.ShapeDtypeStruct((b, t, p), BF16),
        scratch_shapes=[pltpu.VMEM((t + GRID_W, gd), F32)],
        compiler_params=_params("parallel", "parallel"),
        name="pool_mixer",
    )(u, band, icnt, pool_w.astype(BF16), pool_scale.reshape(n_g, 1, gd))


def _rms(a, g):
    return a * lax.rsqrt(jnp.mean(a * a, axis=-1, keepdims=True) + NORM_EPS) * g


def _outproj_kernel(dn_ref, pool_ref, x_ref, w1_ref, w2_ref, gt_ref, sc_ref, sh_ref,
                    gpost_ref, gpre_ref, x1_ref, h_ref):
    y = (jnp.dot(dn_ref[0], w1_ref[...], preferred_element_type=F32)
         + jnp.dot(pool_ref[0], w2_ref[...], preferred_element_type=F32))
    x1 = x_ref[0] + gt_ref[0] * _rms(y, gpost_ref[...])
    x1_ref[0] = x1
    h_ref[0] = (_rms(x1, gpre_ref[...]) * (1.0 + sc_ref[0]) + sh_ref[0]).astype(h_ref.dtype)


def _out_projection(dn, pool, x, w_out, gt1, sc2, sh2, g_post, g_pre, tm):
    b, t, d = x.shape
    wd = dn.shape[-1]
    w1 = w_out[:wd].astype(BF16)
    w2 = w_out[wd:].astype(BF16)
    row = lambda i, j: (i, j, 0)
    per_b = lambda i, j: (i, 0, 0)
    const = lambda i, j: (0, 0)
    return pl.pallas_call(
        _outproj_kernel,
        grid=(b, t // tm),
        in_specs=[pl.BlockSpec((1, tm, wd), row),
                  pl.BlockSpec((1, tm, pool.shape[-1]), row),
                  pl.BlockSpec((1, tm, d), row),
                  pl.BlockSpec(w1.shape, const),
                  pl.BlockSpec(w2.shape, const),
                  pl.BlockSpec((1, 1, d), per_b),
                  pl.BlockSpec((1, 1, d), per_b),
                  pl.BlockSpec((1, 1, d), per_b),
                  pl.BlockSpec((1, d), const),
                  pl.BlockSpec((1, d), const)],
        out_specs=[pl.BlockSpec((1, tm, d), row), pl.BlockSpec((1, tm, d), row)],
        out_shape=[jax.ShapeDtypeStruct((b, t, d), F32), jax.ShapeDtypeStruct((b, t, d), BF16)],
        compiler_params=_params("parallel", "parallel"),
        name="out_projection",
    )(dn, pool, x, w1, w2, gt1, sc2, sh2, g_post.reshape(1, d), g_pre.reshape(1, d))


FFN_HALO = 16


def _ffn_kernel(h_ref, hp_ref, hn_ref, x1_ref, wgu_ref, cw_ref, wd_ref, gt_ref, gpost_ref,
                out_ref, hext, act, *, fc, n_fc):
    j = pl.program_id(1)
    nj = pl.num_programs(1)
    tm = h_ref.shape[1]
    hal = FFN_HALO
    dff = fc * n_fc
    hext[0:hal, :] = jnp.where(j > 0, hp_ref[0], jnp.zeros_like(hp_ref[0]))
    hext[hal:hal + tm, :] = h_ref[0]
    hext[hal + tm:, :] = jnp.where(j < nj - 1, hn_ref[0], jnp.zeros_like(hn_ref[0]))
    for c in range(n_fc):
        gate = jnp.dot(hext[...], wgu_ref[:, c * fc:(c + 1) * fc], preferred_element_type=F32)
        up = jnp.dot(h_ref[0], wgu_ref[:, dff + c * fc:dff + (c + 1) * fc], preferred_element_type=F32)
        cw = cw_ref[:, c * fc:(c + 1) * fc]
        conv = (gate[hal - 1:hal - 1 + tm] * cw[0:1] + gate[hal:hal + tm] * cw[1:2]
                + gate[hal + 1:hal + 1 + tm] * cw[2:3])
        act[:, c * fc:(c + 1) * fc] = (_silu(conv) * up).astype(BF16)
    y = jnp.dot(act[...], wd_ref[...], preferred_element_type=F32)
    out_ref[0] = x1_ref[0] + gt_ref[0] * _rms(y, gpost_ref[...])


def _conv_ffn(h, x1, w_up, conv_ffn, w_down, gt2, g_post, tm, fc):
    b, t, d = x1.shape
    dff = w_down.shape[0]
    n_fc = dff // fc
    wgu = w_up.astype(BF16)
    hal = FFN_HALO
    nb = tm // hal
    last = t // hal - 1
    kern = functools.partial(_ffn_kernel, fc=fc, n_fc=n_fc)
    row = lambda i, j: (i, j, 0)
    per_b = lambda i, j: (i, 0, 0)
    const = lambda i, j: (0, 0)
    return pl.pallas_call(
        kern,
        grid=(b, t // tm),
        in_specs=[pl.BlockSpec((1, tm, d), row),
                  pl.BlockSpec((1, hal, d), lambda i, j: (i, jnp.maximum(j * nb - 1, 0), 0)),
                  pl.BlockSpec((1, hal, d), lambda i, j: (i, jnp.minimum((j + 1) * nb, last), 0)),
                  pl.BlockSpec((1, tm, d), row),
                  pl.BlockSpec(wgu.shape, const),
                  pl.BlockSpec(conv_ffn.shape, const),
                  pl.BlockSpec(w_down.shape, const),
                  pl.BlockSpec((1, 1, d), per_b),
                  pl.BlockSpec((1, d), const)],
        out_specs=pl.BlockSpec((1, tm, d), row),
        out_shape=jax.ShapeDtypeStruct((b, t, d), F32),
        scratch_shapes=[pltpu.VMEM((tm + 2 * hal, d), BF16), pltpu.VMEM((tm, dff), BF16)],
        compiler_params=_params("parallel", "parallel"),
        name="conv_ffn",
    )(h, h, h, x1, wgu, conv_ffn, w_down.astype(BF16), gt2, g_post.reshape(1, d))


def _layer(x, c, ctx, c_ctx, w_mod, b_mod, g_pre_mix, g_post_mix, g_pre_ffn, g_post_ffn,
           w_in, conv_qkv, a_log, dt_bias, o_norm, pool_w, pool_scale, w_out, w_up,
           conv_ffn, w_down):
    b, t, d = x.shape
    n_heads = a_log.shape[-1]
    w = n_heads * HEAD_DIM
    n_gate = 4 * n_heads

    rows = 16
    c_all = jnp.zeros((rows, d), F32).at[:b].set(c).at[b].set(c_ctx)
    m = _modulation(c_all, w_mod, b_mod)
    sh1, sc1, gt1, sh2, sc2, gt2 = [m[:b, None, i * d:(i + 1) * d] for i in range(N_MOD)]
    csh1 = jnp.broadcast_to(m[b, 0 * d:1 * d], (b, 1, d))
    csc1 = jnp.broadcast_to(m[b, 1 * d:2 * d], (b, 1, d))

    off_g = 4 * w
    w_gate = jnp.transpose(w_in[:, off_g:off_g + n_gate].reshape(d, 4, n_heads), (0, 2, 1)).reshape(d, n_gate)
    w_all = jnp.concatenate([w_in[:, :off_g], w_in[:, off_g + n_gate:],
                             jnp.pad(w_gate, ((0, 0), (0, LANES - n_gate)))], axis=1).astype(BF16)
    tm = min(512, t)
    qkv, z, pool_u, gcol, grow = _in_projection(x, sc1, sh1, g_pre_mix, w_all, a_log, dt_bias, n_heads, tm)
    qkv_c, _, _, gcolc, growc = _in_projection(ctx, csc1, csh1, g_pre_mix, w_all, a_log, dt_bias,
                                                n_heads, ctx.shape[1])
    dn = _delta_mixer(qkv, qkv_c, gcol, grow, gcolc, growc, z, conv_qkv, o_norm, n_heads)
    pool = _pool_mixer(pool_u, pool_w, pool_scale)
    x1, h2 = _out_projection(dn, pool, x, w_out, gt1, sc2, sh2, g_post_mix, g_pre_ffn, tm)
    return _conv_ffn(h2, x1, w_up, conv_ffn, w_down, gt2, g_post_ffn, tm, 256)


def kernel(x, c, ctx, c_ctx, w_mod, b_mod, g_pre_mix, g_post_mix, g_pre_ffn, g_post_ffn, w_in,
           conv_qkv, a_log, dt_bias, o_norm, pool_w, pool_scale, w_out, w_up, conv_ffn, w_down):
    depth = w_mod.shape[0]
    assert depth == 1, "context-stream update between layers is not implemented"
    return _layer(x, c, ctx, c_ctx, w_mod[0], b_mod[0], g_pre_mix[0], g_post_mix[0],
                  g_pre_ffn[0], g_post_ffn[0], w_in[0], conv_qkv[0], a_log[0], dt_bias[0],
                  o_norm[0], pool_w[0], pool_scale[0], w_out[0], w_up[0], conv_ffn[0], w_down[0])
```

```python
import functools

import numpy as np
import jax
import jax.numpy as jnp
from jax import lax
from jax.experimental import pallas as pl
from jax.experimental.pallas import tpu as pltpu

NORM_EPS = 1e-6
GRID_W = 64
POOL_WINDOWS = (2, 4, 8, 16)
N_MOD = 6
CHUNK = 128
HEAD_DIM = 128
VMEM_LIMIT = 56 * 1024 * 1024

F32 = jnp.float32
BF16 = jnp.bfloat16
HIGHEST = lax.Precision.HIGHEST


def _silu(a):
    return a * (1.0 / (1.0 + jnp.exp(-a)))


def _dot(a, b):
    return jnp.dot(a.astype(BF16), b.astype(BF16), preferred_element_type=F32)


def _bdot(a, b):
    return jnp.dot(a, b, preferred_element_type=F32).astype(BF16)


def _dot_hi(a, b):
    return jnp.dot(a, b, preferred_element_type=F32, precision=HIGHEST)


def _params(*sem):
    return pltpu.CompilerParams(dimension_semantics=sem, vmem_limit_bytes=VMEM_LIMIT)


def _mod_kernel(c_ref, w_ref, b_ref, o_ref):
    o_ref[...] = _dot_hi(_silu(c_ref[...]), w_ref[...]) + b_ref[...]


def _modulation(c_all, w_mod, b_mod):
    r, d = c_all.shape
    n = w_mod.shape[1]
    tn = 512
    return pl.pallas_call(
        _mod_kernel,
        grid=(n // tn,),
        in_specs=[pl.BlockSpec((r, d), lambda j: (0, 0)),
                  pl.BlockSpec((d, tn), lambda j: (0, j)),
                  pl.BlockSpec((1, tn), lambda j: (0, j))],
        out_specs=pl.BlockSpec((r, tn), lambda j: (0, j)),
        out_shape=jax.ShapeDtypeStruct((r, n), F32),
        compiler_params=_params("parallel"),
        name="modulation",
    )(c_all, w_mod, b_mod.reshape(1, n))


LANES = 128


def _split3(a):
    hi = a.astype(BF16)
    r1 = a - hi.astype(F32)
    mid = r1.astype(BF16)
    lo = (r1 - mid.astype(F32)).astype(BF16)
    return hi, mid, lo


def _inproj_kernel(x_ref, sc_ref, sh_ref, g_ref, w_ref, alog_ref, dtb_ref, tri_ref,
                   qkv_ref, z_ref, pool_ref, gcol_ref, grow_ref, *, n_qkv, n_z, n_pool, n_heads):
    x = x_ref[0]
    xn = x * lax.rsqrt(jnp.mean(x * x, axis=-1, keepdims=True) + NORM_EPS) * g_ref[...]
    h = (xn * (1.0 + sc_ref[0]) + sh_ref[0]).astype(BF16)
    o1 = n_qkv
    o2 = o1 + n_z
    o3 = o2 + n_pool
    qkv_ref[0] = jnp.dot(h, w_ref[:, 0:o1], preferred_element_type=F32).astype(qkv_ref.dtype)
    z_ref[0] = jnp.dot(h, w_ref[:, o1:o2], preferred_element_type=F32).astype(z_ref.dtype)
    pool_ref[0] = jnp.dot(h, w_ref[:, o2:o3], preferred_element_type=F32).astype(pool_ref.dtype)
    pg = jnp.dot(h, w_ref[:, o3:], preferred_element_type=F32)
    beta = 1.0 / (1.0 + jnp.exp(-pg))
    a = pg + dtb_ref[...]
    softplus = jnp.maximum(a, 0.0) + jnp.log(1.0 + jnp.exp(-jnp.abs(a)))
    g = -jnp.exp(alog_ref[...]) * softplus
    tm = g.shape[0]
    n_ch = tm // CHUNK
    rows = 4 * n_heads
    beta_t = [beta[c * CHUNK:(c + 1) * CHUNK, :].T[0:rows, :] for c in range(n_ch)]
    g_t = [g[c * CHUNK:(c + 1) * CHUNK, :].T[0:rows, :] for c in range(n_ch)]
    parts = _split3(jnp.concatenate(g_t, axis=0))
    r = jnp.dot(jnp.concatenate(parts, axis=0), tri_ref[...], preferred_element_type=F32)
    m = n_ch * rows
    pre_all = r[0:m] + r[m:2 * m] + r[2 * m:3 * m]
    kind = lax.broadcasted_iota(jnp.int32, (rows, CHUNK), 0) & 3
    zpad = jnp.zeros((LANES - rows, CHUNK), F32)
    for c in range(n_ch):
        pre = pre_all[c * rows:(c + 1) * rows, :]
        suf = pre[:, CHUNK - 1:CHUNK] - pre + g_t[c]
        row = jnp.where(kind < 2, beta_t[c], jnp.where(kind == 2, pre, suf))
        grow_ref[0, :, c * CHUNK:(c + 1) * CHUNK] = row
        gcol_ref[0, c * CHUNK:(c + 1) * CHUNK, :] = jnp.concatenate([row, zpad], axis=0).T


def _in_projection(x, sc, sh, g_pre, w_all, a_log, dt_bias, n_heads, tm):
    b, t, d = x.shape
    w = n_heads * HEAD_DIM
    n_qkv, n_z = 3 * w, w
    n_pool = w_all.shape[1] - n_qkv - n_z - LANES
    assert CHUNK == LANES
    tri = jnp.asarray(np.triu(np.ones((CHUNK, CHUNK), np.float32)), BF16)
    lane_par = jnp.zeros((2, n_heads, 4), F32)
    lane_par = lane_par.at[0, :, 2:].set(a_log.T).at[1, :, 2:].set(dt_bias.T)
    lane_par = jnp.pad(lane_par.reshape(2, 4 * n_heads), ((0, 0), (0, LANES - 4 * n_heads)))
    kern = functools.partial(_inproj_kernel, n_qkv=n_qkv, n_z=n_z, n_pool=n_pool, n_heads=n_heads)
    row = lambda i, j: (i, j, 0)
    per_b = lambda i, j: (i, 0, 0)
    const = lambda i, j: (0, 0)
    return pl.pallas_call(
        kern,
        grid=(b, t // tm),
        in_specs=[pl.BlockSpec((1, tm, d), row),
                  pl.BlockSpec((1, 1, d), per_b),
                  pl.BlockSpec((1, 1, d), per_b),
                  pl.BlockSpec((1, d), const),
                  pl.BlockSpec(w_all.shape, const),
                  pl.BlockSpec((1, LANES), const),
                  pl.BlockSpec((1, LANES), const),
                  pl.BlockSpec((CHUNK, CHUNK), const)],
        out_specs=[pl.BlockSpec((1, tm, n_qkv), row),
                   pl.BlockSpec((1, tm, n_z), row),
                   pl.BlockSpec((1, tm, n_pool), row),
                   pl.BlockSpec((1, tm, LANES), row),
                   pl.BlockSpec((1, 4 * n_heads, tm), lambda i, j: (i, 0, j))],
        out_shape=[jax.ShapeDtypeStruct((b, t, n_qkv), BF16),
                   jax.ShapeDtypeStruct((b, t, n_z), BF16),
                   jax.ShapeDtypeStruct((b, t, n_pool), BF16),
                   jax.ShapeDtypeStruct((b, t, LANES), F32),
                   jax.ShapeDtypeStruct((b, 4 * n_heads, t), F32)],
        compiler_params=_params("parallel", "parallel"),
        name="in_projection",
    )(x, sc, sh, g_pre.reshape(1, d), w_all, lane_par[0:1], lane_par[1:2], tri)


CONV_HALO = 16


SUBLANES = 8


def _conv_silu(ref, cw_ref, win, c, n_chunks, post=None):
    hal = CONV_HALO
    r0 = pl.multiple_of(c * CHUNK, CHUNK)
    p0 = pl.multiple_of(jnp.maximum(r0 - hal, 0), hal)
    n0 = pl.multiple_of(jnp.minimum(r0 + CHUNK, n_chunks * CHUNK - hal), hal)
    win[0:hal, :] = jnp.where(c > 0, ref[0, pl.ds(p0, hal), :].astype(F32), 0.0)
    win[hal:hal + CHUNK, :] = ref[0, pl.ds(r0, CHUNK), :].astype(F32)
    win[hal + CHUNK:, :] = jnp.where(c < n_chunks - 1, ref[0, pl.ds(n0, hal), :].astype(F32), 0.0)
    width = cw_ref.shape[0]
    pad = width // 2
    n_t = CHUNK // SUBLANES
    taps = [cw_ref[j:j + 1, :] for j in range(width)]
    xs = [win[pl.ds(hal - pad + s, n_t, stride=SUBLANES), :] for s in range(SUBLANES + width - 1)]
    outs = []
    for r in range(SUBLANES):
        acc = xs[r] * taps[0]
        for j in range(1, width):
            acc = acc + xs[r + j] * taps[j]
        acc = _silu(acc)
        outs.append(acc if post is None else post(acc))
    for r in range(SUBLANES):
        win[pl.ds(hal + r, n_t, stride=SUBLANES), :] = outs[r]
    return win[hal:hal + CHUNK, :]


def _l2n(a):
    return a * lax.rsqrt(jnp.sum(a * a, axis=-1, keepdims=True) + NORM_EPS)


INV_BASE = 32
LOCAL_UNROLL = 4
HEAD_PAIR = 2


def _inv_unit(mats):
    n = mats[0].shape[0]
    ii = lax.broadcasted_iota(jnp.int32, (n, n), 0)
    jj = lax.broadcasted_iota(jnp.int32, (n, n), 1)

    def same_block(bits):
        return lax.shift_right_logical(ii, bits) == lax.shift_right_logical(jj, bits)

    bits = int(np.log2(INV_BASE))
    base = same_block(bits)
    eye = jnp.where(ii == jj, 1.0, 0.0).astype(BF16)
    zero = jnp.zeros((), BF16)
    abf = [a.astype(BF16) for a in mats]
    ms = [jnp.where(base, -a, zero) for a in abf]
    ps = [eye + m for m in ms]
    ms = [_bdot(m, m) for m in ms]
    for _ in range(bits - 2):
        pm = [_bdot(jnp.concatenate([p, m], axis=0), m) for p, m in zip(ps, ms)]
        ps = [p + x[:n] for p, x in zip(ps, pm)]
        ms = [x[n:] for x in pm]
    ps = [p + _bdot(p, m) for p, m in zip(ps, ms)]
    while (1 << bits) < n:
        lvl = same_block(bits + 1) & jnp.logical_not(same_block(bits))
        tmp = [_bdot(jnp.where(lvl, a, zero), p) for a, p in zip(abf, ps)]
        ps = [p - _bdot(p, t) for p, t in zip(ps, tmp)]
        bits += 1
    return ps


def _chunk_local(chunks):
    c = chunks[0][1].shape[0]
    ii = lax.broadcasted_iota(jnp.int32, (c, c), 0)
    jj = lax.broadcasted_iota(jnp.int32, (c, c), 1)
    mats, rhs, ktd, attn, qe = [], [], [], [], []
    for q, k, kt, v, kk, qk, gcol, grow in chunks:
        for d in (0, 1):
            incl = (ii >= jj) if d == 0 else (ii <= jj)
            strict = (ii > jj) if d == 0 else (ii < jj)
            beta_c = gcol[:, d:d + 1]
            gc_c = gcol[:, 2 + d:3 + d]
            gc_r = grow[2 + d:3 + d, :]
            glast = gc_r[:, c - 1:c] if d == 0 else gc_r[:, 0:1]
            decay = jnp.where(incl, jnp.exp(jnp.where(incl, gc_c - gc_r, 0.0)), 0.0)
            e_c = jnp.exp(gc_c)
            mats.append(jnp.where(strict, kk * decay * beta_c, 0.0))
            rhs.append(jnp.concatenate([k * (beta_c * e_c), v * beta_c], axis=1).astype(BF16))
            ktd.append((kt * jnp.exp(glast - gc_r)).astype(BF16))
            attn.append(None if q is None else jnp.where(incl, qk * decay, 0.0).astype(BF16))
            qe.append(None if q is None else q * e_c)
    ts = _inv_unit(mats)
    wus = [_dot(t, r) for t, r in zip(ts, rhs)]
    hd = HEAD_DIM
    out = []
    for kd, a, wu, qe_ in zip(ktd, attn, wus, qe):
        if a is None:
            kn_ = _dot(kd, wu)
            out.append((kn_[:, :hd], kn_[:, hd:], None, None))
        else:
            x = _dot(jnp.concatenate([kd, a], axis=0), wu)
            out.append((x[:hd, :hd], x[:hd, hd:], qe_ - x[hd:, :hd], x[hd:, hd:]))
    return out


def _delta_kernel(q_ref, k_ref, v_ref, kc_ref, vc_ref, cwq_ref, cwk_ref, cwv_ref,
                  gcol_ref, grow_ref, gcolc_ref, growc_ref, z_ref, onorm_ref,
                  out_ref, kq_scr, n_scr, o_scr, s_ref, win_scr, stage_a, stage_b, *, n_x, n_c):
    n_tot = n_x + n_c
    hd = HEAD_DIM
    head_shift = lax.rem(LANES - 4 * pl.program_id(1), LANES)
    par = lax.rem(pl.program_id(1), HEAD_PAIR)

    def prep(src_q, src_k, src_v, c0, count, n_seq, stage):
        for j in range(count):
            c = c0 + j
            k = _conv_silu(src_k, cwk_ref, win_scr.at[3 * j], c, n_seq, _l2n)
            stage[j, 0] = k
            stage[j, 1] = k.T
            stage[j, 2] = _conv_silu(src_v, cwv_ref, win_scr.at[3 * j + 1], c, n_seq)
            if src_q is not None:
                stage[j, 3] = _conv_silu(src_q, cwq_ref, win_scr.at[3 * j + 2], c, n_seq,
                                         lambda a: _l2n(a) * (hd ** -0.5))

    def local(stage, with_q, gcol_r, grow_r, c0, count, slot0):
        chunks = []
        for j in range(count):
            c = c0 + j
            r0 = pl.multiple_of(c * CHUNK, CHUNK)
            gcol = pltpu.roll(gcol_r[0, pl.ds(r0, CHUNK), :], head_shift, axis=1)[:, 0:4]
            chunks.append([stage[j, 3] if with_q else None, stage[j, 0], stage[j, 1], stage[j, 2],
                           None, None, gcol, grow_r[0, par, c]])
        for ch in chunks:
            if ch[0] is None:
                ch[4] = _dot(ch[1], ch[2])
            else:
                kq = _dot(jnp.concatenate([ch[1], ch[0]], axis=0), ch[2])
                ch[4], ch[5] = kq[:CHUNK], kq[CHUNK:]
        res = _chunk_local(chunks)
        for j in range(count):
            r0 = pl.multiple_of((c0 + j) * CHUNK, CHUNK)
            for d in (0, 1):
                kmat, nmat, qmat, omat = res[2 * j + d]
                idx = (par * 2 + d) * n_tot + slot0 + j
                kq_scr[idx, 0:hd, :] = kmat.astype(BF16)
                n_scr[idx] = nmat.astype(BF16)
                if qmat is not None:
                    kq_scr[idx, hd:, :] = qmat.astype(BF16)
                    o_scr[par * 2 + d, pl.ds(r0, CHUNK), :] = omat

    u_c = min(LOCAL_UNROLL, n_c)
    u_x = min(LOCAL_UNROLL, n_x)
    g_c, g_x = n_c // u_c, n_x // u_x
    stages = (stage_a, stage_b)

    def prep_x(g, stage):
        prep(q_ref, k_ref, v_ref, g * u_x, u_x, n_x, stage)

    def local_x(g, stage):
        local(stage, True, gcol_ref, grow_ref, g * u_x, u_x, n_c + g * u_x)

    prep(None, kc_ref, vc_ref, 0, u_c, n_c, stages[0])
    for g in range(g_c):
        local(stages[g % 2], False, gcolc_ref, growc_ref, g * u_c, u_c, g * u_c)
        if g + 1 < g_c:
            prep(None, kc_ref, vc_ref, (g + 1) * u_c, u_c, n_c, stages[(g + 1) % 2])
        else:
            prep_x(0, stages[(g + 1) % 2])

    if g_x % 2 == 0:
        def local_pair(p, carry):
            for h in (0, 1):
                g = 2 * p + h
                local_x(g, stages[(g_c + h) % 2])
                prep_x(jnp.minimum(g + 1, g_x - 1), stages[(g_c + h + 1) % 2])
            return carry

        lax.fori_loop(0, g_x // 2, local_pair, 0)
    else:
        for g in range(g_x):
            local_x(g, stages[(g_c + g) % 2])
            if g + 1 < g_x:
                prep_x(g + 1, stages[(g_c + g + 1) % 2])
    @pl.when(par == HEAD_PAIR - 1)
    def _sequential():
        s_ref[...] = jnp.zeros_like(s_ref)
        chains = [(pp, d) for pp in range(HEAD_PAIR) for d in (0, 1)]

        def state_decay(grow, d):
            tot = grow[2 + d:3 + d, CHUNK - 1:CHUNK] if d == 0 else grow[2 + d:3 + d, 0:1]
            return jnp.exp(tot)

        def seq_ctx(i, carry):
            for pp, d in chains:
                c = i if d == 0 else n_c - 1 - i
                idx = (pp * 2 + d) * n_tot + c
                s = s_ref[pp * 2 + d]
                ks = jnp.dot(kq_scr[idx, 0:hd, :], s.astype(BF16), preferred_element_type=F32)
                s_ref[pp * 2 + d] = (s * state_decay(growc_ref[0, pp, c], d) - ks
                                     + n_scr[idx].astype(F32))
            return carry

        def finish(pp, r0, o):
            lanes = slice(pp * hd, (pp + 1) * hd)
            on = o * lax.rsqrt(jnp.mean(o * o, axis=-1, keepdims=True) + NORM_EPS) * onorm_ref[...]
            zg = _silu(z_ref[0, pl.ds(r0, CHUNK), lanes].astype(F32))
            out_ref[0, pl.ds(r0, CHUNK), lanes] = (on * zg).astype(out_ref.dtype)

        def seq_x(i, carry, second_visit):
            for pp, d in chains:
                c = i if d == 0 else n_x - 1 - i
                r0 = pl.multiple_of(c * CHUNK, CHUNK)
                idx = (pp * 2 + d) * n_tot + n_c + c
                s = s_ref[pp * 2 + d]
                kqs = jnp.dot(kq_scr[idx], s.astype(BF16), preferred_element_type=F32)
                o = o_scr[pp * 2 + d, pl.ds(r0, CHUNK), :] + kqs[hd:, :]
                if second_visit:
                    finish(pp, r0, o + o_scr[pp * 2 + 1 - d, pl.ds(r0, CHUNK), :])
                else:
                    o_scr[pp * 2 + d, pl.ds(r0, CHUNK), :] = o
                s_ref[pp * 2 + d] = (s * state_decay(grow_ref[0, pp, c], d) - kqs[0:hd, :]
                                     + n_scr[idx].astype(F32))
            return carry

        lax.fori_loop(0, n_c, seq_ctx, 0)
        half = (n_x + 1) // 2
        lax.fori_loop(0, half, functools.partial(seq_x, second_visit=False), 0)
        if n_x % 2:
            r_mid = (half - 1) * CHUNK
            for pp in range(HEAD_PAIR):
                finish(pp, r_mid, o_scr[pp * 2, r_mid:r_mid + CHUNK, :] + o_scr[pp * 2 + 1, r_mid:r_mid + CHUNK, :])
        lax.fori_loop(half, n_x, functools.partial(seq_x, second_visit=True), 0)


def _row_gates(grow, n_heads):
    b, _, t = grow.shape
    return jnp.transpose(grow.reshape(b, n_heads, 4, t // CHUNK, CHUNK), (0, 1, 3, 2, 4))


def _delta_mixer(qkv, qkv_c, gcol, grow, gcolc, growc, z, conv_qkv, o_norm, n_heads):
    b, t, _ = qkv.shape
    tc = qkv_c.shape[1]
    n_x, n_c = t // CHUNK, tc // CHUNK
    hd = HEAD_DIM
    assert CHUNK == hd and t % CHUNK == 0 and tc % CHUNK == 0
    width = conv_qkv.shape[0]
    grow = _row_gates(grow, n_heads)
    growc = _row_gates(growc, n_heads)
    kern = functools.partial(_delta_kernel, n_x=n_x, n_c=n_c)
    nh = n_heads

    def colblk(off):
        return lambda i, h: (i, 0, off + h)

    def cwblk(off):
        return lambda i, h: (0, off + h)

    assert nh % HEAD_PAIR == 0
    pair5 = lambda i, h: (i, h // HEAD_PAIR, 0, 0, 0)
    pair3 = lambda i, h: (i, 0, h // HEAD_PAIR)
    return pl.pallas_call(
        kern,
        grid=(b, nh),
        in_specs=[pl.BlockSpec((1, t, hd), colblk(0)),
                  pl.BlockSpec((1, t, hd), colblk(nh)),
                  pl.BlockSpec((1, t, hd), colblk(2 * nh)),
                  pl.BlockSpec((1, tc, hd), colblk(nh)),
                  pl.BlockSpec((1, tc, hd), colblk(2 * nh)),
                  pl.BlockSpec((width, hd), cwblk(0)),
                  pl.BlockSpec((width, hd), cwblk(nh)),
                  pl.BlockSpec((width, hd), cwblk(2 * nh)),
                  pl.BlockSpec((1, t, LANES), lambda i, h: (i, 0, 0)),
                  pl.BlockSpec((1, HEAD_PAIR, n_x, 4, CHUNK), pair5),
                  pl.BlockSpec((1, tc, LANES), lambda i, h: (i, 0, 0)),
                  pl.BlockSpec((1, HEAD_PAIR, n_c, 4, CHUNK), pair5),
                  pl.BlockSpec((1, t, HEAD_PAIR * hd), pair3),
                  pl.BlockSpec((1, hd), lambda i, h: (0, 0))],
        out_specs=pl.BlockSpec((1, t, HEAD_PAIR * hd), pair3),
        out_shape=jax.ShapeDtypeStruct((b, t, nh * hd), BF16),
        scratch_shapes=[pltpu.VMEM((2 * HEAD_PAIR * (n_x + n_c), 2 * hd, hd), BF16),
                        pltpu.VMEM((2 * HEAD_PAIR * (n_x + n_c), hd, hd), BF16),
                        pltpu.VMEM((2 * HEAD_PAIR, t, hd), F32),
                        pltpu.VMEM((2 * HEAD_PAIR, hd, hd), F32),
                        pltpu.VMEM((3 * LOCAL_UNROLL, CHUNK + 2 * CONV_HALO, hd), F32),
                        pltpu.VMEM((LOCAL_UNROLL, 4, CHUNK, hd), F32),
                        pltpu.VMEM((LOCAL_UNROLL, 4, CHUNK, hd), F32)],
        compiler_params=_params("parallel", "arbitrary"),
        name="delta_mixer",
    )(qkv, qkv, qkv, qkv_c, qkv_c, conv_qkv, conv_qkv, conv_qkv,
      gcol, grow, gcolc, growc, z, o_norm.reshape(1, hd))


POOL_UNROLL = 4


def _pool_kernel(u_ref, band_ref, icnt_ref, w_ref, sc_ref, out_ref, cs, *, rows, blk):
    g = pl.program_id(1)
    left = lax.shift_left(jnp.int32(1), g)
    right = left - 1
    gw = GRID_W
    gd = cs.shape[1]
    t = rows * gw

    cs[0:gw, :] = jnp.zeros((gw, gd), F32)

    def prefix(r, acc):
        acc = acc + u_ref[0, pl.ds(pl.multiple_of(r * gw, gw), gw), :].astype(F32)
        cs[pl.ds(pl.multiple_of((r + 1) * gw, gw), gw), :] = acc
        return acc

    lax.fori_loop(0, rows, prefix, jnp.zeros((gw, gd), F32))

    def row_mean(r):
        lo = jnp.maximum(r - left, 0)
        hi = jnp.minimum(r + right + 1, rows)
        tot = cs[pl.ds(pl.multiple_of(hi * gw, gw), gw), :] - cs[pl.ds(pl.multiple_of(lo * gw, gw), gw), :]
        return tot / jnp.full((gw, gd), hi - lo, jnp.int32).astype(F32)

    def col_body(i, carry):
        blocks = [i * POOL_UNROLL + j for j in range(POOL_UNROLL)]
        starts = [pl.multiple_of(bi * blk, blk) for bi in blocks]
        rs = [jnp.concatenate([row_mean(bi * (blk // gw) + rr) for rr in range(blk // gw)], axis=0)
              for bi in blocks]
        ms = [jnp.dot(band_ref[0], r.astype(BF16), preferred_element_type=F32) for r in rs]
        ds = [(m * icnt_ref[0] - u_ref[0, pl.ds(r0, blk), :].astype(F32)).astype(BF16)
              for m, r0 in zip(ms, starts)]
        ys = [jnp.dot(dlt, w_ref[0], preferred_element_type=F32) for dlt in ds]
        for y, r0 in zip(ys, starts):
            out_ref[0, pl.ds(r0, blk), :] = (y * sc_ref[0]).astype(out_ref.dtype)
        return carry

    lax.fori_loop(0, t // (blk * POOL_UNROLL), col_body, 0)


def _pool_consts(blk):
    band = np.zeros((len(POOL_WINDOWS), blk, blk), np.float32)
    icnt = np.zeros((len(POOL_WINDOWS), blk, HEAD_DIM), np.float32)
    for gi, w in enumerate(POOL_WINDOWS):
        left = w // 2
        right = w - 1 - left
        for i in range(blk):
            base, col = (i // GRID_W) * GRID_W, i % GRID_W
            lo, hi = max(col - left, 0), min(col + right, GRID_W - 1)
            band[gi, i, base + lo:base + hi + 1] = 1.0
            icnt[gi, i, :] = 1.0 / (hi - lo + 1)
    return jnp.asarray(band, BF16), jnp.asarray(icnt)


def _pool_mixer(u, pool_w, pool_scale):
    b, t, p = u.shape
    n_g, gd = pool_w.shape[0], pool_w.shape[1]
    rows = t // GRID_W
    blk = 4 * GRID_W
    assert t % (blk * POOL_UNROLL) == 0
    band, icnt = _pool_consts(blk)
    kern = functools.partial(_pool_kernel, rows=rows, blk=blk)
    grp = lambda i, g: (g, 0, 0)
    return pl.pallas_call(
        kern,
        grid=(b, n_g),
        in_specs=[pl.BlockSpec((1, t, gd), lambda i, g: (i, 0, g)),
                  pl.BlockSpec((1, blk, blk), grp),
                  pl.BlockSpec((1, blk, gd), grp),
                  pl.BlockSpec((1, gd, gd), grp),
                  pl.BlockSpec((1, 1, gd), grp)],
        out_specs=pl.BlockSpec((1, t, gd), lambda i, g: (i, 0, g)),
        out_shape=jax.ShapeDtypeStruct((b, t, p), BF16),
        scratch_shapes=[pltpu.VMEM((t + GRID_W, gd), F32)],
        compiler_params=_params("parallel", "parallel"),
        name="pool_mixer",
    )(u, band, icnt, pool_w.astype(BF16), pool_scale.reshape(n_g, 1, gd))


def _rms(a, g):
    return a * lax.rsqrt(jnp.mean(a * a, axis=-1, keepdims=True) + NORM_EPS) * g


def _outproj_kernel(dn_ref, pool_ref, x_ref, w1_ref, w2_ref, gt_ref, sc_ref, sh_ref,
                    gpost_ref, gpre_ref, x1_ref, h_ref):
    y = (jnp.dot(dn_ref[0], w1_ref[...], preferred_element_type=F32)
         + jnp.dot(pool_ref[0], w2_ref[...], preferred_element_type=F32))
    x1 = x_ref[0] + gt_ref[0] * _rms(y, gpost_ref[...])
    x1_ref[0] = x1
    h_ref[0] = (_rms(x1, gpre_ref[...]) * (1.0 + sc_ref[0]) + sh_ref[0]).astype(h_ref.dtype)


def _out_projection(dn, pool, x, w_out, gt1, sc2, sh2, g_post, g_pre, tm):
    b, t, d = x.shape
    wd = dn.shape[-1]
    w1 = w_out[:wd].astype(BF16)
    w2 = w_out[wd:].astype(BF16)
    row = lambda i, j: (i, j, 0)
    per_b = lambda i, j: (i, 0, 0)
    const = lambda i, j: (0, 0)
    return pl.pallas_call(
        _outproj_kernel,
        grid=(b, t // tm),
        in_specs=[pl.BlockSpec((1, tm, wd), row),
                  pl.BlockSpec((1, tm, pool.shape[-1]), row),
                  pl.BlockSpec((1, tm, d), row),
                  pl.BlockSpec(w1.shape, const),
                  pl.BlockSpec(w2.shape, const),
                  pl.BlockSpec((1, 1, d), per_b),
                  pl.BlockSpec((1, 1, d), per_b),
                  pl.BlockSpec((1, 1, d), per_b),
                  pl.BlockSpec((1, d), const),
                  pl.BlockSpec((1, d), const)],
        out_specs=[pl.BlockSpec((1, tm, d), row), pl.BlockSpec((1, tm, d), row)],
        out_shape=[jax.ShapeDtypeStruct((b, t, d), F32), jax.ShapeDtypeStruct((b, t, d), BF16)],
        compiler_params=_params("parallel", "parallel"),
        name="out_projection",
    )(dn, pool, x, w1, w2, gt1, sc2, sh2, g_post.reshape(1, d), g_pre.reshape(1, d))


FFN_HALO = 16


def _ffn_kernel(h_ref, hp_ref, hn_ref, x1_ref, wgu_ref, cw_ref, wd_ref, gt_ref, gpost_ref,
                out_ref, hext, act, *, fc, n_fc):
    j = pl.program_id(1)
    nj = pl.num_programs(1)
    tm = h_ref.shape[1]
    hal = FFN_HALO
    dff = fc * n_fc
    hext[0:hal, :] = jnp.where(j > 0, hp_ref[0], jnp.zeros_like(hp_ref[0]))
    hext[hal:hal + tm, :] = h_ref[0]
    hext[hal + tm:, :] = jnp.where(j < nj - 1, hn_ref[0], jnp.zeros_like(hn_ref[0]))
    for c in range(n_fc):
        gate = jnp.dot(hext[...], wgu_ref[:, c * fc:(c + 1) * fc], preferred_element_type=F32)
        up = jnp.dot(h_ref[0], wgu_ref[:, dff + c * fc:dff + (c + 1) * fc], preferred_element_type=F32)
        cw = cw_ref[:, c * fc:(c + 1) * fc]
        conv = (gate[hal - 1:hal - 1 + tm] * cw[0:1] + gate[hal:hal + tm] * cw[1:2]
                + gate[hal + 1:hal + 1 + tm] * cw[2:3])
        act[:, c * fc:(c + 1) * fc] = (_silu(conv) * up).astype(BF16)
    y = jnp.dot(act[...], wd_ref[...], preferred_element_type=F32)
    out_ref[0] = x1_ref[0] + gt_ref[0] * _rms(y, gpost_ref[...])


def _conv_ffn(h, x1, w_up, conv_ffn, w_down, gt2, g_post, tm, fc):
    b, t, d = x1.shape
    dff = w_down.shape[0]
    n_fc = dff // fc
    wgu = w_up.astype(BF16)
    hal = FFN_HALO
    nb = tm // hal
    last = t // hal - 1
    kern = functools.partial(_ffn_kernel, fc=fc, n_fc=n_fc)
    row = lambda i, j: (i, j, 0)
    per_b = lambda i, j: (i, 0, 0)
    const = lambda i, j: (0, 0)
    return pl.pallas_call(
        kern,
        grid=(b, t // tm),
        in_specs=[pl.BlockSpec((1, tm, d), row),
                  pl.BlockSpec((1, hal, d), lambda i, j: (i, jnp.maximum(j * nb - 1, 0), 0)),
                  pl.BlockSpec((1, hal, d), lambda i, j: (i, jnp.minimum((j + 1) * nb, last), 0)),
                  pl.BlockSpec((1, tm, d), row),
                  pl.BlockSpec(wgu.shape, const),
                  pl.BlockSpec(conv_ffn.shape, const),
                  pl.BlockSpec(w_down.shape, const),
                  pl.BlockSpec((1, 1, d), per_b),
                  pl.BlockSpec((1, d), const)],
        out_specs=pl.BlockSpec((1, tm, d), row),
        out_shape=jax.ShapeDtypeStruct((b, t, d), F32),
        scratch_shapes=[pltpu.VMEM((tm + 2 * hal, d), BF16), pltpu.VMEM((tm, dff), BF16)],
        compiler_params=_params("parallel", "parallel"),
        name="conv_ffn",
    )(h, h, h, x1, wgu, conv_ffn, w_down.astype(BF16), gt2, g_post.reshape(1, d))


def _layer(x, c, ctx, c_ctx, w_mod, b_mod, g_pre_mix, g_post_mix, g_pre_ffn, g_post_ffn,
           w_in, conv_qkv, a_log, dt_bias, o_norm, pool_w, pool_scale, w_out, w_up,
           conv_ffn, w_down):
    b, t, d = x.shape
    n_heads = a_log.shape[-1]
    w = n_heads * HEAD_DIM
    n_gate = 4 * n_heads

    rows = 16
    c_all = jnp.zeros((rows, d), F32).at[:b].set(c).at[b].set(c_ctx)
    m = _modulation(c_all, w_mod, b_mod)
    sh1, sc1, gt1, sh2, sc2, gt2 = [m[:b, None, i * d:(i + 1) * d] for i in range(N_MOD)]
    csh1 = jnp.broadcast_to(m[b, 0 * d:1 * d], (b, 1, d))
    csc1 = jnp.broadcast_to(m[b, 1 * d:2 * d], (b, 1, d))

    off_g = 4 * w
    w_gate = jnp.transpose(w_in[:, off_g:off_g + n_gate].reshape(d, 4, n_heads), (0, 2, 1)).reshape(d, n_gate)
    w_all = jnp.concatenate([w_in[:, :off_g], w_in[:, off_g + n_gate:],
                             jnp.pad(w_gate, ((0, 0), (0, LANES - n_gate)))], axis=1).astype(BF16)
    tm = min(512, t)
    qkv, z, pool_u, gcol, grow = _in_projection(x, sc1, sh1, g_pre_mix, w_all, a_log, dt_bias, n_heads, tm)
    qkv_c, _, _, gcolc, growc = _in_projection(ctx, csc1, csh1, g_pre_mix, w_all, a_log, dt_bias,
                                                n_heads, ctx.shape[1])
    dn = _delta_mixer(qkv, qkv_c, gcol, grow, gcolc, growc, z, conv_qkv, o_norm, n_heads)
    pool = _pool_mixer(pool_u, pool_w, pool_scale)
    x1, h2 = _out_projection(dn, pool, x, w_out, gt1, sc2, sh2, g_post_mix, g_pre_ffn, tm)
    return _conv_ffn(h2, x1, w_up, conv_ffn, w_down, gt2, g_post_ffn, tm, 256)


def kernel(x, c, ctx, c_ctx, w_mod, b_mod, g_pre_mix, g_post_mix, g_pre_ffn, g_post_ffn, w_in,
           conv_qkv, a_log, dt_bias, o_norm, pool_w, pool_scale, w_out, w_up, conv_ffn, w_down):
    depth = w_mod.shape[0]
    assert depth == 1, "context-stream update between layers is not implemented"
    return _layer(x, c, ctx, c_ctx, w_mod[0], b_mod[0], g_pre_mix[0], g_post_mix[0],
                  g_pre_ffn[0], g_post_ffn[0], w_in[0], conv_qkv[0], a_log[0], dt_bias[0],
                  o_norm[0], pool_w[0], pool_scale[0], w_out[0], w_up[0], conv_ffn[0], w_down[0])
```

```python
import functools

import numpy as np
import jax
import jax.numpy as jnp
from jax import lax
from jax.experimental import pallas as pl
from jax.experimental.pallas import tpu as pltpu

NORM_EPS = 1e-6
GRID_W = 64
POOL_WINDOWS = (2, 4, 8, 16)
N_MOD = 6
CHUNK = 128
HEAD_DIM = 128
VMEM_LIMIT = 56 * 1024 * 1024

F32 = jnp.float32
BF16 = jnp.bfloat16
HIGHEST = lax.Precision.HIGHEST


def _silu(a):
    return a * (1.0 / (1.0 + jnp.exp(-a)))


def _dot(a, b):
    return jnp.dot(a.astype(BF16), b.astype(BF16), preferred_element_type=F32)


def _bdot(a, b):
    return jnp.dot(a, b, preferred_element_type=F32).astype(BF16)


def _dot_hi(a, b):
    return jnp.dot(a, b, preferred_element_type=F32, precision=HIGHEST)


def _params(*sem):
    return pltpu.CompilerParams(dimension_semantics=sem, vmem_limit_bytes=VMEM_LIMIT)


def _mod_kernel(c_ref, w_ref, b_ref, o_ref):
    o_ref[...] = _dot_hi(_silu(c_ref[...]), w_ref[...]) + b_ref[...]


def _modulation(c_all, w_mod, b_mod):
    r, d = c_all.shape
    n = w_mod.shape[1]
    tn = 512
    return pl.pallas_call(
        _mod_kernel,
        grid=(n // tn,),
        in_specs=[pl.BlockSpec((r, d), lambda j: (0, 0)),
                  pl.BlockSpec((d, tn), lambda j: (0, j)),
                  pl.BlockSpec((1, tn), lambda j: (0, j))],
        out_specs=pl.BlockSpec((r, tn), lambda j: (0, j)),
        out_shape=jax.ShapeDtypeStruct((r, n), F32),
        compiler_params=_params("parallel"),
        name="modulation",
    )(c_all, w_mod, b_mod.reshape(1, n))


LANES = 128


def _split3(a):
    hi = a.astype(BF16)
    r1 = a - hi.astype(F32)
    mid = r1.astype(BF16)
    lo = (r1 - mid.astype(F32)).astype(BF16)
    return hi, mid, lo


def _inproj_kernel(x_ref, sc_ref, sh_ref, g_ref, w_ref, alog_ref, dtb_ref, tri_ref,
                   qkv_ref, z_ref, pool_ref, gcol_ref, grow_ref, *, n_qkv, n_z, n_pool, n_heads):
    x = x_ref[0]
    xn = x * lax.rsqrt(jnp.mean(x * x, axis=-1, keepdims=True) + NORM_EPS) * g_ref[...]
    h = (xn * (1.0 + sc_ref[0]) + sh_ref[0]).astype(BF16)
    o1 = n_qkv
    o2 = o1 + n_z
    o3 = o2 + n_pool
    pg = jnp.dot(h, w_ref[:, o3:], preferred_element_type=F32)
    qkv_ref[0] = jnp.dot(h, w_ref[:, 0:o1], preferred_element_type=F32).astype(qkv_ref.dtype)
    beta = 1.0 / (1.0 + jnp.exp(-pg))
    a = pg + dtb_ref[...]
    softplus = jnp.maximum(a, 0.0) + jnp.log(1.0 + jnp.exp(-jnp.abs(a)))
    g = -jnp.exp(alog_ref[...]) * softplus
    tm = g.shape[0]
    n_ch = tm // CHUNK
    rows = 4 * n_heads
    beta_t = [beta[c * CHUNK:(c + 1) * CHUNK, :].T[0:rows, :] for c in range(n_ch)]
    g_t = [g[c * CHUNK:(c + 1) * CHUNK, :].T[0:rows, :] for c in range(n_ch)]
    parts = _split3(jnp.concatenate(g_t, axis=0))
    r = jnp.dot(jnp.concatenate(parts, axis=0), tri_ref[...], preferred_element_type=F32)
    z_ref[0] = jnp.dot(h, w_ref[:, o1:o2], preferred_element_type=F32).astype(z_ref.dtype)
    pool_ref[0] = jnp.dot(h, w_ref[:, o2:o3], preferred_element_type=F32).astype(pool_ref.dtype)
    m = n_ch * rows
    pre_all = r[0:m] + r[m:2 * m] + r[2 * m:3 * m]
    kind = lax.broadcasted_iota(jnp.int32, (rows, CHUNK), 0) & 3
    zpad = jnp.zeros((LANES - rows, CHUNK), F32)
    for c in range(n_ch):
        pre = pre_all[c * rows:(c + 1) * rows, :]
        suf = pre[:, CHUNK - 1:CHUNK] - pre + g_t[c]
        row = jnp.where(kind < 2, beta_t[c], jnp.where(kind == 2, pre, suf))
        grow_ref[0, :, c * CHUNK:(c + 1) * CHUNK] = row
        gcol_ref[0, c * CHUNK:(c + 1) * CHUNK, :] = jnp.concatenate([row, zpad], axis=0).T


def _in_projection(x, sc, sh, g_pre, w_all, a_log, dt_bias, n_heads, tm):
    b, t, d = x.shape
    w = n_heads * HEAD_DIM
    n_qkv, n_z = 3 * w, w
    n_pool = w_all.shape[1] - n_qkv - n_z - LANES
    assert CHUNK == LANES
    tri = jnp.asarray(np.triu(np.ones((CHUNK, CHUNK), np.float32)), BF16)
    lane_par = jnp.zeros((2, n_heads, 4), F32)
    lane_par = lane_par.at[0, :, 2:].set(a_log.T).at[1, :, 2:].set(dt_bias.T)
    lane_par = jnp.pad(lane_par.reshape(2, 4 * n_heads), ((0, 0), (0, LANES - 4 * n_heads)))
    kern = functools.partial(_inproj_kernel, n_qkv=n_qkv, n_z=n_z, n_pool=n_pool, n_heads=n_heads)
    row = lambda i, j: (i, j, 0)
    per_b = lambda i, j: (i, 0, 0)
    const = lambda i, j: (0, 0)
    return pl.pallas_call(
        kern,
        grid=(b, t // tm),
        in_specs=[pl.BlockSpec((1, tm, d), row),
                  pl.BlockSpec((1, 1, d), per_b),
                  pl.BlockSpec((1, 1, d), per_b),
                  pl.BlockSpec((1, d), const),
                  pl.BlockSpec(w_all.shape, const),
                  pl.BlockSpec((1, LANES), const),
                  pl.BlockSpec((1, LANES), const),
                  pl.BlockSpec((CHUNK, CHUNK), const)],
        out_specs=[pl.BlockSpec((1, tm, n_qkv), row),
                   pl.BlockSpec((1, tm, n_z), row),
                   pl.BlockSpec((1, tm, n_pool), row),
                   pl.BlockSpec((1, tm, LANES), row),
                   pl.BlockSpec((1, 4 * n_heads, tm), lambda i, j: (i, 0, j))],
        out_shape=[jax.ShapeDtypeStruct((b, t, n_qkv), BF16),
                   jax.ShapeDtypeStruct((b, t, n_z), BF16),
                   jax.ShapeDtypeStruct((b, t, n_pool), BF16),
                   jax.ShapeDtypeStruct((b, t, LANES), F32),
                   jax.ShapeDtypeStruct((b, 4 * n_heads, t), F32)],
        compiler_params=_params("parallel", "parallel"),
        name="in_projection",
    )(x, sc, sh, g_pre.reshape(1, d), w_all, lane_par[0:1], lane_par[1:2], tri)


CONV_HALO = 16


SUBLANES = 8


def _conv_silu(ref, cw_ref, win, c, n_chunks, post=None):
    hal = CONV_HALO
    r0 = pl.multiple_of(c * CHUNK, CHUNK)
    p0 = pl.multiple_of(jnp.maximum(r0 - hal, 0), hal)
    n0 = pl.multiple_of(jnp.minimum(r0 + CHUNK, n_chunks * CHUNK - hal), hal)
    win[0:hal, :] = jnp.where(c > 0, ref[0, pl.ds(p0, hal), :].astype(F32), 0.0)
    win[hal:hal + CHUNK, :] = ref[0, pl.ds(r0, CHUNK), :].astype(F32)
    win[hal + CHUNK:, :] = jnp.where(c < n_chunks - 1, ref[0, pl.ds(n0, hal), :].astype(F32), 0.0)
    width = cw_ref.shape[0]
    pad = width // 2
    n_t = CHUNK // SUBLANES
    taps = [cw_ref[j:j + 1, :] for j in range(width)]
    xs = [win[pl.ds(hal - pad + s, n_t, stride=SUBLANES), :] for s in range(SUBLANES + width - 1)]
    outs = []
    for r in range(SUBLANES):
        acc = xs[r] * taps[0]
        for j in range(1, width):
            acc = acc + xs[r + j] * taps[j]
        acc = _silu(acc)
        outs.append(acc if post is None else post(acc))
    for r in range(SUBLANES):
        win[pl.ds(hal + r, n_t, stride=SUBLANES), :] = outs[r]
    return win[hal:hal + CHUNK, :]


def _l2n(a):
    return a * lax.rsqrt(jnp.sum(a * a, axis=-1, keepdims=True) + NORM_EPS)


INV_BASE = 32
LOCAL_UNROLL = 4
HEAD_PAIR = 2


def _inv_unit(mats):
    n = mats[0].shape[0]
    ii = lax.broadcasted_iota(jnp.int32, (n, n), 0)
    jj = lax.broadcasted_iota(jnp.int32, (n, n), 1)

    def same_block(bits):
        return lax.shift_right_logical(ii, bits) == lax.shift_right_logical(jj, bits)

    bits = int(np.log2(INV_BASE))
    base = same_block(bits)
    eye = jnp.where(ii == jj, 1.0, 0.0).astype(BF16)
    zero = jnp.zeros((), BF16)
    abf = [a.astype(BF16) for a in mats]
    ms = [jnp.where(base, -a, zero) for a in abf]
    ps = [eye + m for m in ms]
    ms = [_bdot(m, m) for m in ms]
    for _ in range(bits - 2):
        pm = [_bdot(jnp.concatenate([p, m], axis=0), m) for p, m in zip(ps, ms)]
        ps = [p + x[:n] for p, x in zip(ps, pm)]
        ms = [x[n:] for x in pm]
    ps = [p + _bdot(p, m) for p, m in zip(ps, ms)]
    while (1 << bits) < n:
        lvl = same_block(bits + 1) & jnp.logical_not(same_block(bits))
        tmp = [_bdot(jnp.where(lvl, a, zero), p) for a, p in zip(abf, ps)]
        ps = [p - _bdot(p, t) for p, t in zip(ps, tmp)]
        bits += 1
    return ps


def _chunk_local(chunks):
    c = chunks[0][1].shape[0]
    ii = lax.broadcasted_iota(jnp.int32, (c, c), 0)
    jj = lax.broadcasted_iota(jnp.int32, (c, c), 1)
    mats, rhs, ktd, attn, qe = [], [], [], [], []
    for q, k, kt, v, kk, qk, gcol, grow in chunks:
        for d in (0, 1):
            incl = (ii >= jj) if d == 0 else (ii <= jj)
            strict = (ii > jj) if d == 0 else (ii < jj)
            beta_c = gcol[:, d:d + 1]
            gc_c = gcol[:, 2 + d:3 + d]
            gc_r = grow[2 + d:3 + d, :]
            glast = gc_r[:, c - 1:c] if d == 0 else gc_r[:, 0:1]
            decay = jnp.where(incl, jnp.exp(jnp.where(incl, gc_c - gc_r, 0.0)), 0.0)
            e_c = jnp.exp(gc_c)
            mats.append(jnp.where(strict, kk * decay * beta_c, 0.0))
            rhs.append(jnp.concatenate([k * (beta_c * e_c), v * beta_c], axis=1).astype(BF16))
            ktd.append((kt * jnp.exp(glast - gc_r)).astype(BF16))
            attn.append(None if q is None else jnp.where(incl, qk * decay, 0.0).astype(BF16))
            qe.append(None if q is None else q * e_c)
    ts = _inv_unit(mats)
    wus = [_dot(t, r) for t, r in zip(ts, rhs)]
    hd = HEAD_DIM
    out = []
    for kd, a, wu, qe_ in zip(ktd, attn, wus, qe):
        if a is None:
            kn_ = _dot(kd, wu)
            out.append((kn_[:, :hd], kn_[:, hd:], None, None))
        else:
            x = _dot(jnp.concatenate([kd, a], axis=0), wu)
            out.append((x[:hd, :hd], x[:hd, hd:], qe_ - x[hd:, :hd], x[hd:, hd:]))
    return out


def _delta_kernel(q_ref, k_ref, v_ref, kc_ref, vc_ref, kc2_ref, vc2_ref, cwq_ref, cwk_ref, cwv_ref,
                  cwk2_ref, cwv2_ref,
                  gcol_ref, grow_ref, gcolc_ref, growc_ref, z_ref, onorm_ref,
                  out_ref, kq_scr, n_scr, o_scr, s_ref, win_scr, stage_a, stage_b, *, n_x, n_c):
    n_tot = n_x + n_c
    hd = HEAD_DIM
    head_shift = lax.rem(LANES - 4 * pl.program_id(1), LANES)
    par = lax.rem(pl.program_id(1), HEAD_PAIR)

    def prep(src_q, src_k, src_v, c0, count, n_seq, stage, slot0=0, cwk=cwk_ref, cwv=cwv_ref):
        for j in range(count):
            c = c0 + j
            sl = slot0 + j
            k = _conv_silu(src_k, cwk, win_scr.at[3 * sl], c, n_seq, _l2n)
            stage[sl, 0] = k
            stage[sl, 1] = k.T
            stage[sl, 2] = _conv_silu(src_v, cwv, win_scr.at[3 * sl + 1], c, n_seq)
            if src_q is not None:
                stage[sl, 3] = _conv_silu(src_q, cwq_ref, win_scr.at[3 * sl + 2], c, n_seq,
                                          lambda a: _l2n(a) * (hd ** -0.5))

    def local(stage, with_q, gcol_r, grow_r, items):
        chunks = []
        for sl, c, pp, shift, _ in items:
            r0 = pl.multiple_of(c * CHUNK, CHUNK)
            gcol = pltpu.roll(gcol_r[0, pl.ds(r0, CHUNK), :], shift, axis=1)[:, 0:4]
            chunks.append([stage[sl, 3] if with_q else None, stage[sl, 0], stage[sl, 1], stage[sl, 2],
                           None, None, gcol, grow_r[0, pp, c]])
        for ch in chunks:
            if ch[0] is None:
                ch[4] = _dot(ch[1], ch[2])
            else:
                kq = _dot(jnp.concatenate([ch[1], ch[0]], axis=0), ch[2])
                ch[4], ch[5] = kq[:CHUNK], kq[CHUNK:]
        res = _chunk_local(chunks)
        for j, (_, c, pp, _, out_slot) in enumerate(items):
            r0 = pl.multiple_of(c * CHUNK, CHUNK)
            for d in (0, 1):
                kmat, nmat, qmat, omat = res[2 * j + d]
                idx = (pp * 2 + d) * n_tot + out_slot
                kq_scr[idx, 0:hd, :] = kmat.astype(BF16)
                n_scr[idx] = nmat.astype(BF16)
                if qmat is not None:
                    kq_scr[idx, hd:, :] = qmat.astype(BF16)
                    o_scr[pp * 2 + d, pl.ds(r0, CHUNK), :] = omat

    u_c = min(LOCAL_UNROLL, n_c)
    u_x = min(LOCAL_UNROLL, n_x)
    g_c, g_x = n_c // u_c, n_x // u_x
    stages = (stage_a, stage_b)

    def prep_x(g, stage):
        prep(q_ref, k_ref, v_ref, g * u_x, u_x, n_x, stage)

    def local_x(g, stage):
        local(stage, True, gcol_ref, grow_ref,
              [(j, g * u_x + j, par, head_shift, n_c + g * u_x + j) for j in range(u_x)])

    if g_c == 1 and HEAD_PAIR * n_c <= LOCAL_UNROLL:
        @pl.when(par == 0)
        def _context_pair():
            items = []
            for pp in range(HEAD_PAIR):
                src = (kc_ref, vc_ref, cwk_ref, cwv_ref) if pp == 0 else (kc2_ref, vc2_ref, cwk2_ref, cwv2_ref)
                prep(None, src[0], src[1], 0, n_c, n_c, stages[0], pp * n_c, src[2], src[3])
                shift = lax.rem(LANES - 4 * (pl.program_id(1) + pp), LANES)
                items += [(pp * n_c + j, j, pp, shift, j) for j in range(n_c)]
            local(stages[0], False, gcolc_ref, growc_ref, items)
            prep_x(0, stages[1])

        @pl.when(par != 0)
        def _latent_only():
            prep_x(0, stages[1])
    else:
        prep(None, kc_ref, vc_ref, 0, u_c, n_c, stages[0])
        for g in range(g_c):
            local(stages[g % 2], False, gcolc_ref, growc_ref,
                  [(j, g * u_c + j, par, head_shift, g * u_c + j) for j in range(u_c)])
            if g + 1 < g_c:
                prep(None, kc_ref, vc_ref, (g + 1) * u_c, u_c, n_c, stages[(g + 1) % 2])
            else:
                prep_x(0, stages[(g + 1) % 2])

    if g_x % 2 == 0:
        def local_pair(p, carry):
            for h in (0, 1):
                g = 2 * p + h
                prep_x(jnp.minimum(g + 1, g_x - 1), stages[(g_c + h + 1) % 2])
                local_x(g, stages[(g_c + h) % 2])
            return carry

        lax.fori_loop(0, g_x // 2, local_pair, 0)
    else:
        for g in range(g_x):
            local_x(g, stages[(g_c + g) % 2])
            if g + 1 < g_x:
                prep_x(g + 1, stages[(g_c + g + 1) % 2])
    @pl.when(par == HEAD_PAIR - 1)
    def _sequential():
        s_ref[...] = jnp.zeros_like(s_ref)
        chains = [(pp, d) for pp in range(HEAD_PAIR) for d in (0, 1)]

        def state_decay(grow, d):
            tot = grow[2 + d:3 + d, CHUNK - 1:CHUNK] if d == 0 else grow[2 + d:3 + d, 0:1]
            return jnp.exp(tot)

        def seq_ctx(i, carry):
            for pp, d in chains:
                c = i if d == 0 else n_c - 1 - i
                idx = (pp * 2 + d) * n_tot + c
                s = s_ref[pp * 2 + d]
                ks = jnp.dot(kq_scr[idx, 0:hd, :], s.astype(BF16), preferred_element_type=F32)
                s_ref[pp * 2 + d] = (s * state_decay(growc_ref[0, pp, c], d) - ks
                                     + n_scr[idx].astype(F32))
            return carry

        def finish(pp, r0, o):
            lanes = slice(pp * hd, (pp + 1) * hd)
            on = o * lax.rsqrt(jnp.mean(o * o, axis=-1, keepdims=True) + NORM_EPS) * onorm_ref[...]
            zg = _silu(z_ref[0, pl.ds(r0, CHUNK), lanes].astype(F32))
            out_ref[0, pl.ds(r0, CHUNK), lanes] = (on * zg).astype(out_ref.dtype)

        def seq_x(i, carry, second_visit):
            for pp, d in chains:
                c = i if d == 0 else n_x - 1 - i
                r0 = pl.multiple_of(c * CHUNK, CHUNK)
                idx = (pp * 2 + d) * n_tot + n_c + c
                s = s_ref[pp * 2 + d]
                kqs = jnp.dot(kq_scr[idx], s.astype(BF16), preferred_element_type=F32)
                o = o_scr[pp * 2 + d, pl.ds(r0, CHUNK), :] + kqs[hd:, :]
                if second_visit:
                    finish(pp, r0, o + o_scr[pp * 2 + 1 - d, pl.ds(r0, CHUNK), :])
                else:
                    o_scr[pp * 2 + d, pl.ds(r0, CHUNK), :] = o
                s_ref[pp * 2 + d] = (s * state_decay(grow_ref[0, pp, c], d) - kqs[0:hd, :]
                                     + n_scr[idx].astype(F32))
            return carry

        lax.fori_loop(0, n_c, seq_ctx, 0)
        half = (n_x + 1) // 2
        lax.fori_loop(0, half, functools.partial(seq_x, second_visit=False), 0)
        if n_x % 2:
            r_mid = (half - 1) * CHUNK
            for pp in range(HEAD_PAIR):
                finish(pp, r_mid, o_scr[pp * 2, r_mid:r_mid + CHUNK, :] + o_scr[pp * 2 + 1, r_mid:r_mid + CHUNK, :])
        lax.fori_loop(half, n_x, functools.partial(seq_x, second_visit=True), 0)


def _row_gates(grow, n_heads):
    b, _, t = grow.shape
    return jnp.transpose(grow.reshape(b, n_heads, 4, t // CHUNK, CHUNK), (0, 1, 3, 2, 4))


def _delta_mixer(qkv, qkv_c, gcol, grow, gcolc, growc, z, conv_qkv, o_norm, n_heads):
    b, t, _ = qkv.shape
    tc = qkv_c.shape[1]
    n_x, n_c = t // CHUNK, tc // CHUNK
    hd = HEAD_DIM
    assert CHUNK == hd and t % CHUNK == 0 and tc % CHUNK == 0
    width = conv_qkv.shape[0]
    grow = _row_gates(grow, n_heads)
    growc = _row_gates(growc, n_heads)
    kern = functools.partial(_delta_kernel, n_x=n_x, n_c=n_c)
    nh = n_heads

    def colblk(off, ahead=0):
        return lambda i, h: (i, 0, off + jnp.minimum(h + ahead, nh - 1))

    def cwblk(off, ahead=0):
        return lambda i, h: (0, off + jnp.minimum(h + ahead, nh - 1))

    assert nh % HEAD_PAIR == 0
    pair5 = lambda i, h: (i, h // HEAD_PAIR, 0, 0, 0)
    pair3 = lambda i, h: (i, 0, h // HEAD_PAIR)
    return pl.pallas_call(
        kern,
        grid=(b, nh),
        in_specs=[pl.BlockSpec((1, t, hd), colblk(0)),
                  pl.BlockSpec((1, t, hd), colblk(nh)),
                  pl.BlockSpec((1, t, hd), colblk(2 * nh)),
                  pl.BlockSpec((1, tc, hd), colblk(nh)),
                  pl.BlockSpec((1, tc, hd), colblk(2 * nh)),
                  pl.BlockSpec((1, tc, hd), colblk(nh, 1)),
                  pl.BlockSpec((1, tc, hd), colblk(2 * nh, 1)),
                  pl.BlockSpec((width, hd), cwblk(0)),
                  pl.BlockSpec((width, hd), cwblk(nh)),
                  pl.BlockSpec((width, hd), cwblk(2 * nh)),
                  pl.BlockSpec((width, hd), cwblk(nh, 1)),
                  pl.BlockSpec((width, hd), cwblk(2 * nh, 1)),
                  pl.BlockSpec((1, t, LANES), lambda i, h: (i, 0, 0)),
                  pl.BlockSpec((1, HEAD_PAIR, n_x, 4, CHUNK), pair5),
                  pl.BlockSpec((1, tc, LANES), lambda i, h: (i, 0, 0)),
                  pl.BlockSpec((1, HEAD_PAIR, n_c, 4, CHUNK), pair5),
                  pl.BlockSpec((1, t, HEAD_PAIR * hd), pair3),
                  pl.BlockSpec((1, hd), lambda i, h: (0, 0))],
        out_specs=pl.BlockSpec((1, t, HEAD_PAIR * hd), pair3),
        out_shape=jax.ShapeDtypeStruct((b, t, nh * hd), BF16),
        scratch_shapes=[pltpu.VMEM((2 * HEAD_PAIR * (n_x + n_c), 2 * hd, hd), BF16),
                        pltpu.VMEM((2 * HEAD_PAIR * (n_x + n_c), hd, hd), BF16),
                        pltpu.VMEM((2 * HEAD_PAIR, t, hd), F32),
                        pltpu.VMEM((2 * HEAD_PAIR, hd, hd), F32),
                        pltpu.VMEM((3 * LOCAL_UNROLL, CHUNK + 2 * CONV_HALO, hd), F32),
                        pltpu.VMEM((LOCAL_UNROLL, 4, CHUNK, hd), F32),
                        pltpu.VMEM((LOCAL_UNROLL, 4, CHUNK, hd), F32)],
        compiler_params=_params("parallel", "arbitrary"),
        name="delta_mixer",
    )(qkv, qkv, qkv, qkv_c, qkv_c, qkv_c, qkv_c, conv_qkv, conv_qkv, conv_qkv, conv_qkv, conv_qkv,
      gcol, grow, gcolc, growc, z, o_norm.reshape(1, hd))


POOL_UNROLL = 4


def _pool_kernel(u_ref, band_ref, icnt_ref, w_ref, sc_ref, out_ref, cs, *, rows, blk):
    g = pl.program_id(1)
    left = lax.shift_left(jnp.int32(1), g)
    right = left - 1
    gw = GRID_W
    gd = cs.shape[1]
    t = rows * gw

    cs[0:gw, :] = jnp.zeros((gw, gd), F32)

    def prefix(r, acc):
        acc = acc + u_ref[0, pl.ds(pl.multiple_of(r * gw, gw), gw), :].astype(F32)
        cs[pl.ds(pl.multiple_of((r + 1) * gw, gw), gw), :] = acc
        return acc

    lax.fori_loop(0, rows, prefix, jnp.zeros((gw, gd), F32))

    def row_mean(r):
        lo = jnp.maximum(r - left, 0)
        hi = jnp.minimum(r + right + 1, rows)
        tot = cs[pl.ds(pl.multiple_of(hi * gw, gw), gw), :] - cs[pl.ds(pl.multiple_of(lo * gw, gw), gw), :]
        return tot / jnp.full((gw, gd), hi - lo, jnp.int32).astype(F32)

    def col_body(i, carry):
        blocks = [i * POOL_UNROLL + j for j in range(POOL_UNROLL)]
        starts = [pl.multiple_of(bi * blk, blk) for bi in blocks]
        rs = [jnp.concatenate([row_mean(bi * (blk // gw) + rr) for rr in range(blk // gw)], axis=0)
              for bi in blocks]
        ms = [jnp.dot(band_ref[0], r.astype(BF16), preferred_element_type=F32) for r in rs]
        ds = [(m * icnt_ref[0] - u_ref[0, pl.ds(r0, blk), :].astype(F32)).astype(BF16)
              for m, r0 in zip(ms, starts)]
        ys = [jnp.dot(dlt, w_ref[0], preferred_element_type=F32) for dlt in ds]
        for y, r0 in zip(ys, starts):
            out_ref[0, pl.ds(r0, blk), :] = (y * sc_ref[0]).astype(out_ref.dtype)
        return carry

    lax.fori_loop(0, t // (blk * POOL_UNROLL), col_body, 0)


def _pool_consts(blk):
    band = np.zeros((len(POOL_WINDOWS), blk, blk), np.float32)
    icnt = np.zeros((len(POOL_WINDOWS), blk, HEAD_DIM), np.float32)
    for gi, w in enumerate(POOL_WINDOWS):
        left = w // 2
        right = w - 1 - left
        for i in range(blk):
            base, col = (i // GRID_W) * GRID_W, i % GRID_W
            lo, hi = max(col - left, 0), min(col + right, GRID_W - 1)
            band[gi, i, base + lo:base + hi + 1] = 1.0
            icnt[gi, i, :] = 1.0 / (hi - lo + 1)
    return jnp.asarray(band, BF16), jnp.asarray(icnt)


def _pool_mixer(u, pool_w, pool_scale):
    b, t, p = u.shape
    n_g, gd = pool_w.shape[0], pool_w.shape[1]
    rows = t // GRID_W
    blk = 4 * GRID_W
    assert t % (blk * POOL_UNROLL) == 0
    band, icnt = _pool_consts(blk)
    kern = functools.partial(_pool_kernel, rows=rows, blk=blk)
    grp = lambda i, g: (g, 0, 0)
    return pl.pallas_call(
        kern,
        grid=(b, n_g),
        in_specs=[pl.BlockSpec((1, t, gd), lambda i, g: (i, 0, g)),
                  pl.BlockSpec((1, blk, blk), grp),
                  pl.BlockSpec((1, blk, gd), grp),
                  pl.BlockSpec((1, gd, gd), grp),
                  pl.BlockSpec((1, 1, gd), grp)],
        out_specs=pl.BlockSpec((1, t, gd), lambda i, g: (i, 0, g)),
        out_shape=jax.ShapeDtypeStruct((b, t, p), BF16),
        scratch_shapes=[pltpu.VMEM((t + GRID_W, gd), F32)],
        compiler_params=_params("parallel", "parallel"),
        name="pool_mixer",
    )(u, band, icnt, pool_w.astype(BF16), pool_scale.reshape(n_g, 1, gd))


def _rms(a, g):
    return a * lax.rsqrt(jnp.mean(a * a, axis=-1, keepdims=True) + NORM_EPS) * g


def _outproj_kernel(dn_ref, pool_ref, x_ref, w1_ref, w2_ref, gt_ref, sc_ref, sh_ref,
                    gpost_ref, gpre_ref, x1_ref, h_ref):
    y = (jnp.dot(dn_ref[0], w1_ref[...], preferred_element_type=F32)
         + jnp.dot(pool_ref[0], w2_ref[...], preferred_element_type=F32))
    x1 = x_ref[0] + gt_ref[0] * _rms(y, gpost_ref[...])
    x1_ref[0] = x1
    h_ref[0] = (_rms(x1, gpre_ref[...]) * (1.0 + sc_ref[0]) + sh_ref[0]).astype(h_ref.dtype)


def _out_projection(dn, pool, x, w_out, gt1, sc2, sh2, g_post, g_pre, tm):
    b, t, d = x.shape
    wd = dn.shape[-1]
    w1 = w_out[:wd].astype(BF16)
    w2 = w_out[wd:].astype(BF16)
    row = lambda i, j: (i, j, 0)
    per_b = lambda i, j: (i, 0, 0)
    const = lambda i, j: (0, 0)
    return pl.pallas_call(
        _outproj_kernel,
        grid=(b, t // tm),
        in_specs=[pl.BlockSpec((1, tm, wd), row),
                  pl.BlockSpec((1, tm, pool.shape[-1]), row),
                  pl.BlockSpec((1, tm, d), row),
                  pl.BlockSpec(w1.shape, const),
                  pl.BlockSpec(w2.shape, const),
                  pl.BlockSpec((1, 1, d), per_b),
                  pl.BlockSpec((1, 1, d), per_b),
                  pl.BlockSpec((1, 1, d), per_b),
                  pl.BlockSpec((1, d), const),
                  pl.BlockSpec((1, d), const)],
        out_specs=[pl.BlockSpec((1, tm, d), row), pl.BlockSpec((1, tm, d), row)],
        out_shape=[jax.ShapeDtypeStruct((b, t, d), F32), jax.ShapeDtypeStruct((b, t, d), BF16)],
        compiler_params=_params("parallel", "parallel"),
        name="out_projection",
    )(dn, pool, x, w1, w2, gt1, sc2, sh2, g_post.reshape(1, d), g_pre.reshape(1, d))


FFN_HALO = 16


def _ffn_kernel(h_ref, hp_ref, hn_ref, x1_ref, wgu_ref, cw_ref, wd_ref, gt_ref, gpost_ref,
                out_ref, hext, act, *, fc, n_fc):
    j = pl.program_id(1)
    nj = pl.num_programs(1)
    tm = h_ref.shape[1]
    hal = FFN_HALO
    dff = fc * n_fc
    hext[0:hal, :] = jnp.where(j > 0, hp_ref[0], jnp.zeros_like(hp_ref[0]))
    hext[hal:hal + tm, :] = h_ref[0]
    hext[hal + tm:, :] = jnp.where(j < nj - 1, hn_ref[0], jnp.zeros_like(hn_ref[0]))
    for c in range(n_fc):
        gate = jnp.dot(hext[...], wgu_ref[:, c * fc:(c + 1) * fc], preferred_element_type=F32)
        up = jnp.dot(h_ref[0], wgu_ref[:, dff + c * fc:dff + (c + 1) * fc], preferred_element_type=F32)
        cw = cw_ref[:, c * fc:(c + 1) * fc]
        conv = (gate[hal - 1:hal - 1 + tm] * cw[0:1] + gate[hal:hal + tm] * cw[1:2]
                + gate[hal + 1:hal + 1 + tm] * cw[2:3])
        act[:, c * fc:(c + 1) * fc] = (_silu(conv) * up).astype(BF16)
    y = jnp.dot(act[...], wd_ref[...], preferred_element_type=F32)
    out_ref[0] = x1_ref[0] + gt_ref[0] * _rms(y, gpost_ref[...])


def _conv_ffn(h, x1, w_up, conv_ffn, w_down, gt2, g_post, tm, fc):
    b, t, d = x1.shape
    dff = w_down.shape[0]
    n_fc = dff // fc
    wgu = w_up.astype(BF16)
    hal = FFN_HALO
    nb = tm // hal
    last = t // hal - 1
    kern = functools.partial(_ffn_kernel, fc=fc, n_fc=n_fc)
    row = lambda i, j: (i, j, 0)
    per_b = lambda i, j: (i, 0, 0)
    const = lambda i, j: (0, 0)
    return pl.pallas_call(
        kern,
        grid=(b, t // tm),
        in_specs=[pl.BlockSpec((1, tm, d), row),
                  pl.BlockSpec((1, hal, d), lambda i, j: (i, jnp.maximum(j * nb - 1, 0), 0)),
                  pl.BlockSpec((1, hal, d), lambda i, j: (i, jnp.minimum((j + 1) * nb, last), 0)),
                  pl.BlockSpec((1, tm, d), row),
                  pl.BlockSpec(wgu.shape, const),
                  pl.BlockSpec(conv_ffn.shape, const),
                  pl.BlockSpec(w_down.shape, const),
                  pl.BlockSpec((1, 1, d), per_b),
                  pl.BlockSpec((1, d), const)],
        out_specs=pl.BlockSpec((1, tm, d), row),
        out_shape=jax.ShapeDtypeStruct((b, t, d), F32),
        scratch_shapes=[pltpu.VMEM((tm + 2 * hal, d), BF16), pltpu.VMEM((tm, dff), BF16)],
        compiler_params=_params("parallel", "parallel"),
        name="conv_ffn",
    )(h, h, h, x1, wgu, conv_ffn, w_down.astype(BF16), gt2, g_post.reshape(1, d))


def _layer(x, c, ctx, c_ctx, w_mod, b_mod, g_pre_mix, g_post_mix, g_pre_ffn, g_post_ffn,
           w_in, conv_qkv, a_log, dt_bias, o_norm, pool_w, pool_scale, w_out, w_up,
           conv_ffn, w_down):
    b, t, d = x.shape
    n_heads = a_log.shape[-1]
    w = n_heads * HEAD_DIM
    n_gate = 4 * n_heads

    rows = 16
    c_all = jnp.zeros((rows, d), F32).at[:b].set(c).at[b].set(c_ctx)
    m = _modulation(c_all, w_mod, b_mod)
    sh1, sc1, gt1, sh2, sc2, gt2 = [m[:b, None, i * d:(i + 1) * d] for i in range(N_MOD)]
    csh1 = jnp.broadcast_to(m[b, 0 * d:1 * d], (b, 1, d))
    csc1 = jnp.broadcast_to(m[b, 1 * d:2 * d], (b, 1, d))

    off_g = 4 * w
    w_gate = jnp.transpose(w_in[:, off_g:off_g + n_gate].reshape(d, 4, n_heads), (0, 2, 1)).reshape(d, n_gate)
    w_all = jnp.concatenate([w_in[:, :off_g], w_in[:, off_g + n_gate:],
                             jnp.pad(w_gate, ((0, 0), (0, LANES - n_gate)))], axis=1).astype(BF16)
    tm = min(512, t)
    qkv, z, pool_u, gcol, grow = _in_projection(x, sc1, sh1, g_pre_mix, w_all, a_log, dt_bias, n_heads, tm)
    qkv_c, _, _, gcolc, growc = _in_projection(ctx, csc1, csh1, g_pre_mix, w_all, a_log, dt_bias,
                                                n_heads, ctx.shape[1])
    dn = _delta_mixer(qkv, qkv_c, gcol, grow, gcolc, growc, z, conv_qkv, o_norm, n_heads)
    pool = _pool_mixer(pool_u, pool_w, pool_scale)
    x1, h2 = _out_projection(dn, pool, x, w_out, gt1, sc2, sh2, g_post_mix, g_pre_ffn, tm)
    return _conv_ffn(h2, x1, w_up, conv_ffn, w_down, gt2, g_post_ffn, tm, 256)


def kernel(x, c, ctx, c_ctx, w_mod, b_mod, g_pre_mix, g_post_mix, g_pre_ffn, g_post_ffn, w_in,
           conv_qkv, a_log, dt_bias, o_norm, pool_w, pool_scale, w_out, w_up, conv_ffn, w_down):
    depth = w_mod.shape[0]
    assert depth == 1, "context-stream update between layers is not implemented"
    return _layer(x, c, ctx, c_ctx, w_mod[0], b_mod[0], g_pre_mix[0], g_post_mix[0],
                  g_pre_ffn[0], g_post_ffn[0], w_in[0], conv_qkv[0], a_log[0], dt_bias[0],
                  o_norm[0], pool_w[0], pool_scale[0], w_out[0], w_up[0], conv_ffn[0], w_down[0])
```

```python
import functools

import numpy as np
import jax
import jax.numpy as jnp
from jax import lax
from jax.experimental import pallas as pl
from jax.experimental.pallas import tpu as pltpu

NORM_EPS = 1e-6
GRID_W = 64
POOL_WINDOWS = (2, 4, 8, 16)
N_MOD = 6
CHUNK = 128
HEAD_DIM = 128
VMEM_LIMIT = 56 * 1024 * 1024

F32 = jnp.float32
BF16 = jnp.bfloat16
HIGHEST = lax.Precision.HIGHEST


def _silu(a):
    return a * (1.0 / (1.0 + jnp.exp(-a)))


def _dot(a, b):
    return jnp.dot(a.astype(BF16), b.astype(BF16), preferred_element_type=F32)


def _bdot(a, b):
    return jnp.dot(a, b, preferred_element_type=F32).astype(BF16)


def _dot_hi(a, b):
    return jnp.dot(a, b, preferred_element_type=F32, precision=HIGHEST)


def _params(*sem):
    return pltpu.CompilerParams(dimension_semantics=sem, vmem_limit_bytes=VMEM_LIMIT)


def _mod_kernel(c_ref, w_ref, b_ref, o_ref):
    o_ref[...] = _dot_hi(_silu(c_ref[...]), w_ref[...]) + b_ref[...]


def _modulation(c_all, w_mod, b_mod):
    r, d = c_all.shape
    n = w_mod.shape[1]
    tn = 512
    return pl.pallas_call(
        _mod_kernel,
        grid=(n // tn,),
        in_specs=[pl.BlockSpec((r, d), lambda j: (0, 0)),
                  pl.BlockSpec((d, tn), lambda j: (0, j)),
                  pl.BlockSpec((1, tn), lambda j: (0, j))],
        out_specs=pl.BlockSpec((r, tn), lambda j: (0, j)),
        out_shape=jax.ShapeDtypeStruct((r, n), F32),
        compiler_params=_params("parallel"),
        name="modulation",
    )(c_all, w_mod, b_mod.reshape(1, n))


LANES = 128


def _split3(a):
    hi = a.astype(BF16)
    r1 = a - hi.astype(F32)
    mid = r1.astype(BF16)
    lo = (r1 - mid.astype(F32)).astype(BF16)
    return hi, mid, lo


def _inproj_kernel(x_ref, sc_ref, sh_ref, g_ref, w_ref, alog_ref, dtb_ref, tri_ref,
                   qkv_ref, z_ref, pool_ref, gcol_ref, grow_ref, *, n_qkv, n_z, n_pool, n_heads):
    x = x_ref[0]
    xn = x * lax.rsqrt(jnp.mean(x * x, axis=-1, keepdims=True) + NORM_EPS) * g_ref[...]
    h = (xn * (1.0 + sc_ref[0]) + sh_ref[0]).astype(BF16)
    o1 = n_qkv
    o2 = o1 + n_z
    o3 = o2 + n_pool
    pg = jnp.dot(h, w_ref[:, o3:], preferred_element_type=F32)
    qkv_ref[0] = jnp.dot(h, w_ref[:, 0:o1], preferred_element_type=F32).astype(qkv_ref.dtype)
    beta = 1.0 / (1.0 + jnp.exp(-pg))
    a = pg + dtb_ref[...]
    softplus = jnp.maximum(a, 0.0) + jnp.log(1.0 + jnp.exp(-jnp.abs(a)))
    g = -jnp.exp(alog_ref[...]) * softplus
    tm = g.shape[0]
    n_ch = tm // CHUNK
    rows = 4 * n_heads
    beta_t = [beta[c * CHUNK:(c + 1) * CHUNK, :].T[0:rows, :] for c in range(n_ch)]
    g_t = [g[c * CHUNK:(c + 1) * CHUNK, :].T[0:rows, :] for c in range(n_ch)]
    parts = _split3(jnp.concatenate(g_t, axis=0))
    r = jnp.dot(jnp.concatenate(parts, axis=0), tri_ref[...], preferred_element_type=F32)
    z_ref[0] = jnp.dot(h, w_ref[:, o1:o2], preferred_element_type=F32).astype(z_ref.dtype)
    pool_ref[0] = jnp.dot(h, w_ref[:, o2:o3], preferred_element_type=F32).astype(pool_ref.dtype)
    m = n_ch * rows
    pre_all = r[0:m] + r[m:2 * m] + r[2 * m:3 * m]
    kind = lax.broadcasted_iota(jnp.int32, (rows, CHUNK), 0) & 3
    zpad = jnp.zeros((LANES - rows, CHUNK), F32)
    for c in range(n_ch):
        pre = pre_all[c * rows:(c + 1) * rows, :]
        suf = pre[:, CHUNK - 1:CHUNK] - pre + g_t[c]
        row = jnp.where(kind < 2, beta_t[c], jnp.where(kind == 2, pre, suf))
        grow_ref[0, :, c * CHUNK:(c + 1) * CHUNK] = row
        gcol_ref[0, c * CHUNK:(c + 1) * CHUNK, :] = jnp.concatenate([row, zpad], axis=0).T


def _in_projection(x, sc, sh, g_pre, w_all, a_log, dt_bias, n_heads, tm):
    b, t, d = x.shape
    w = n_heads * HEAD_DIM
    n_qkv, n_z = 3 * w, w
    n_pool = w_all.shape[1] - n_qkv - n_z - LANES
    assert CHUNK == LANES
    tri = jnp.asarray(np.triu(np.ones((CHUNK, CHUNK), np.float32)), BF16)
    lane_par = jnp.zeros((2, n_heads, 4), F32)
    lane_par = lane_par.at[0, :, 2:].set(a_log.T).at[1, :, 2:].set(dt_bias.T)
    lane_par = jnp.pad(lane_par.reshape(2, 4 * n_heads), ((0, 0), (0, LANES - 4 * n_heads)))
    kern = functools.partial(_inproj_kernel, n_qkv=n_qkv, n_z=n_z, n_pool=n_pool, n_heads=n_heads)
    row = lambda i, j: (i, j, 0)
    per_b = lambda i, j: (i, 0, 0)
    const = lambda i, j: (0, 0)
    return pl.pallas_call(
        kern,
        grid=(b, t // tm),
        in_specs=[pl.BlockSpec((1, tm, d), row),
                  pl.BlockSpec((1, 1, d), per_b),
                  pl.BlockSpec((1, 1, d), per_b),
                  pl.BlockSpec((1, d), const),
                  pl.BlockSpec(w_all.shape, const),
                  pl.BlockSpec((1, LANES), const),
                  pl.BlockSpec((1, LANES), const),
                  pl.BlockSpec((CHUNK, CHUNK), const)],
        out_specs=[pl.BlockSpec((1, tm, n_qkv), row),
                   pl.BlockSpec((1, tm, n_z), row),
                   pl.BlockSpec((1, tm, n_pool), row),
                   pl.BlockSpec((1, tm, LANES), row),
                   pl.BlockSpec((1, 4 * n_heads, tm), lambda i, j: (i, 0, j))],
        out_shape=[jax.ShapeDtypeStruct((b, t, n_qkv), BF16),
                   jax.ShapeDtypeStruct((b, t, n_z), BF16),
                   jax.ShapeDtypeStruct((b, t, n_pool), BF16),
                   jax.ShapeDtypeStruct((b, t, LANES), F32),
                   jax.ShapeDtypeStruct((b, 4 * n_heads, t), F32)],
        compiler_params=_params("parallel", "parallel"),
        name="in_projection",
    )(x, sc, sh, g_pre.reshape(1, d), w_all, lane_par[0:1], lane_par[1:2], tri)


CONV_HALO = 16


SUBLANES = 8


def _conv_silu(ref, cw_ref, win, c, n_chunks, post=None):
    hal = CONV_HALO
    r0 = pl.multiple_of(c * CHUNK, CHUNK)
    p0 = pl.multiple_of(jnp.maximum(r0 - hal, 0), hal)
    n0 = pl.multiple_of(jnp.minimum(r0 + CHUNK, n_chunks * CHUNK - hal), hal)
    win[0:hal, :] = jnp.where(c > 0, ref[0, pl.ds(p0, hal), :].astype(F32), 0.0)
    win[hal:hal + CHUNK, :] = ref[0, pl.ds(r0, CHUNK), :].astype(F32)
    win[hal + CHUNK:, :] = jnp.where(c < n_chunks - 1, ref[0, pl.ds(n0, hal), :].astype(F32), 0.0)
    width = cw_ref.shape[0]
    pad = width // 2
    n_t = CHUNK // SUBLANES
    taps = [cw_ref[j:j + 1, :] for j in range(width)]
    xs = [win[pl.ds(hal - pad + s, n_t, stride=SUBLANES), :] for s in range(SUBLANES + width - 1)]
    outs = []
    for r in range(SUBLANES):
        acc = xs[r] * taps[0]
        for j in range(1, width):
            acc = acc + xs[r + j] * taps[j]
        acc = _silu(acc)
        outs.append(acc if post is None else post(acc))
    for r in range(SUBLANES):
        win[pl.ds(hal + r, n_t, stride=SUBLANES), :] = outs[r]
    return win[hal:hal + CHUNK, :]


def _l2n(a):
    return a * lax.rsqrt(jnp.sum(a * a, axis=-1, keepdims=True) + NORM_EPS)


INV_BASE = 8
LOCAL_UNROLL = 4
HEAD_PAIR = 2


def _inv_unit(mats):
    n = mats[0].shape[0]
    ii = lax.broadcasted_iota(jnp.int32, (n, n), 0)
    jj = lax.broadcasted_iota(jnp.int32, (n, n), 1)

    def same_block(bits):
        return lax.shift_right_logical(ii, bits) == lax.shift_right_logical(jj, bits)

    bits = int(np.log2(INV_BASE))
    base = same_block(bits)
    eye = jnp.where(ii == jj, 1.0, 0.0).astype(BF16)
    zero = jnp.zeros((), BF16)
    abf = [a.astype(BF16) for a in mats]
    ms = [jnp.where(base, -a, zero) for a in abf]
    ps = [eye + m for m in ms]
    ms = [_bdot(m, m) for m in ms]
    for _ in range(bits - 2):
        pm = [_bdot(jnp.concatenate([p, m], axis=0), m) for p, m in zip(ps, ms)]
        ps = [p + x[:n] for p, x in zip(ps, pm)]
        ms = [x[n:] for x in pm]
    ps = [p + _bdot(p, m) for p, m in zip(ps, ms)]
    while (1 << bits) < n:
        lvl = same_block(bits + 1) & jnp.logical_not(same_block(bits))
        tmp = [_bdot(jnp.where(lvl, a, zero), p) for a, p in zip(abf, ps)]
        ps = [p - _bdot(p, t) for p, t in zip(ps, tmp)]
        bits += 1
    return ps


def _chunk_local(chunks):
    c = chunks[0][1].shape[0]
    ii = lax.broadcasted_iota(jnp.int32, (c, c), 0)
    jj = lax.broadcasted_iota(jnp.int32, (c, c), 1)
    mats, rhs, ktd, attn, qe = [], [], [], [], []
    for q, k, kt, v, kk, qk, gcol, grow in chunks:
        for d in (0, 1):
            incl = (ii >= jj) if d == 0 else (ii <= jj)
            strict = (ii > jj) if d == 0 else (ii < jj)
            beta_c = gcol[:, d:d + 1]
            gc_c = gcol[:, 2 + d:3 + d]
            gc_r = grow[2 + d:3 + d, :]
            glast = gc_r[:, c - 1:c] if d == 0 else gc_r[:, 0:1]
            decay = jnp.where(incl, jnp.exp(jnp.where(incl, gc_c - gc_r, 0.0)), 0.0)
            e_c = jnp.exp(gc_c)
            mats.append(jnp.where(strict, kk * decay * beta_c, 0.0))
            rhs.append(jnp.concatenate([k * (beta_c * e_c), v * beta_c], axis=1).astype(BF16))
            ktd.append((kt * jnp.exp(glast - gc_r)).astype(BF16))
            attn.append(None if q is None else jnp.where(incl, qk * decay, 0.0).astype(BF16))
            qe.append(None if q is None else q * e_c)
    ts = _inv_unit(mats)
    wus = [_dot(t, r) for t, r in zip(ts, rhs)]
    hd = HEAD_DIM
    out = []
    for kd, a, wu, qe_ in zip(ktd, attn, wus, qe):
        if a is None:
            kn_ = _dot(kd, wu)
            out.append((kn_[:, :hd], kn_[:, hd:], None, None))
        else:
            x = _dot(jnp.concatenate([kd, a], axis=0), wu)
            out.append((x[:hd, :hd], x[:hd, hd:], qe_ - x[hd:, :hd], x[hd:, hd:]))
    return out


def _delta_kernel(q_ref, k_ref, v_ref, kc_ref, vc_ref, kc2_ref, vc2_ref, cwq_ref, cwk_ref, cwv_ref,
                  cwk2_ref, cwv2_ref,
                  gcol_ref, grow_ref, gcolc_ref, growc_ref, z_ref, onorm_ref,
                  out_ref, kq_scr, n_scr, o_scr, s_ref, win_scr, stage_a, stage_b, *, n_x, n_c):
    n_tot = n_x + n_c
    hd = HEAD_DIM
    head_shift = lax.rem(LANES - 4 * pl.program_id(1), LANES)
    par = lax.rem(pl.program_id(1), HEAD_PAIR)

    def prep(src_q, src_k, src_v, c0, count, n_seq, stage, slot0=0, cwk=cwk_ref, cwv=cwv_ref):
        for j in range(count):
            c = c0 + j
            sl = slot0 + j
            k = _conv_silu(src_k, cwk, win_scr.at[3 * sl], c, n_seq, _l2n)
            stage[sl, 0] = k
            stage[sl, 1] = k.T
            stage[sl, 2] = _conv_silu(src_v, cwv, win_scr.at[3 * sl + 1], c, n_seq)
            if src_q is not None:
                stage[sl, 3] = _conv_silu(src_q, cwq_ref, win_scr.at[3 * sl + 2], c, n_seq,
                                          lambda a: _l2n(a) * (hd ** -0.5))

    def local(stage, with_q, gcol_r, grow_r, items):
        chunks = []
        for sl, c, pp, shift, _ in items:
            r0 = pl.multiple_of(c * CHUNK, CHUNK)
            gcol = pltpu.roll(gcol_r[0, pl.ds(r0, CHUNK), :], shift, axis=1)[:, 0:4]
            chunks.append([stage[sl, 3] if with_q else None, stage[sl, 0], stage[sl, 1], stage[sl, 2],
                           None, None, gcol, grow_r[0, pp, c]])
        for ch in chunks:
            if ch[0] is None:
                ch[4] = _dot(ch[1], ch[2])
            else:
                kq = _dot(jnp.concatenate([ch[1], ch[0]], axis=0), ch[2])
                ch[4], ch[5] = kq[:CHUNK], kq[CHUNK:]
        res = _chunk_local(chunks)
        for j, (_, c, pp, _, out_slot) in enumerate(items):
            r0 = pl.multiple_of(c * CHUNK, CHUNK)
            for d in (0, 1):
                kmat, nmat, qmat, omat = res[2 * j + d]
                idx = (pp * 2 + d) * n_tot + out_slot
                kq_scr[idx, 0:hd, :] = kmat.astype(BF16)
                n_scr[idx] = nmat.astype(BF16)
                if qmat is not None:
                    kq_scr[idx, hd:, :] = qmat.astype(BF16)
                    o_scr[pp * 2 + d, pl.ds(r0, CHUNK), :] = omat

    u_c = min(LOCAL_UNROLL, n_c)
    u_x = min(LOCAL_UNROLL, n_x)
    g_c, g_x = n_c // u_c, n_x // u_x
    stages = (stage_a, stage_b)

    def prep_x(g, stage):
        prep(q_ref, k_ref, v_ref, g * u_x, u_x, n_x, stage)

    def local_x(g, stage):
        local(stage, True, gcol_ref, grow_ref,
              [(j, g * u_x + j, par, head_shift, n_c + g * u_x + j) for j in range(u_x)])

    if g_c == 1 and HEAD_PAIR * n_c <= LOCAL_UNROLL:
        @pl.when(par == 0)
        def _context_pair():
            items = []
            for pp in range(HEAD_PAIR):
                src = (kc_ref, vc_ref, cwk_ref, cwv_ref) if pp == 0 else (kc2_ref, vc2_ref, cwk2_ref, cwv2_ref)
                prep(None, src[0], src[1], 0, n_c, n_c, stages[0], pp * n_c, src[2], src[3])
                shift = lax.rem(LANES - 4 * (pl.program_id(1) + pp), LANES)
                items += [(pp * n_c + j, j, pp, shift, j) for j in range(n_c)]
            local(stages[0], False, gcolc_ref, growc_ref, items)
            prep_x(0, stages[1])

        @pl.when(par != 0)
        def _latent_only():
            prep_x(0, stages[1])
    else:
        prep(None, kc_ref, vc_ref, 0, u_c, n_c, stages[0])
        for g in range(g_c):
            local(stages[g % 2], False, gcolc_ref, growc_ref,
                  [(j, g * u_c + j, par, head_shift, g * u_c + j) for j in range(u_c)])
            if g + 1 < g_c:
                prep(None, kc_ref, vc_ref, (g + 1) * u_c, u_c, n_c, stages[(g + 1) % 2])
            else:
                prep_x(0, stages[(g + 1) % 2])

    if g_x % 2 == 0:
        def local_pair(p, carry):
            for h in (0, 1):
                g = 2 * p + h
                prep_x(jnp.minimum(g + 1, g_x - 1), stages[(g_c + h + 1) % 2])
                local_x(g, stages[(g_c + h) % 2])
            return carry

        lax.fori_loop(0, g_x // 2, local_pair, 0)
    else:
        for g in range(g_x):
            local_x(g, stages[(g_c + g) % 2])
            if g + 1 < g_x:
                prep_x(g + 1, stages[(g_c + g + 1) % 2])
    @pl.when(par == HEAD_PAIR - 1)
    def _sequential():
        s_ref[...] = jnp.zeros_like(s_ref)
        chains = [(pp, d) for pp in range(HEAD_PAIR) for d in (0, 1)]

        def state_decay(grow, d):
            tot = grow[2 + d:3 + d, CHUNK - 1:CHUNK] if d == 0 else grow[2 + d:3 + d, 0:1]
            return jnp.exp(tot)

        def seq_ctx(i, carry):
            for pp, d in chains:
                c = i if d == 0 else n_c - 1 - i
                idx = (pp * 2 + d) * n_tot + c
                s = s_ref[pp * 2 + d]
                ks = jnp.dot(kq_scr[idx, 0:hd, :], s.astype(BF16), preferred_element_type=F32)
                s_ref[pp * 2 + d] = (s * state_decay(growc_ref[0, pp, c], d) - ks
                                     + n_scr[idx].astype(F32))
            return carry

        def finish(pp, r0, o):
            lanes = slice(pp * hd, (pp + 1) * hd)
            on = o * lax.rsqrt(jnp.mean(o * o, axis=-1, keepdims=True) + NORM_EPS) * onorm_ref[...]
            zg = _silu(z_ref[0, pl.ds(r0, CHUNK), lanes].astype(F32))
            out_ref[0, pl.ds(r0, CHUNK), lanes] = (on * zg).astype(out_ref.dtype)

        def seq_x(i, carry, second_visit):
            for pp, d in chains:
                c = i if d == 0 else n_x - 1 - i
                r0 = pl.multiple_of(c * CHUNK, CHUNK)
                idx = (pp * 2 + d) * n_tot + n_c + c
                s = s_ref[pp * 2 + d]
                kqs = jnp.dot(kq_scr[idx], s.astype(BF16), preferred_element_type=F32)
                o = o_scr[pp * 2 + d, pl.ds(r0, CHUNK), :] + kqs[hd:, :]
                if second_visit:
                    finish(pp, r0, o + o_scr[pp * 2 + 1 - d, pl.ds(r0, CHUNK), :])
                else:
                    o_scr[pp * 2 + d, pl.ds(r0, CHUNK), :] = o
                s_ref[pp * 2 + d] = (s * state_decay(grow_ref[0, pp, c], d) - kqs[0:hd, :]
                                     + n_scr[idx].astype(F32))
            return carry

        lax.fori_loop(0, n_c, seq_ctx, 0)
        half = (n_x + 1) // 2
        lax.fori_loop(0, half, functools.partial(seq_x, second_visit=False), 0)
        if n_x % 2:
            r_mid = (half - 1) * CHUNK
            for pp in range(HEAD_PAIR):
                finish(pp, r_mid, o_scr[pp * 2, r_mid:r_mid + CHUNK, :] + o_scr[pp * 2 + 1, r_mid:r_mid + CHUNK, :])
        lax.fori_loop(half, n_x, functools.partial(seq_x, second_visit=True), 0)


def _row_gates(grow, n_heads):
    b, _, t = grow.shape
    return jnp.transpose(grow.reshape(b, n_heads, 4, t // CHUNK, CHUNK), (0, 1, 3, 2, 4))


def _delta_mixer(qkv, qkv_c, gcol, grow, gcolc, growc, z, conv_qkv, o_norm, n_heads):
    b, t, _ = qkv.shape
    tc = qkv_c.shape[1]
    n_x, n_c = t // CHUNK, tc // CHUNK
    hd = HEAD_DIM
    assert CHUNK == hd and t % CHUNK == 0 and tc % CHUNK == 0
    width = conv_qkv.shape[0]
    grow = _row_gates(grow, n_heads)
    growc = _row_gates(growc, n_heads)
    kern = functools.partial(_delta_kernel, n_x=n_x, n_c=n_c)
    nh = n_heads

    def colblk(off, ahead=0):
        return lambda i, h: (i, 0, off + jnp.minimum(h + ahead, nh - 1))

    def cwblk(off, ahead=0):
        return lambda i, h: (0, off + jnp.minimum(h + ahead, nh - 1))

    assert nh % HEAD_PAIR == 0
    pair5 = lambda i, h: (i, h // HEAD_PAIR, 0, 0, 0)
    pair3 = lambda i, h: (i, 0, h // HEAD_PAIR)
    return pl.pallas_call(
        kern,
        grid=(b, nh),
        in_specs=[pl.BlockSpec((1, t, hd), colblk(0)),
                  pl.BlockSpec((1, t, hd), colblk(nh)),
                  pl.BlockSpec((1, t, hd), colblk(2 * nh)),
                  pl.BlockSpec((1, tc, hd), colblk(nh)),
                  pl.BlockSpec((1, tc, hd), colblk(2 * nh)),
                  pl.BlockSpec((1, tc, hd), colblk(nh, 1)),
                  pl.BlockSpec((1, tc, hd), colblk(2 * nh, 1)),
                  pl.BlockSpec((width, hd), cwblk(0)),
                  pl.BlockSpec((width, hd), cwblk(nh)),
                  pl.BlockSpec((width, hd), cwblk(2 * nh)),
                  pl.BlockSpec((width, hd), cwblk(nh, 1)),
                  pl.BlockSpec((width, hd), cwblk(2 * nh, 1)),
                  pl.BlockSpec((1, t, LANES), lambda i, h: (i, 0, 0)),
                  pl.BlockSpec((1, HEAD_PAIR, n_x, 4, CHUNK), pair5),
                  pl.BlockSpec((1, tc, LANES), lambda i, h: (i, 0, 0)),
                  pl.BlockSpec((1, HEAD_PAIR, n_c, 4, CHUNK), pair5),
                  pl.BlockSpec((1, t, HEAD_PAIR * hd), pair3),
                  pl.BlockSpec((1, hd), lambda i, h: (0, 0))],
        out_specs=pl.BlockSpec((1, t, HEAD_PAIR * hd), pair3),
        out_shape=jax.ShapeDtypeStruct((b, t, nh * hd), BF16),
        scratch_shapes=[pltpu.VMEM((2 * HEAD_PAIR * (n_x + n_c), 2 * hd, hd), BF16),
                        pltpu.VMEM((2 * HEAD_PAIR * (n_x + n_c), hd, hd), BF16),
                        pltpu.VMEM((2 * HEAD_PAIR, t, hd), F32),
                        pltpu.VMEM((2 * HEAD_PAIR, hd, hd), F32),
                        pltpu.VMEM((3 * LOCAL_UNROLL, CHUNK + 2 * CONV_HALO, hd), F32),
                        pltpu.VMEM((LOCAL_UNROLL, 4, CHUNK, hd), F32),
                        pltpu.VMEM((LOCAL_UNROLL, 4, CHUNK, hd), F32)],
        compiler_params=_params("parallel", "arbitrary"),
        name="delta_mixer",
    )(qkv, qkv, qkv, qkv_c, qkv_c, qkv_c, qkv_c, conv_qkv, conv_qkv, conv_qkv, conv_qkv, conv_qkv,
      gcol, grow, gcolc, growc, z, o_norm.reshape(1, hd))


POOL_UNROLL = 4


def _pool_kernel(u_ref, band_ref, icnt_ref, w_ref, sc_ref, out_ref, cs, *, rows, blk):
    g = pl.program_id(1)
    left = lax.shift_left(jnp.int32(1), g)
    right = left - 1
    gw = GRID_W
    gd = cs.shape[1]
    t = rows * gw

    cs[0:gw, :] = jnp.zeros((gw, gd), F32)

    def prefix(r, acc):
        acc = acc + u_ref[0, pl.ds(pl.multiple_of(r * gw, gw), gw), :].astype(F32)
        cs[pl.ds(pl.multiple_of((r + 1) * gw, gw), gw), :] = acc
        return acc

    lax.fori_loop(0, rows, prefix, jnp.zeros((gw, gd), F32))

    def row_mean(r):
        lo = jnp.maximum(r - left, 0)
        hi = jnp.minimum(r + right + 1, rows)
        tot = cs[pl.ds(pl.multiple_of(hi * gw, gw), gw), :] - cs[pl.ds(pl.multiple_of(lo * gw, gw), gw), :]
        return tot / jnp.full((gw, gd), hi - lo, jnp.int32).astype(F32)

    def col_body(i, carry):
        blocks = [i * POOL_UNROLL + j for j in range(POOL_UNROLL)]
        starts = [pl.multiple_of(bi * blk, blk) for bi in blocks]
        rs = [jnp.concatenate([row_mean(bi * (blk // gw) + rr) for rr in range(blk // gw)], axis=0)
              for bi in blocks]
        ms = [jnp.dot(band_ref[0], r.astype(BF16), preferred_element_type=F32) for r in rs]
        ds = [(m * icnt_ref[0] - u_ref[0, pl.ds(r0, blk), :].astype(F32)).astype(BF16)
              for m, r0 in zip(ms, starts)]
        ys = [jnp.dot(dlt, w_ref[0], preferred_element_type=F32) for dlt in ds]
        for y, r0 in zip(ys, starts):
            out_ref[0, pl.ds(r0, blk), :] = (y * sc_ref[0]).astype(out_ref.dtype)
        return carry

    lax.fori_loop(0, t // (blk * POOL_UNROLL), col_body, 0)


def _pool_consts(blk):
    band = np.zeros((len(POOL_WINDOWS), blk, blk), np.float32)
    icnt = np.zeros((len(POOL_WINDOWS), blk, HEAD_DIM), np.float32)
    for gi, w in enumerate(POOL_WINDOWS):
        left = w // 2
        right = w - 1 - left
        for i in range(blk):
            base, col = (i // GRID_W) * GRID_W, i % GRID_W
            lo, hi = max(col - left, 0), min(col + right, GRID_W - 1)
            band[gi, i, base + lo:base + hi + 1] = 1.0
            icnt[gi, i, :] = 1.0 / (hi - lo + 1)
    return jnp.asarray(band, BF16), jnp.asarray(icnt)


def _pool_mixer(u, pool_w, pool_scale):
    b, t, p = u.shape
    n_g, gd = pool_w.shape[0], pool_w.shape[1]
    rows = t // GRID_W
    blk = 4 * GRID_W
    assert t % (blk * POOL_UNROLL) == 0
    band, icnt = _pool_consts(blk)
    kern = functools.partial(_pool_kernel, rows=rows, blk=blk)
    grp = lambda i, g: (g, 0, 0)
    return pl.pallas_call(
        kern,
        grid=(b, n_g),
        in_specs=[pl.BlockSpec((1, t, gd), lambda i, g: (i, 0, g)),
                  pl.BlockSpec((1, blk, blk), grp),
                  pl.BlockSpec((1, blk, gd), grp),
                  pl.BlockSpec((1, gd, gd), grp),
                  pl.BlockSpec((1, 1, gd), grp)],
        out_specs=pl.BlockSpec((1, t, gd), lambda i, g: (i, 0, g)),
        out_shape=jax.ShapeDtypeStruct((b, t, p), BF16),
        scratch_shapes=[pltpu.VMEM((t + GRID_W, gd), F32)],
        compiler_params=_params("parallel", "parallel"),
        name="pool_mixer",
    )(u, band, icnt, pool_w.astype(BF16), pool_scale.reshape(n_g, 1, gd))


def _rms(a, g):
    return a * lax.rsqrt(jnp.mean(a * a, axis=-1, keepdims=True) + NORM_EPS) * g


def _outproj_kernel(dn_ref, pool_ref, x_ref, w1_ref, w2_ref, gt_ref, sc_ref, sh_ref,
                    gpost_ref, gpre_ref, x1_ref, h_ref):
    y = (jnp.dot(dn_ref[0], w1_ref[...], preferred_element_type=F32)
         + jnp.dot(pool_ref[0], w2_ref[...], preferred_element_type=F32))
    x1 = x_ref[0] + gt_ref[0] * _rms(y, gpost_ref[...])
    x1_ref[0] = x1
    h_ref[0] = (_rms(x1, gpre_ref[...]) * (1.0 + sc_ref[0]) + sh_ref[0]).astype(h_ref.dtype)


def _out_projection(dn, pool, x, w_out, gt1, sc2, sh2, g_post, g_pre, tm):
    b, t, d = x.shape
    wd = dn.shape[-1]
    w1 = w_out[:wd].astype(BF16)
    w2 = w_out[wd:].astype(BF16)
    row = lambda i, j: (i, j, 0)
    per_b = lambda i, j: (i, 0, 0)
    const = lambda i, j: (0, 0)
    return pl.pallas_call(
        _outproj_kernel,
        grid=(b, t // tm),
        in_specs=[pl.BlockSpec((1, tm, wd), row),
                  pl.BlockSpec((1, tm, pool.shape[-1]), row),
                  pl.BlockSpec((1, tm, d), row),
                  pl.BlockSpec(w1.shape, const),
                  pl.BlockSpec(w2.shape, const),
                  pl.BlockSpec((1, 1, d), per_b),
                  pl.BlockSpec((1, 1, d), per_b),
                  pl.BlockSpec((1, 1, d), per_b),
                  pl.BlockSpec((1, d), const),
                  pl.BlockSpec((1, d), const)],
        out_specs=[pl.BlockSpec((1, tm, d), row), pl.BlockSpec((1, tm, d), row)],
        out_shape=[jax.ShapeDtypeStruct((b, t, d), F32), jax.ShapeDtypeStruct((b, t, d), BF16)],
        compiler_params=_params("parallel", "parallel"),
        name="out_projection",
    )(dn, pool, x, w1, w2, gt1, sc2, sh2, g_post.reshape(1, d), g_pre.reshape(1, d))


FFN_HALO = 16


def _ffn_kernel(h_ref, hp_ref, hn_ref, x1_ref, wgu_ref, cw_ref, wd_ref, gt_ref, gpost_ref,
                out_ref, hext, act, *, fc, n_fc):
    j = pl.program_id(1)
    nj = pl.num_programs(1)
    tm = h_ref.shape[1]
    hal = FFN_HALO
    dff = fc * n_fc
    hext[0:hal, :] = jnp.where(j > 0, hp_ref[0], jnp.zeros_like(hp_ref[0]))
    hext[hal:hal + tm, :] = h_ref[0]
    hext[hal + tm:, :] = jnp.where(j < nj - 1, hn_ref[0], jnp.zeros_like(hn_ref[0]))
    for c in range(n_fc):
        gate = jnp.dot(hext[...], wgu_ref[:, c * fc:(c + 1) * fc], preferred_element_type=F32)
        up = jnp.dot(h_ref[0], wgu_ref[:, dff + c * fc:dff + (c + 1) * fc], preferred_element_type=F32)
        cw = cw_ref[:, c * fc:(c + 1) * fc]
        conv = (gate[hal - 1:hal - 1 + tm] * cw[0:1] + gate[hal:hal + tm] * cw[1:2]
                + gate[hal + 1:hal + 1 + tm] * cw[2:3])
        act[:, c * fc:(c + 1) * fc] = (_silu(conv) * up).astype(BF16)
    y = jnp.dot(act[...], wd_ref[...], preferred_element_type=F32)
    out_ref[0] = x1_ref[0] + gt_ref[0] * _rms(y, gpost_ref[...])


def _conv_ffn(h, x1, w_up, conv_ffn, w_down, gt2, g_post, tm, fc):
    b, t, d = x1.shape
    dff = w_down.shape[0]
    n_fc = dff // fc
    wgu = w_up.astype(BF16)
    hal = FFN_HALO
    nb = tm // hal
    last = t // hal - 1
    kern = functools.partial(_ffn_kernel, fc=fc, n_fc=n_fc)
    row = lambda i, j: (i, j, 0)
    per_b = lambda i, j: (i, 0, 0)
    const = lambda i, j: (0, 0)
    return pl.pallas_call(
        kern,
        grid=(b, t // tm),
        in_specs=[pl.BlockSpec((1, tm, d), row),
                  pl.BlockSpec((1, hal, d), lambda i, j: (i, jnp.maximum(j * nb - 1, 0), 0)),
                  pl.BlockSpec((1, hal, d), lambda i, j: (i, jnp.minimum((j + 1) * nb, last), 0)),
                  pl.BlockSpec((1, tm, d), row),
                  pl.BlockSpec(wgu.shape, const),
                  pl.BlockSpec(conv_ffn.shape, const),
                  pl.BlockSpec(w_down.shape, const),
                  pl.BlockSpec((1, 1, d), per_b),
                  pl.BlockSpec((1, d), const)],
        out_specs=pl.BlockSpec((1, tm, d), row),
        out_shape=jax.ShapeDtypeStruct((b, t, d), F32),
        scratch_shapes=[pltpu.VMEM((tm + 2 * hal, d), BF16), pltpu.VMEM((tm, dff), BF16)],
        compiler_params=_params("parallel", "parallel"),
        name="conv_ffn",
    )(h, h, h, x1, wgu, conv_ffn, w_down.astype(BF16), gt2, g_post.reshape(1, d))


def _layer(x, c, ctx, c_ctx, w_mod, b_mod, g_pre_mix, g_post_mix, g_pre_ffn, g_post_ffn,
           w_in, conv_qkv, a_log, dt_bias, o_norm, pool_w, pool_scale, w_out, w_up,
           conv_ffn, w_down):
    b, t, d = x.shape
    n_heads = a_log.shape[-1]
    w = n_heads * HEAD_DIM
    n_gate = 4 * n_heads

    rows = 16
    c_all = jnp.zeros((rows, d), F32).at[:b].set(c).at[b].set(c_ctx)
    m = _modulation(c_all, w_mod, b_mod)
    sh1, sc1, gt1, sh2, sc2, gt2 = [m[:b, None, i * d:(i + 1) * d] for i in range(N_MOD)]
    csh1 = jnp.broadcast_to(m[b, 0 * d:1 * d], (b, 1, d))
    csc1 = jnp.broadcast_to(m[b, 1 * d:2 * d], (b, 1, d))

    off_g = 4 * w
    w_gate = jnp.transpose(w_in[:, off_g:off_g + n_gate].reshape(d, 4, n_heads), (0, 2, 1)).reshape(d, n_gate)
    w_all = jnp.concatenate([w_in[:, :off_g], w_in[:, off_g + n_gate:],
                             jnp.pad(w_gate, ((0, 0), (0, LANES - n_gate)))], axis=1).astype(BF16)
    tm = min(512, t)
    qkv, z, pool_u, gcol, grow = _in_projection(x, sc1, sh1, g_pre_mix, w_all, a_log, dt_bias, n_heads, tm)
    qkv_c, _, _, gcolc, growc = _in_projection(ctx, csc1, csh1, g_pre_mix, w_all, a_log, dt_bias,
                                                n_heads, ctx.shape[1])
    dn = _delta_mixer(qkv, qkv_c, gcol, grow, gcolc, growc, z, conv_qkv, o_norm, n_heads)
    pool = _pool_mixer(pool_u, pool_w, pool_scale)
    x1, h2 = _out_projection(dn, pool, x, w_out, gt1, sc2, sh2, g_post_mix, g_pre_ffn, tm)
    return _conv_ffn(h2, x1, w_up, conv_ffn, w_down, gt2, g_post_ffn, tm, 256)


def kernel(x, c, ctx, c_ctx, w_mod, b_mod, g_pre_mix, g_post_mix, g_pre_ffn, g_post_ffn, w_in,
           conv_qkv, a_log, dt_bias, o_norm, pool_w, pool_scale, w_out, w_up, conv_ffn, w_down):
    depth = w_mod.shape[0]
    assert depth == 1, "context-stream update between layers is not implemented"
    return _layer(x, c, ctx, c_ctx, w_mod[0], b_mod[0], g_pre_mix[0], g_post_mix[0],
                  g_pre_ffn[0], g_post_ffn[0], w_in[0], conv_qkv[0], a_log[0], dt_bias[0],
                  o_norm[0], pool_w[0], pool_scale[0], w_out[0], w_up[0], conv_ffn[0], w_down[0])
```

```python
import functools

import numpy as np
import jax
import jax.numpy as jnp
from jax import lax
from jax.experimental import pallas as pl
from jax.experimental.pallas import tpu as pltpu

NORM_EPS = 1e-6
GRID_W = 64
POOL_WINDOWS = (2, 4, 8, 16)
N_MOD = 6
CHUNK = 128
HEAD_DIM = 128
VMEM_LIMIT = 56 * 1024 * 1024

F32 = jnp.float32
BF16 = jnp.bfloat16


def _silu(a):
    return a * (1.0 / (1.0 + jnp.exp(-a)))


def _dot(a, b):
    return jnp.dot(a.astype(BF16), b.astype(BF16), preferred_element_type=F32)


def _bdot(a, b):
    return jnp.dot(a, b, preferred_element_type=F32).astype(BF16)


def _split3(a):
    hi = a.astype(BF16)
    r1 = a - hi.astype(F32)
    mid = r1.astype(BF16)
    lo = (r1 - mid.astype(F32)).astype(BF16)
    return hi, mid, lo


def _dot_f32(a, b):
    ah, am, al = _split3(a)
    bh, bm, bl = _split3(b)
    d = lambda x, y: jnp.dot(x, y, preferred_element_type=F32)
    return ((d(al, bh) + d(ah, bl)) + d(am, bm)) + (d(am, bh) + d(ah, bm)) + d(ah, bh)


def _params(*sem):
    return pltpu.CompilerParams(dimension_semantics=sem, vmem_limit_bytes=VMEM_LIMIT)


def _mod_kernel(c_ref, w_ref, b_ref, o_ref):
    o_ref[...] = _dot_f32(_silu(c_ref[...]), w_ref[...]) + b_ref[...]


def _modulation(c_all, w_mod, b_mod):
    r, d = c_all.shape
    n = w_mod.shape[1]
    tn = 512
    return pl.pallas_call(
        _mod_kernel,
        grid=(n // tn,),
        in_specs=[pl.BlockSpec((r, d), lambda j: (0, 0)),
                  pl.BlockSpec((d, tn), lambda j: (0, j)),
                  pl.BlockSpec((1, tn), lambda j: (0, j))],
        out_specs=pl.BlockSpec((r, tn), lambda j: (0, j)),
        out_shape=jax.ShapeDtypeStruct((r, n), F32),
        compiler_params=_params("parallel"),
        name="modulation",
    )(c_all, w_mod, b_mod.reshape(1, n))


LANES = 128


def _inproj_kernel(x_ref, sc_ref, sh_ref, g_ref, w_ref, alog_ref, dtb_ref, tri_ref,
                   qkv_ref, z_ref, pool_ref, gcol_ref, grow_ref, *, n_qkv, n_z, n_pool, n_heads):
    x = x_ref[0]
    xn = x * lax.rsqrt(jnp.mean(x * x, axis=-1, keepdims=True) + NORM_EPS) * g_ref[...]
    h = (xn * (1.0 + sc_ref[0]) + sh_ref[0]).astype(BF16)
    o1 = n_qkv
    o2 = o1 + n_z
    o3 = o2 + n_pool
    pg = jnp.dot(h, w_ref[:, o3:], preferred_element_type=F32)
    qkv_ref[0] = jnp.dot(h, w_ref[:, 0:o1], preferred_element_type=F32).astype(qkv_ref.dtype)
    beta = 1.0 / (1.0 + jnp.exp(-pg))
    a = pg + dtb_ref[...]
    softplus = jnp.maximum(a, 0.0) + jnp.log(1.0 + jnp.exp(-jnp.abs(a)))
    g = -jnp.exp(alog_ref[...]) * softplus
    tm = g.shape[0]
    n_ch = tm // CHUNK
    rows = 4 * n_heads
    beta_t = [beta[c * CHUNK:(c + 1) * CHUNK, :].T[0:rows, :] for c in range(n_ch)]
    g_t = [g[c * CHUNK:(c + 1) * CHUNK, :].T[0:rows, :] for c in range(n_ch)]
    parts = _split3(jnp.concatenate(g_t, axis=0))
    r = jnp.dot(jnp.concatenate(parts, axis=0), tri_ref[...], preferred_element_type=F32)
    z_ref[0] = jnp.dot(h, w_ref[:, o1:o2], preferred_element_type=F32).astype(z_ref.dtype)
    pool_ref[0] = jnp.dot(h, w_ref[:, o2:o3], preferred_element_type=F32).astype(pool_ref.dtype)
    m = n_ch * rows
    pre_all = r[0:m] + r[m:2 * m] + r[2 * m:3 * m]
    kind = lax.broadcasted_iota(jnp.int32, (rows, CHUNK), 0) & 3
    zpad = jnp.zeros((LANES - rows, CHUNK), F32)
    for c in range(n_ch):
        pre = pre_all[c * rows:(c + 1) * rows, :]
        suf = pre[:, CHUNK - 1:CHUNK] - pre + g_t[c]
        row = jnp.where(kind < 2, beta_t[c], jnp.where(kind == 2, pre, suf))
        grow_ref[0, :, c * CHUNK:(c + 1) * CHUNK] = row
        gcol_ref[0, c * CHUNK:(c + 1) * CHUNK, :] = jnp.concatenate([row, zpad], axis=0).T


def _in_projection(x, sc, sh, g_pre, w_all, a_log, dt_bias, n_heads, tm):
    b, t, d = x.shape
    w = n_heads * HEAD_DIM
    n_qkv, n_z = 3 * w, w
    n_pool = w_all.shape[1] - n_qkv - n_z - LANES
    assert CHUNK == LANES
    tri = jnp.asarray(np.triu(np.ones((CHUNK, CHUNK), np.float32)), BF16)
    lane_par = jnp.zeros((2, n_heads, 4), F32)
    lane_par = lane_par.at[0, :, 2:].set(a_log.T).at[1, :, 2:].set(dt_bias.T)
    lane_par = jnp.pad(lane_par.reshape(2, 4 * n_heads), ((0, 0), (0, LANES - 4 * n_heads)))
    kern = functools.partial(_inproj_kernel, n_qkv=n_qkv, n_z=n_z, n_pool=n_pool, n_heads=n_heads)
    row = lambda i, j: (i, j, 0)
    per_b = lambda i, j: (i, 0, 0)
    const = lambda i, j: (0, 0)
    return pl.pallas_call(
        kern,
        grid=(b, t // tm),
        in_specs=[pl.BlockSpec((1, tm, d), row),
                  pl.BlockSpec((1, 1, d), per_b),
                  pl.BlockSpec((1, 1, d), per_b),
                  pl.BlockSpec((1, d), const),
                  pl.BlockSpec(w_all.shape, const),
                  pl.BlockSpec((1, LANES), const),
                  pl.BlockSpec((1, LANES), const),
                  pl.BlockSpec((CHUNK, CHUNK), const)],
        out_specs=[pl.BlockSpec((1, tm, n_qkv), row),
                   pl.BlockSpec((1, tm, n_z), row),
                   pl.BlockSpec((1, tm, n_pool), row),
                   pl.BlockSpec((1, tm, LANES), row),
                   pl.BlockSpec((1, 4 * n_heads, tm), lambda i, j: (i, 0, j))],
        out_shape=[jax.ShapeDtypeStruct((b, t, n_qkv), BF16),
                   jax.ShapeDtypeStruct((b, t, n_z), BF16),
                   jax.ShapeDtypeStruct((b, t, n_pool), BF16),
                   jax.ShapeDtypeStruct((b, t, LANES), F32),
                   jax.ShapeDtypeStruct((b, 4 * n_heads, t), F32)],
        compiler_params=_params("parallel", "parallel"),
        name="in_projection",
    )(x, sc, sh, g_pre.reshape(1, d), w_all, lane_par[0:1], lane_par[1:2], tri)


CONV_HALO = 16


SUBLANES = 8


def _conv_silu(ref, cw_ref, win, c, n_chunks, post=None):
    hal = CONV_HALO
    r0 = pl.multiple_of(c * CHUNK, CHUNK)
    p0 = pl.multiple_of(jnp.maximum(r0 - hal, 0), hal)
    n0 = pl.multiple_of(jnp.minimum(r0 + CHUNK, n_chunks * CHUNK - hal), hal)
    win[0:hal, :] = jnp.where(c > 0, ref[0, pl.ds(p0, hal), :].astype(F32), 0.0)
    win[hal:hal + CHUNK, :] = ref[0, pl.ds(r0, CHUNK), :].astype(F32)
    win[hal + CHUNK:, :] = jnp.where(c < n_chunks - 1, ref[0, pl.ds(n0, hal), :].astype(F32), 0.0)
    width = cw_ref.shape[0]
    pad = width // 2
    n_t = CHUNK // SUBLANES
    taps = [cw_ref[j:j + 1, :] for j in range(width)]
    xs = [win[pl.ds(hal - pad + s, n_t, stride=SUBLANES), :] for s in range(SUBLANES + width - 1)]
    outs = []
    for r in range(SUBLANES):
        acc = xs[r] * taps[0]
        for j in range(1, width):
            acc = acc + xs[r + j] * taps[j]
        acc = _silu(acc)
        outs.append(acc if post is None else post(acc))
    for r in range(SUBLANES):
        win[pl.ds(hal + r, n_t, stride=SUBLANES), :] = outs[r]
    return win[hal:hal + CHUNK, :]


def _l2n(a):
    return a * lax.rsqrt(jnp.sum(a * a, axis=-1, keepdims=True) + NORM_EPS)


INV_BASE = 8
LOCAL_UNROLL = 4
HEAD_PAIR = 2


def _inv_unit(mats):
    n = mats[0].shape[0]
    ii = lax.broadcasted_iota(jnp.int32, (n, n), 0)
    jj = lax.broadcasted_iota(jnp.int32, (n, n), 1)

    def same_block(bits):
        return lax.shift_right_logical(ii, bits) == lax.shift_right_logical(jj, bits)

    bits = int(np.log2(INV_BASE))
    base = same_block(bits)
    eye = jnp.where(ii == jj, 1.0, 0.0).astype(BF16)
    zero = jnp.zeros((), BF16)
    abf = [a.astype(BF16) for a in mats]
    ms = [jnp.where(base, -a, zero) for a in abf]
    ps = [eye + m for m in ms]
    ms = [_bdot(m, m) for m in ms]
    for _ in range(bits - 2):
        pm = [_bdot(jnp.concatenate([p, m], axis=0), m) for p, m in zip(ps, ms)]
        ps = [p + x[:n] for p, x in zip(ps, pm)]
        ms = [x[n:] for x in pm]
    ps = [p + _bdot(p, m) for p, m in zip(ps, ms)]
    while (1 << bits) < n:
        lvl = same_block(bits + 1) & jnp.logical_not(same_block(bits))
        tmp = [_bdot(jnp.where(lvl, a, zero), p) for a, p in zip(abf, ps)]
        ps = [p - _bdot(p, t) for p, t in zip(ps, tmp)]
        bits += 1
    return ps


def _chunk_local(chunks):
    c = chunks[0][1].shape[0]
    ii = lax.broadcasted_iota(jnp.int32, (c, c), 0)
    jj = lax.broadcasted_iota(jnp.int32, (c, c), 1)
    mats, rhs, ktd, attn, qe = [], [], [], [], []
    for q, k, kt, v, kk, qk, gcol, grow in chunks:
        for d in (0, 1):
            incl = (ii >= jj) if d == 0 else (ii <= jj)
            strict = (ii > jj) if d == 0 else (ii < jj)
            beta_c = gcol[:, d:d + 1]
            gc_c = gcol[:, 2 + d:3 + d]
            gc_r = grow[2 + d:3 + d, :]
            glast = gc_r[:, c - 1:c] if d == 0 else gc_r[:, 0:1]
            decay = jnp.where(incl, jnp.exp(jnp.where(incl, gc_c - gc_r, 0.0)), 0.0)
            e_c = jnp.exp(gc_c)
            mats.append(jnp.where(strict, kk * decay * beta_c, 0.0))
            rhs.append(jnp.concatenate([k * (beta_c * e_c), v * beta_c], axis=1).astype(BF16))
            ktd.append((kt * jnp.exp(glast - gc_r)).astype(BF16))
            attn.append(None if q is None else jnp.where(incl, qk * decay, 0.0).astype(BF16))
            qe.append(None if q is None else q * e_c)
    ts = _inv_unit(mats)
    wus = [_dot(t, r) for t, r in zip(ts, rhs)]
    hd = HEAD_DIM
    out = []
    for kd, a, wu, qe_ in zip(ktd, attn, wus, qe):
        if a is None:
            kn_ = _dot(kd, wu)
            out.append((kn_[:, :hd], kn_[:, hd:], None, None))
        else:
            x = _dot(jnp.concatenate([kd, a], axis=0), wu)
            out.append((x[:hd, :hd], x[:hd, hd:], qe_ - x[hd:, :hd], x[hd:, hd:]))
    return out


def _delta_kernel(q_ref, k_ref, v_ref, kc_ref, vc_ref, kc2_ref, vc2_ref, cwq_ref, cwk_ref, cwv_ref,
                  cwk2_ref, cwv2_ref,
                  gcol_ref, grow_ref, gcolc_ref, growc_ref, z_ref, onorm_ref,
                  out_ref, kq_scr, n_scr, o_scr, s_ref, win_scr, stage_a, stage_b, *, n_x, n_c):
    n_tot = n_x + n_c
    hd = HEAD_DIM
    head_shift = lax.rem(LANES - 4 * pl.program_id(1), LANES)
    par = lax.rem(pl.program_id(1), HEAD_PAIR)

    def prep(src_q, src_k, src_v, c0, count, n_seq, stage, slot0=0, cwk=cwk_ref, cwv=cwv_ref):
        for j in range(count):
            c = c0 + j
            sl = slot0 + j
            k = _conv_silu(src_k, cwk, win_scr.at[3 * sl], c, n_seq, _l2n)
            stage[sl, 0] = k
            stage[sl, 1] = k.T
            stage[sl, 2] = _conv_silu(src_v, cwv, win_scr.at[3 * sl + 1], c, n_seq)
            if src_q is not None:
                stage[sl, 3] = _conv_silu(src_q, cwq_ref, win_scr.at[3 * sl + 2], c, n_seq,
                                          lambda a: _l2n(a) * (hd ** -0.5))

    def local(stage, with_q, gcol_r, grow_r, items):
        chunks = []
        for sl, c, pp, shift, _ in items:
            r0 = pl.multiple_of(c * CHUNK, CHUNK)
            gcol = pltpu.roll(gcol_r[0, pl.ds(r0, CHUNK), :], shift, axis=1)[:, 0:4]
            chunks.append([stage[sl, 3] if with_q else None, stage[sl, 0], stage[sl, 1], stage[sl, 2],
                           None, None, gcol, grow_r[0, pp, c]])
        for ch in chunks:
            if ch[0] is None:
                ch[4] = _dot(ch[1], ch[2])
            else:
                kq = _dot(jnp.concatenate([ch[1], ch[0]], axis=0), ch[2])
                ch[4], ch[5] = kq[:CHUNK], kq[CHUNK:]
        res = _chunk_local(chunks)
        for j, (_, c, pp, _, out_slot) in enumerate(items):
            r0 = pl.multiple_of(c * CHUNK, CHUNK)
            for d in (0, 1):
                kmat, nmat, qmat, omat = res[2 * j + d]
                idx = (pp * 2 + d) * n_tot + out_slot
                kq_scr[idx, 0:hd, :] = kmat.astype(BF16)
                n_scr[idx] = nmat.astype(BF16)
                if qmat is not None:
                    kq_scr[idx, hd:, :] = qmat.astype(BF16)
                    o_scr[pp * 2 + d, pl.ds(r0, CHUNK), :] = omat

    u_c = min(LOCAL_UNROLL, n_c)
    u_x = min(LOCAL_UNROLL, n_x)
    g_c, g_x = n_c // u_c, n_x // u_x
    stages = (stage_a, stage_b)

    def prep_x(g, stage):
        prep(q_ref, k_ref, v_ref, g * u_x, u_x, n_x, stage)

    def local_x(g, stage):
        local(stage, True, gcol_ref, grow_ref,
              [(j, g * u_x + j, par, head_shift, n_c + g * u_x + j) for j in range(u_x)])

    if g_c == 1 and HEAD_PAIR * n_c <= LOCAL_UNROLL:
        @pl.when(par == 0)
        def _context_pair():
            items = []
            for pp in range(HEAD_PAIR):
                src = (kc_ref, vc_ref, cwk_ref, cwv_ref) if pp == 0 else (kc2_ref, vc2_ref, cwk2_ref, cwv2_ref)
                prep(None, src[0], src[1], 0, n_c, n_c, stages[0], pp * n_c, src[2], src[3])
                shift = lax.rem(LANES - 4 * (pl.program_id(1) + pp), LANES)
                items += [(pp * n_c + j, j, pp, shift, j) for j in range(n_c)]
            local(stages[0], False, gcolc_ref, growc_ref, items)
            prep_x(0, stages[1])

        @pl.when(par != 0)
        def _latent_only():
            prep_x(0, stages[1])
    else:
        prep(None, kc_ref, vc_ref, 0, u_c, n_c, stages[0])
        for g in range(g_c):
            local(stages[g % 2], False, gcolc_ref, growc_ref,
                  [(j, g * u_c + j, par, head_shift, g * u_c + j) for j in range(u_c)])
            if g + 1 < g_c:
                prep(None, kc_ref, vc_ref, (g + 1) * u_c, u_c, n_c, stages[(g + 1) % 2])
            else:
                prep_x(0, stages[(g + 1) % 2])

    if g_x % 2 == 0:
        def local_pair(p, carry):
            for h in (0, 1):
                g = 2 * p + h
                prep_x(jnp.minimum(g + 1, g_x - 1), stages[(g_c + h + 1) % 2])
                local_x(g, stages[(g_c + h) % 2])
            return carry

        lax.fori_loop(0, g_x // 2, local_pair, 0)
    else:
        for g in range(g_x):
            local_x(g, stages[(g_c + g) % 2])
            if g + 1 < g_x:
                prep_x(g + 1, stages[(g_c + g + 1) % 2])
    @pl.when(par == HEAD_PAIR - 1)
    def _sequential():
        s_ref[...] = jnp.zeros_like(s_ref)
        chains = [(pp, d) for pp in range(HEAD_PAIR) for d in (0, 1)]

        def state_decay(grow, d):
            tot = grow[2 + d:3 + d, CHUNK - 1:CHUNK] if d == 0 else grow[2 + d:3 + d, 0:1]
            return jnp.exp(tot)

        def seq_ctx(i, carry):
            for pp, d in chains:
                c = i if d == 0 else n_c - 1 - i
                idx = (pp * 2 + d) * n_tot + c
                s = s_ref[pp * 2 + d]
                ks = jnp.dot(kq_scr[idx, 0:hd, :], s.astype(BF16), preferred_element_type=F32)
                s_ref[pp * 2 + d] = (s * state_decay(growc_ref[0, pp, c], d) - ks
                                     + n_scr[idx].astype(F32))
            return carry

        def finish(pp, r0, o):
            lanes = slice(pp * hd, (pp + 1) * hd)
            on = o * lax.rsqrt(jnp.mean(o * o, axis=-1, keepdims=True) + NORM_EPS) * onorm_ref[...]
            zg = _silu(z_ref[0, pl.ds(r0, CHUNK), lanes].astype(F32))
            out_ref[0, pl.ds(r0, CHUNK), lanes] = (on * zg).astype(out_ref.dtype)

        def seq_x(i, carry, second_visit):
            for pp, d in chains:
                c = i if d == 0 else n_x - 1 - i
                r0 = pl.multiple_of(c * CHUNK, CHUNK)
                idx = (pp * 2 + d) * n_tot + n_c + c
                s = s_ref[pp * 2 + d]
                kqs = jnp.dot(kq_scr[idx], s.astype(BF16), preferred_element_type=F32)
                o = o_scr[pp * 2 + d, pl.ds(r0, CHUNK), :] + kqs[hd:, :]
                if second_visit:
                    finish(pp, r0, o + o_scr[pp * 2 + 1 - d, pl.ds(r0, CHUNK), :])
                else:
                    o_scr[pp * 2 + d, pl.ds(r0, CHUNK), :] = o
                s_ref[pp * 2 + d] = (s * state_decay(grow_ref[0, pp, c], d) - kqs[0:hd, :]
                                     + n_scr[idx].astype(F32))
            return carry

        lax.fori_loop(0, n_c, seq_ctx, 0)
        half = (n_x + 1) // 2
        lax.fori_loop(0, half, functools.partial(seq_x, second_visit=False), 0)
        if n_x % 2:
            r_mid = (half - 1) * CHUNK
            for pp in range(HEAD_PAIR):
                finish(pp, r_mid, o_scr[pp * 2, r_mid:r_mid + CHUNK, :] + o_scr[pp * 2 + 1, r_mid:r_mid + CHUNK, :])
        lax.fori_loop(half, n_x, functools.partial(seq_x, second_visit=True), 0)


def _row_gates(grow, n_heads):
    b, _, t = grow.shape
    return jnp.transpose(grow.reshape(b, n_heads, 4, t // CHUNK, CHUNK), (0, 1, 3, 2, 4))


def _delta_mixer(qkv, qkv_c, gcol, grow, gcolc, growc, z, conv_qkv, o_norm, n_heads):
    b, t, _ = qkv.shape
    tc = qkv_c.shape[1]
    n_x, n_c = t // CHUNK, tc // CHUNK
    hd = HEAD_DIM
    assert CHUNK == hd and t % CHUNK == 0 and tc % CHUNK == 0
    width = conv_qkv.shape[0]
    grow = _row_gates(grow, n_heads)
    growc = _row_gates(growc, n_heads)
    kern = functools.partial(_delta_kernel, n_x=n_x, n_c=n_c)
    nh = n_heads

    def colblk(off, ahead=0):
        return lambda i, h: (i, 0, off + jnp.minimum(h + ahead, nh - 1))

    def cwblk(off, ahead=0):
        return lambda i, h: (0, off + jnp.minimum(h + ahead, nh - 1))

    assert nh % HEAD_PAIR == 0
    pair5 = lambda i, h: (i, h // HEAD_PAIR, 0, 0, 0)
    pair3 = lambda i, h: (i, 0, h // HEAD_PAIR)
    return pl.pallas_call(
        kern,
        grid=(b, nh),
        in_specs=[pl.BlockSpec((1, t, hd), colblk(0)),
                  pl.BlockSpec((1, t, hd), colblk(nh)),
                  pl.BlockSpec((1, t, hd), colblk(2 * nh)),
                  pl.BlockSpec((1, tc, hd), colblk(nh)),
                  pl.BlockSpec((1, tc, hd), colblk(2 * nh)),
                  pl.BlockSpec((1, tc, hd), colblk(nh, 1)),
                  pl.BlockSpec((1, tc, hd), colblk(2 * nh, 1)),
                  pl.BlockSpec((width, hd), cwblk(0)),
                  pl.BlockSpec((width, hd), cwblk(nh)),
                  pl.BlockSpec((width, hd), cwblk(2 * nh)),
                  pl.BlockSpec((width, hd), cwblk(nh, 1)),
                  pl.BlockSpec((width, hd), cwblk(2 * nh, 1)),
                  pl.BlockSpec((1, t, LANES), lambda i, h: (i, 0, 0)),
                  pl.BlockSpec((1, HEAD_PAIR, n_x, 4, CHUNK), pair5),
                  pl.BlockSpec((1, tc, LANES), lambda i, h: (i, 0, 0)),
                  pl.BlockSpec((1, HEAD_PAIR, n_c, 4, CHUNK), pair5),
                  pl.BlockSpec((1, t, HEAD_PAIR * hd), pair3),
                  pl.BlockSpec((1, hd), lambda i, h: (0, 0))],
        out_specs=pl.BlockSpec((1, t, HEAD_PAIR * hd), pair3),
        out_shape=jax.ShapeDtypeStruct((b, t, nh * hd), BF16),
        scratch_shapes=[pltpu.VMEM((2 * HEAD_PAIR * (n_x + n_c), 2 * hd, hd), BF16),
                        pltpu.VMEM((2 * HEAD_PAIR * (n_x + n_c), hd, hd), BF16),
                        pltpu.VMEM((2 * HEAD_PAIR, t, hd), F32),
                        pltpu.VMEM((2 * HEAD_PAIR, hd, hd), F32),
                        pltpu.VMEM((3 * LOCAL_UNROLL, CHUNK + 2 * CONV_HALO, hd), F32),
                        pltpu.VMEM((LOCAL_UNROLL, 4, CHUNK, hd), F32),
                        pltpu.VMEM((LOCAL_UNROLL, 4, CHUNK, hd), F32)],
        compiler_params=_params("parallel", "arbitrary"),
        name="delta_mixer",
    )(qkv, qkv, qkv, qkv_c, qkv_c, qkv_c, qkv_c, conv_qkv, conv_qkv, conv_qkv, conv_qkv, conv_qkv,
      gcol, grow, gcolc, growc, z, o_norm.reshape(1, hd))


POOL_UNROLL = 4


def _pool_kernel(u_ref, band_ref, icnt_ref, w_ref, sc_ref, out_ref, cs, *, rows, blk):
    g = pl.program_id(1)
    left = lax.shift_left(jnp.int32(1), g)
    right = left - 1
    gw = GRID_W
    gd = cs.shape[1]
    t = rows * gw

    cs[0:gw, :] = jnp.zeros((gw, gd), F32)

    def prefix(r, acc):
        acc = acc + u_ref[0, pl.ds(pl.multiple_of(r * gw, gw), gw), :].astype(F32)
        cs[pl.ds(pl.multiple_of((r + 1) * gw, gw), gw), :] = acc
        return acc

    lax.fori_loop(0, rows, prefix, jnp.zeros((gw, gd), F32))

    def row_mean(r):
        lo = jnp.maximum(r - left, 0)
        hi = jnp.minimum(r + right + 1, rows)
        tot = cs[pl.ds(pl.multiple_of(hi * gw, gw), gw), :] - cs[pl.ds(pl.multiple_of(lo * gw, gw), gw), :]
        return tot / jnp.full((gw, gd), hi - lo, jnp.int32).astype(F32)

    def col_body(i, carry):
        blocks = [i * POOL_UNROLL + j for j in range(POOL_UNROLL)]
        starts = [pl.multiple_of(bi * blk, blk) for bi in blocks]
        rs = [jnp.concatenate([row_mean(bi * (blk // gw) + rr) for rr in range(blk // gw)], axis=0)
              for bi in blocks]
        ms = [jnp.dot(band_ref[0], r.astype(BF16), preferred_element_type=F32) for r in rs]
        ds = [(m * icnt_ref[0] - u_ref[0, pl.ds(r0, blk), :].astype(F32)).astype(BF16)
              for m, r0 in zip(ms, starts)]
        ys = [jnp.dot(dlt, w_ref[0], preferred_element_type=F32) for dlt in ds]
        for y, r0 in zip(ys, starts):
            out_ref[0, pl.ds(r0, blk), :] = (y * sc_ref[0]).astype(out_ref.dtype)
        return carry

    lax.fori_loop(0, t // (blk * POOL_UNROLL), col_body, 0)


def _pool_consts(blk):
    band = np.zeros((len(POOL_WINDOWS), blk, blk), np.float32)
    icnt = np.zeros((len(POOL_WINDOWS), blk, HEAD_DIM), np.float32)
    for gi, w in enumerate(POOL_WINDOWS):
        left = w // 2
        right = w - 1 - left
        for i in range(blk):
            base, col = (i // GRID_W) * GRID_W, i % GRID_W
            lo, hi = max(col - left, 0), min(col + right, GRID_W - 1)
            band[gi, i, base + lo:base + hi + 1] = 1.0
            icnt[gi, i, :] = 1.0 / (hi - lo + 1)
    return jnp.asarray(band, BF16), jnp.asarray(icnt)


def _pool_mixer(u, pool_w, pool_scale):
    b, t, p = u.shape
    n_g, gd = pool_w.shape[0], pool_w.shape[1]
    rows = t // GRID_W
    blk = 4 * GRID_W
    assert t % (blk * POOL_UNROLL) == 0
    band, icnt = _pool_consts(blk)
    kern = functools.partial(_pool_kernel, rows=rows, blk=blk)
    grp = lambda i, g: (g, 0, 0)
    return pl.pallas_call(
        kern,
        grid=(b, n_g),
        in_specs=[pl.BlockSpec((1, t, gd), lambda i, g: (i, 0, g)),
                  pl.BlockSpec((1, blk, blk), grp),
                  pl.BlockSpec((1, blk, gd), grp),
                  pl.BlockSpec((1, gd, gd), grp),
                  pl.BlockSpec((1, 1, gd), grp)],
        out_specs=pl.BlockSpec((1, t, gd), lambda i, g: (i, 0, g)),
        out_shape=jax.ShapeDtypeStruct((b, t, p), BF16),
        scratch_shapes=[pltpu.VMEM((t + GRID_W, gd), F32)],
        compiler_params=_params("parallel", "parallel"),
        name="pool_mixer",
    )(u, band, icnt, pool_w.astype(BF16), pool_scale.reshape(n_g, 1, gd))


def _rms(a, g):
    return a * lax.rsqrt(jnp.mean(a * a, axis=-1, keepdims=True) + NORM_EPS) * g


def _outproj_kernel(dn_ref, pool_ref, x_ref, w1_ref, w2_ref, gt_ref, sc_ref, sh_ref,
                    gpost_ref, gpre_ref, x1_ref, h_ref):
    y = (jnp.dot(dn_ref[0], w1_ref[...], preferred_element_type=F32)
         + jnp.dot(pool_ref[0], w2_ref[...], preferred_element_type=F32))
    x1 = x_ref[0] + gt_ref[0] * _rms(y, gpost_ref[...])
    x1_ref[0] = x1
    h_ref[0] = (_rms(x1, gpre_ref[...]) * (1.0 + sc_ref[0]) + sh_ref[0]).astype(h_ref.dtype)


def _out_projection(dn, pool, x, w_out, gt1, sc2, sh2, g_post, g_pre, tm):
    b, t, d = x.shape
    wd = dn.shape[-1]
    w1 = w_out[:wd].astype(BF16)
    w2 = w_out[wd:].astype(BF16)
    row = lambda i, j: (i, j, 0)
    per_b = lambda i, j: (i, 0, 0)
    const = lambda i, j: (0, 0)
    return pl.pallas_call(
        _outproj_kernel,
        grid=(b, t // tm),
        in_specs=[pl.BlockSpec((1, tm, wd), row),
                  pl.BlockSpec((1, tm, pool.shape[-1]), row),
                  pl.BlockSpec((1, tm, d), row),
                  pl.BlockSpec(w1.shape, const),
                  pl.BlockSpec(w2.shape, const),
                  pl.BlockSpec((1, 1, d), per_b),
                  pl.BlockSpec((1, 1, d), per_b),
                  pl.BlockSpec((1, 1, d), per_b),
                  pl.BlockSpec((1, d), const),
                  pl.BlockSpec((1, d), const)],
        out_specs=[pl.BlockSpec((1, tm, d), row), pl.BlockSpec((1, tm, d), row)],
        out_shape=[jax.ShapeDtypeStruct((b, t, d), F32), jax.ShapeDtypeStruct((b, t, d), BF16)],
        compiler_params=_params("parallel", "parallel"),
        name="out_projection",
    )(dn, pool, x, w1, w2, gt1, sc2, sh2, g_post.reshape(1, d), g_pre.reshape(1, d))


FFN_HALO = 16


def _ffn_kernel(h_ref, hp_ref, hn_ref, x1_ref, wgu_ref, cw_ref, wd_ref, gt_ref, gpost_ref,
                out_ref, hext, act, *, fc, n_fc):
    j = pl.program_id(1)
    nj = pl.num_programs(1)
    tm = h_ref.shape[1]
    hal = FFN_HALO
    dff = fc * n_fc
    hext[0:hal, :] = jnp.where(j > 0, hp_ref[0], jnp.zeros_like(hp_ref[0]))
    hext[hal:hal + tm, :] = h_ref[0]
    hext[hal + tm:, :] = jnp.where(j < nj - 1, hn_ref[0], jnp.zeros_like(hn_ref[0]))
    for c in range(n_fc):
        gate = jnp.dot(hext[...], wgu_ref[:, c * fc:(c + 1) * fc], preferred_element_type=F32)
        up = jnp.dot(h_ref[0], wgu_ref[:, dff + c * fc:dff + (c + 1) * fc], preferred_element_type=F32)
        cw = cw_ref[:, c * fc:(c + 1) * fc]
        conv = (gate[hal - 1:hal - 1 + tm] * cw[0:1] + gate[hal:hal + tm] * cw[1:2]
                + gate[hal + 1:hal + 1 + tm] * cw[2:3])
        act[:, c * fc:(c + 1) * fc] = (_silu(conv) * up).astype(BF16)
    y = jnp.dot(act[...], wd_ref[...], preferred_element_type=F32)
    out_ref[0] = x1_ref[0] + gt_ref[0] * _rms(y, gpost_ref[...])


def _conv_ffn(h, x1, w_up, conv_ffn, w_down, gt2, g_post, tm, fc):
    b, t, d = x1.shape
    dff = w_down.shape[0]
    n_fc = dff // fc
    wgu = w_up.astype(BF16)
    hal = FFN_HALO
    nb = tm // hal
    last = t // hal - 1
    kern = functools.partial(_ffn_kernel, fc=fc, n_fc=n_fc)
    row = lambda i, j: (i, j, 0)
    per_b = lambda i, j: (i, 0, 0)
    const = lambda i, j: (0, 0)
    return pl.pallas_call(
        kern,
        grid=(b, t // tm),
        in_specs=[pl.BlockSpec((1, tm, d), row),
                  pl.BlockSpec((1, hal, d), lambda i, j: (i, jnp.maximum(j * nb - 1, 0), 0)),
                  pl.BlockSpec((1, hal, d), lambda i, j: (i, jnp.minimum((j + 1) * nb, last), 0)),
                  pl.BlockSpec((1, tm, d), row),
                  pl.BlockSpec(wgu.shape, const),
                  pl.BlockSpec(conv_ffn.shape, const),
                  pl.BlockSpec(w_down.shape, const),
                  pl.BlockSpec((1, 1, d), per_b),
                  pl.BlockSpec((1, d), const)],
        out_specs=pl.BlockSpec((1, tm, d), row),
        out_shape=jax.ShapeDtypeStruct((b, t, d), F32),
        scratch_shapes=[pltpu.VMEM((tm + 2 * hal, d), BF16), pltpu.VMEM((tm, dff), BF16)],
        compiler_params=_params("parallel", "parallel"),
        name="conv_ffn",
    )(h, h, h, x1, wgu, conv_ffn, w_down.astype(BF16), gt2, g_post.reshape(1, d))


def _layer(x, c, ctx, c_ctx, w_mod, b_mod, g_pre_mix, g_post_mix, g_pre_ffn, g_post_ffn,
           w_in, conv_qkv, a_log, dt_bias, o_norm, pool_w, pool_scale, w_out, w_up,
           conv_ffn, w_down):
    b, t, d = x.shape
    n_heads = a_log.shape[-1]
    w = n_heads * HEAD_DIM
    n_gate = 4 * n_heads

    rows = 16
    c_all = jnp.zeros((rows, d), F32).at[:b].set(c).at[b].set(c_ctx)
    m = _modulation(c_all, w_mod, b_mod)
    sh1, sc1, gt1, sh2, sc2, gt2 = [m[:b, None, i * d:(i + 1) * d] for i in range(N_MOD)]
    csh1 = jnp.broadcast_to(m[b, 0 * d:1 * d], (b, 1, d))
    csc1 = jnp.broadcast_to(m[b, 1 * d:2 * d], (b, 1, d))

    off_g = 4 * w
    w_gate = jnp.transpose(w_in[:, off_g:off_g + n_gate].reshape(d, 4, n_heads), (0, 2, 1)).reshape(d, n_gate)
    w_all = jnp.concatenate([w_in[:, :off_g], w_in[:, off_g + n_gate:],
                             jnp.pad(w_gate, ((0, 0), (0, LANES - n_gate)))], axis=1).astype(BF16)
    tm = min(512, t)
    qkv, z, pool_u, gcol, grow = _in_projection(x, sc1, sh1, g_pre_mix, w_all, a_log, dt_bias, n_heads, tm)
    qkv_c, _, _, gcolc, growc = _in_projection(ctx, csc1, csh1, g_pre_mix, w_all, a_log, dt_bias,
                                                n_heads, ctx.shape[1])
    dn = _delta_mixer(qkv, qkv_c, gcol, grow, gcolc, growc, z, conv_qkv, o_norm, n_heads)
    pool = _pool_mixer(pool_u, pool_w, pool_scale)
    x1, h2 = _out_projection(dn, pool, x, w_out, gt1, sc2, sh2, g_post_mix, g_pre_ffn, tm)
    return _conv_ffn(h2, x1, w_up, conv_ffn, w_down, gt2, g_post_ffn, tm, 256)


def kernel(x, c, ctx, c_ctx, w_mod, b_mod, g_pre_mix, g_post_mix, g_pre_ffn, g_post_ffn, w_in,
           conv_qkv, a_log, dt_bias, o_norm, pool_w, pool_scale, w_out, w_up, conv_ffn, w_down):
    depth = w_mod.shape[0]
    assert depth == 1, "context-stream update between layers is not implemented"
    return _layer(x, c, ctx, c_ctx, w_mod[0], b_mod[0], g_pre_mix[0], g_post_mix[0],
                  g_pre_ffn[0], g_post_ffn[0], w_in[0], conv_qkv[0], a_log[0], dt_bias[0],
                  o_norm[0], pool_w[0], pool_scale[0], w_out[0], w_up[0], conv_ffn[0], w_down[0])
```

```python
import functools

import numpy as np
import jax
import jax.numpy as jnp
from jax import lax
from jax.experimental import pallas as pl
from jax.experimental.pallas import tpu as pltpu

NORM_EPS = 1e-6
GRID_W = 64
POOL_WINDOWS = (2, 4, 8, 16)
N_MOD = 6
CHUNK = 128
HEAD_DIM = 128
VMEM_LIMIT = 56 * 1024 * 1024

F32 = jnp.float32
BF16 = jnp.bfloat16


def _silu(a):
    return a * (1.0 / (1.0 + jnp.exp(-a)))


def _dot(a, b):
    return jnp.dot(a.astype(BF16), b.astype(BF16), preferred_element_type=F32)


def _bdot(a, b):
    return jnp.dot(a, b, preferred_element_type=F32).astype(BF16)


def _split3(a):
    hi = a.astype(BF16)
    r1 = a - hi.astype(F32)
    mid = r1.astype(BF16)
    lo = (r1 - mid.astype(F32)).astype(BF16)
    return hi, mid, lo


def _dot_f32(a, b):
    ah, am, al = _split3(a)
    bh, bm, bl = _split3(b)
    d = lambda x, y: jnp.dot(x, y, preferred_element_type=F32)
    return ((d(al, bh) + d(ah, bl)) + d(am, bm)) + (d(am, bh) + d(ah, bm)) + d(ah, bh)


def _params(*sem):
    return pltpu.CompilerParams(dimension_semantics=sem, vmem_limit_bytes=VMEM_LIMIT)


def _mod_kernel(c_ref, w_ref, b_ref, o_ref):
    o_ref[...] = _dot_f32(_silu(c_ref[...]), w_ref[...]) + b_ref[...]


def _modulation(c_all, w_mod, b_mod):
    r, d = c_all.shape
    n = w_mod.shape[1]
    tn = 512
    return pl.pallas_call(
        _mod_kernel,
        grid=(n // tn,),
        in_specs=[pl.BlockSpec((r, d), lambda j: (0, 0)),
                  pl.BlockSpec((d, tn), lambda j: (0, j)),
                  pl.BlockSpec((1, tn), lambda j: (0, j))],
        out_specs=pl.BlockSpec((r, tn), lambda j: (0, j)),
        out_shape=jax.ShapeDtypeStruct((r, n), F32),
        compiler_params=_params("parallel"),
        name="modulation",
    )(c_all, w_mod, b_mod.reshape(1, n))


LANES = 128


def _inproj_kernel(x_ref, sc_ref, sh_ref, g_ref, w_ref, alog_ref, dtb_ref, tri_ref,
                   qkv_ref, z_ref, pool_ref, gcol_ref, grow_ref, *, n_qkv, n_z, n_pool, n_heads):
    x = x_ref[0]
    xn = x * lax.rsqrt(jnp.mean(x * x, axis=-1, keepdims=True) + NORM_EPS) * g_ref[...]
    h = (xn * (1.0 + sc_ref[0]) + sh_ref[0]).astype(BF16)
    o1 = n_qkv
    o2 = o1 + n_z
    o3 = o2 + n_pool
    pg = jnp.dot(h, w_ref[:, o3:], preferred_element_type=F32)
    qkv_ref[0] = jnp.dot(h, w_ref[:, 0:o1], preferred_element_type=F32).astype(qkv_ref.dtype)
    beta = 1.0 / (1.0 + jnp.exp(-pg))
    a = pg + dtb_ref[...]
    softplus = jnp.maximum(a, 0.0) + jnp.log(1.0 + jnp.exp(-jnp.abs(a)))
    g = -jnp.exp(alog_ref[...]) * softplus
    tm = g.shape[0]
    n_ch = tm // CHUNK
    rows = 4 * n_heads
    beta_t = [beta[c * CHUNK:(c + 1) * CHUNK, :].T[0:rows, :] for c in range(n_ch)]
    g_t = [g[c * CHUNK:(c + 1) * CHUNK, :].T[0:rows, :] for c in range(n_ch)]
    parts = _split3(jnp.concatenate(g_t, axis=0))
    r = jnp.dot(jnp.concatenate(parts, axis=0), tri_ref[...], preferred_element_type=F32)
    z_ref[0] = jnp.dot(h, w_ref[:, o1:o2], preferred_element_type=F32).astype(z_ref.dtype)
    pool_ref[0] = jnp.dot(h, w_ref[:, o2:o3], preferred_element_type=F32).astype(pool_ref.dtype)
    m = n_ch * rows
    pre_all = r[0:m] + r[m:2 * m] + r[2 * m:3 * m]
    kind = lax.broadcasted_iota(jnp.int32, (rows, CHUNK), 0) & 3
    zpad = jnp.zeros((LANES - rows, CHUNK), F32)
    for c in range(n_ch):
        pre = pre_all[c * rows:(c + 1) * rows, :]
        suf = pre[:, CHUNK - 1:CHUNK] - pre + g_t[c]
        row = jnp.where(kind < 2, beta_t[c], jnp.where(kind == 2, pre, suf))
        grow_ref[0, :, c * CHUNK:(c + 1) * CHUNK] = row
        gcol_ref[0, c * CHUNK:(c + 1) * CHUNK, :] = jnp.concatenate([row, zpad], axis=0).T


def _in_projection(x, sc, sh, g_pre, w_all, a_log, dt_bias, n_heads, tm):
    b, t, d = x.shape
    w = n_heads * HEAD_DIM
    n_qkv, n_z = 3 * w, w
    n_pool = w_all.shape[1] - n_qkv - n_z - LANES
    assert CHUNK == LANES
    tri = jnp.asarray(np.triu(np.ones((CHUNK, CHUNK), np.float32)), BF16)
    lane_par = jnp.zeros((2, n_heads, 4), F32)
    lane_par = lane_par.at[0, :, 2:].set(a_log.T).at[1, :, 2:].set(dt_bias.T)
    lane_par = jnp.pad(lane_par.reshape(2, 4 * n_heads), ((0, 0), (0, LANES - 4 * n_heads)))
    kern = functools.partial(_inproj_kernel, n_qkv=n_qkv, n_z=n_z, n_pool=n_pool, n_heads=n_heads)
    row = lambda i, j: (i, j, 0)
    per_b = lambda i, j: (i, 0, 0)
    const = lambda i, j: (0, 0)
    return pl.pallas_call(
        kern,
        grid=(b, t // tm),
        in_specs=[pl.BlockSpec((1, tm, d), row),
                  pl.BlockSpec((1, 1, d), per_b),
                  pl.BlockSpec((1, 1, d), per_b),
                  pl.BlockSpec((1, d), const),
                  pl.BlockSpec(w_all.shape, const),
                  pl.BlockSpec((1, LANES), const),
                  pl.BlockSpec((1, LANES), const),
                  pl.BlockSpec((CHUNK, CHUNK), const)],
        out_specs=[pl.BlockSpec((1, tm, n_qkv), row),
                   pl.BlockSpec((1, tm, n_z), row),
                   pl.BlockSpec((1, tm, n_pool), row),
                   pl.BlockSpec((1, tm, LANES), row),
                   pl.BlockSpec((1, 4 * n_heads, tm), lambda i, j: (i, 0, j))],
        out_shape=[jax.ShapeDtypeStruct((b, t, n_qkv), BF16),
                   jax.ShapeDtypeStruct((b, t, n_z), BF16),
                   jax.ShapeDtypeStruct((b, t, n_pool), BF16),
                   jax.ShapeDtypeStruct((b, t, LANES), F32),
                   jax.ShapeDtypeStruct((b, 4 * n_heads, t), F32)],
        compiler_params=_params("parallel", "parallel"),
        name="in_projection",
    )(x, sc, sh, g_pre.reshape(1, d), w_all, lane_par[0:1], lane_par[1:2], tri)


CONV_HALO = 16


SUBLANES = 8


def _conv_silu(ref, cw_ref, win, c, n_chunks, post=None):
    hal = CONV_HALO
    r0 = pl.multiple_of(c * CHUNK, CHUNK)
    p0 = pl.multiple_of(jnp.maximum(r0 - hal, 0), hal)
    n0 = pl.multiple_of(jnp.minimum(r0 + CHUNK, n_chunks * CHUNK - hal), hal)
    win[0:hal, :] = jnp.where(c > 0, ref[0, pl.ds(p0, hal), :].astype(F32), 0.0)
    win[hal:hal + CHUNK, :] = ref[0, pl.ds(r0, CHUNK), :].astype(F32)
    win[hal + CHUNK:, :] = jnp.where(c < n_chunks - 1, ref[0, pl.ds(n0, hal), :].astype(F32), 0.0)
    width = cw_ref.shape[0]
    pad = width // 2
    n_t = CHUNK // SUBLANES
    taps = [cw_ref[j:j + 1, :] for j in range(width)]
    xs = [win[pl.ds(hal - pad + s, n_t, stride=SUBLANES), :] for s in range(SUBLANES + width - 1)]
    outs = []
    for r in range(SUBLANES):
        acc = xs[r] * taps[0]
        for j in range(1, width):
            acc = acc + xs[r + j] * taps[j]
        acc = _silu(acc)
        outs.append(acc if post is None else post(acc))
    for r in range(SUBLANES):
        win[pl.ds(hal + r, n_t, stride=SUBLANES), :] = outs[r]
    return win[hal:hal + CHUNK, :]


def _l2n(a):
    return a * lax.rsqrt(jnp.sum(a * a, axis=-1, keepdims=True) + NORM_EPS)


INV_BASE = 8
LOCAL_UNROLL = 4
HEAD_PAIR = 2


def _inv_unit(mats):
    n = mats[0].shape[0]
    ii = lax.broadcasted_iota(jnp.int32, (n, n), 0)
    jj = lax.broadcasted_iota(jnp.int32, (n, n), 1)

    def same_block(bits):
        return lax.shift_right_logical(ii, bits) == lax.shift_right_logical(jj, bits)

    bits = int(np.log2(INV_BASE))
    base = same_block(bits)
    eye = jnp.where(ii == jj, 1.0, 0.0).astype(BF16)
    zero = jnp.zeros((), BF16)
    abf = [a.astype(BF16) for a in mats]
    ms = [jnp.where(base, -a, zero) for a in abf]
    ps = [eye + m for m in ms]
    ms = [_bdot(m, m) for m in ms]
    for _ in range(bits - 2):
        pm = [_bdot(jnp.concatenate([p, m], axis=0), m) for p, m in zip(ps, ms)]
        ps = [p + x[:n] for p, x in zip(ps, pm)]
        ms = [x[n:] for x in pm]
    ps = [p + _bdot(p, m) for p, m in zip(ps, ms)]
    while (1 << bits) < n:
        lvl = same_block(bits + 1) & jnp.logical_not(same_block(bits))
        tmp = [_bdot(jnp.where(lvl, a, zero), p) for a, p in zip(abf, ps)]
        ps = [p - _bdot(p, t) for p, t in zip(ps, tmp)]
        bits += 1
    return ps


def _chain_operands(q, k, kt, v, gcol, grow, d):
    c = k.shape[0]
    ii = lax.broadcasted_iota(jnp.int32, (c, c), 0)
    jj = lax.broadcasted_iota(jnp.int32, (c, c), 1)
    incl = (ii >= jj) if d == 0 else (ii <= jj)
    strict = (ii > jj) if d == 0 else (ii < jj)
    beta_c = gcol[:, d:d + 1]
    gc_c = gcol[:, 2 + d:3 + d]
    gc_r = grow[2 + d:3 + d, :]
    glast = gc_r[:, c - 1:c] if d == 0 else gc_r[:, 0:1]
    decay = jnp.where(incl, jnp.exp(jnp.where(incl, gc_c - gc_r, 0.0)), 0.0)
    e_c = jnp.exp(gc_c)
    db = jnp.where(strict, decay * beta_c, 0.0).astype(BF16)
    rhs = jnp.concatenate([k * (beta_c * e_c), v * beta_c], axis=1).astype(BF16)
    kdt = (kt * jnp.exp(glast - gc_r)).astype(BF16)
    if q is None:
        return db, None, rhs, kdt, None
    return db, decay.astype(BF16), rhs, kdt, (q * e_c).astype(BF16)


def _chain_solve(chains):
    hd = HEAD_DIM
    mats = [(kk * db.astype(F32)).astype(BF16) for kk, _, db, _, _, _, _ in chains]
    ts = _inv_unit(mats)
    wus = [_dot(t, ch[4]) for t, ch in zip(ts, chains)]
    out = []
    for (kk, qk, db, di, rhs, kdt, qe), wu in zip(chains, wus):
        if qk is None:
            kn_ = _dot(kdt, wu)
            out.append((kn_[:, :hd], kn_[:, hd:], None, None))
        else:
            attn = (qk * di.astype(F32)).astype(BF16)
            x = _dot(jnp.concatenate([kdt, attn], axis=0), wu)
            out.append((x[:hd, :hd], x[:hd, hd:], qe.astype(F32) - x[hd:, :hd], x[hd:, hd:]))
    return out


def _delta_kernel(q_ref, k_ref, v_ref, kc_ref, vc_ref, kc2_ref, vc2_ref, cwq_ref, cwk_ref, cwv_ref,
                  cwk2_ref, cwv2_ref,
                  gcol_ref, grow_ref, gcolc_ref, growc_ref, z_ref, onorm_ref,
                  out_ref, kq_scr, n_scr, o_scr, s_ref, win_scr, stm_a, stg_a, str_a, stm_b, stg_b, str_b,
                  *, n_x, n_c):
    n_tot = n_x + n_c
    hd = HEAD_DIM
    head_shift = lax.rem(LANES - 4 * pl.program_id(1), LANES)
    par = lax.rem(pl.program_id(1), HEAD_PAIR)

    def prep(src_q, src_k, src_v, gcol_r, grow_r, items, n_seq, stage, cwk=cwk_ref, cwv=cwv_ref):
        st_m, st_g, st_r = stage
        for sl, c, pp, shift in items:
            r0 = pl.multiple_of(c * CHUNK, CHUNK)
            k = _conv_silu(src_k, cwk, win_scr.at[3 * sl], c, n_seq, _l2n)
            v = _conv_silu(src_v, cwv, win_scr.at[3 * sl + 1], c, n_seq)
            q = None if src_q is None else _conv_silu(src_q, cwq_ref, win_scr.at[3 * sl + 2], c, n_seq,
                                                      lambda a: _l2n(a) * (hd ** -0.5))
            kt = k.T
            st_m[sl, 0] = k.astype(BF16)
            st_m[sl, 1] = kt.astype(BF16)
            if q is not None:
                st_m[sl, 2] = q.astype(BF16)
            gcol = pltpu.roll(gcol_r[0, pl.ds(r0, CHUNK), :], shift, axis=1)[:, 0:4]
            grow = grow_r[0, pp, c]
            for d in (0, 1):
                db, di, rhs, kdt, qe = _chain_operands(q, k, kt, v, gcol, grow, d)
                st_g[sl, d, 0] = db
                st_g[sl, d, 1] = kdt
                st_r[sl, d] = rhs
                if q is not None:
                    st_g[sl, d, 2] = di
                    st_g[sl, d, 3] = qe

    def local(stage, with_q, items):
        st_m, st_g, st_r = stage
        chains = []
        for sl, c, pp, _ in items:
            if with_q:
                kq = jnp.dot(jnp.concatenate([st_m[sl, 0], st_m[sl, 2]], axis=0), st_m[sl, 1],
                             preferred_element_type=F32)
                kk, qk = kq[:CHUNK], kq[CHUNK:]
            else:
                kk, qk = jnp.dot(st_m[sl, 0], st_m[sl, 1], preferred_element_type=F32), None
            for d in (0, 1):
                chains.append((kk, qk, st_g[sl, d, 0], st_g[sl, d, 2] if with_q else None, st_r[sl, d],
                               st_g[sl, d, 1], st_g[sl, d, 3] if with_q else None))
        res = _chain_solve(chains)
        for j, (_, c, pp, out_slot) in enumerate(items):
            r0 = pl.multiple_of(c * CHUNK, CHUNK)
            for d in (0, 1):
                kmat, nmat, qmat, omat = res[2 * j + d]
                idx = (pp * 2 + d) * n_tot + out_slot
                kq_scr[idx, 0:hd, :] = kmat.astype(BF16)
                n_scr[idx] = nmat.astype(BF16)
                if qmat is not None:
                    kq_scr[idx, hd:, :] = qmat.astype(BF16)
                    o_scr[pp * 2 + d, pl.ds(r0, CHUNK), :] = omat

    u_c = min(LOCAL_UNROLL, n_c)
    u_x = min(LOCAL_UNROLL, n_x)
    g_c, g_x = n_c // u_c, n_x // u_x
    stages = ((stm_a, stg_a, str_a), (stm_b, stg_b, str_b))

    def prep_x(g, stage):
        prep(q_ref, k_ref, v_ref, gcol_ref, grow_ref,
             [(j, g * u_x + j, par, head_shift) for j in range(u_x)], n_x, stage)

    def local_x(g, stage):
        local(stage, True, [(j, g * u_x + j, par, n_c + g * u_x + j) for j in range(u_x)])

    if g_c == 1 and HEAD_PAIR * n_c <= LOCAL_UNROLL:
        @pl.when(par == 0)
        def _context_pair():
            items = []
            for pp in range(HEAD_PAIR):
                src = (kc_ref, vc_ref, cwk_ref, cwv_ref) if pp == 0 else (kc2_ref, vc2_ref, cwk2_ref, cwv2_ref)
                shift = lax.rem(LANES - 4 * (pl.program_id(1) + pp), LANES)
                prep(None, src[0], src[1], gcolc_ref, growc_ref,
                     [(pp * n_c + j, j, pp, shift) for j in range(n_c)], n_c, stages[0], src[2], src[3])
                items += [(pp * n_c + j, j, pp, j) for j in range(n_c)]
            local(stages[0], False, items)
            prep_x(0, stages[1])

        @pl.when(par != 0)
        def _latent_only():
            prep_x(0, stages[1])
    else:
        def prep_c(g, stage):
            prep(None, kc_ref, vc_ref, gcolc_ref, growc_ref,
                 [(j, g * u_c + j, par, head_shift) for j in range(u_c)], n_c, stage)

        prep_c(0, stages[0])
        for g in range(g_c):
            local(stages[g % 2], False, [(j, g * u_c + j, par, g * u_c + j) for j in range(u_c)])
            if g + 1 < g_c:
                prep_c(g + 1, stages[(g + 1) % 2])
            else:
                prep_x(0, stages[(g + 1) % 2])

    if g_x % 2 == 0:
        def local_pair(p, carry):
            for h in (0, 1):
                g = 2 * p + h
                prep_x(jnp.minimum(g + 1, g_x - 1), stages[(g_c + h + 1) % 2])
                local_x(g, stages[(g_c + h) % 2])
            return carry

        lax.fori_loop(0, g_x // 2, local_pair, 0)
    else:
        for g in range(g_x):
            local_x(g, stages[(g_c + g) % 2])
            if g + 1 < g_x:
                prep_x(g + 1, stages[(g_c + g + 1) % 2])
    @pl.when(par == HEAD_PAIR - 1)
    def _sequential():
        s_ref[...] = jnp.zeros_like(s_ref)
        chains = [(pp, d) for pp in range(HEAD_PAIR) for d in (0, 1)]

        def state_decay(grow, d):
            tot = grow[2 + d:3 + d, CHUNK - 1:CHUNK] if d == 0 else grow[2 + d:3 + d, 0:1]
            return jnp.exp(tot)

        def seq_ctx(i, carry):
            for pp, d in chains:
                c = i if d == 0 else n_c - 1 - i
                idx = (pp * 2 + d) * n_tot + c
                s = s_ref[pp * 2 + d]
                ks = jnp.dot(kq_scr[idx, 0:hd, :], s.astype(BF16), preferred_element_type=F32)
                s_ref[pp * 2 + d] = (s * state_decay(growc_ref[0, pp, c], d) - ks
                                     + n_scr[idx].astype(F32))
            return carry

        def finish(pp, r0, o):
            lanes = slice(pp * hd, (pp + 1) * hd)
            on = o * lax.rsqrt(jnp.mean(o * o, axis=-1, keepdims=True) + NORM_EPS) * onorm_ref[...]
            zg = _silu(z_ref[0, pl.ds(r0, CHUNK), lanes].astype(F32))
            out_ref[0, pl.ds(r0, CHUNK), lanes] = (on * zg).astype(out_ref.dtype)

        def seq_x(i, carry, second_visit):
            for pp, d in chains:
                c = i if d == 0 else n_x - 1 - i
                r0 = pl.multiple_of(c * CHUNK, CHUNK)
                idx = (pp * 2 + d) * n_tot + n_c + c
                s = s_ref[pp * 2 + d]
                kqs = jnp.dot(kq_scr[idx], s.astype(BF16), preferred_element_type=F32)
                o = o_scr[pp * 2 + d, pl.ds(r0, CHUNK), :] + kqs[hd:, :]
                if second_visit:
                    finish(pp, r0, o + o_scr[pp * 2 + 1 - d, pl.ds(r0, CHUNK), :])
                else:
                    o_scr[pp * 2 + d, pl.ds(r0, CHUNK), :] = o
                s_ref[pp * 2 + d] = (s * state_decay(grow_ref[0, pp, c], d) - kqs[0:hd, :]
                                     + n_scr[idx].astype(F32))
            return carry

        lax.fori_loop(0, n_c, seq_ctx, 0)
        half = (n_x + 1) // 2
        lax.fori_loop(0, half, functools.partial(seq_x, second_visit=False), 0)
        if n_x % 2:
            r_mid = (half - 1) * CHUNK
            for pp in range(HEAD_PAIR):
                finish(pp, r_mid, o_scr[pp * 2, r_mid:r_mid + CHUNK, :] + o_scr[pp * 2 + 1, r_mid:r_mid + CHUNK, :])
        lax.fori_loop(half, n_x, functools.partial(seq_x, second_visit=True), 0)


def _row_gates(grow, n_heads):
    b, _, t = grow.shape
    return jnp.transpose(grow.reshape(b, n_heads, 4, t // CHUNK, CHUNK), (0, 1, 3, 2, 4))


def _delta_mixer(qkv, qkv_c, gcol, grow, gcolc, growc, z, conv_qkv, o_norm, n_heads):
    b, t, _ = qkv.shape
    tc = qkv_c.shape[1]
    n_x, n_c = t // CHUNK, tc // CHUNK
    hd = HEAD_DIM
    assert CHUNK == hd and t % CHUNK == 0 and tc % CHUNK == 0
    width = conv_qkv.shape[0]
    grow = _row_gates(grow, n_heads)
    growc = _row_gates(growc, n_heads)
    kern = functools.partial(_delta_kernel, n_x=n_x, n_c=n_c)
    nh = n_heads

    def colblk(off, ahead=0):
        return lambda i, h: (i, 0, off + jnp.minimum(h + ahead, nh - 1))

    def cwblk(off, ahead=0):
        return lambda i, h: (0, off + jnp.minimum(h + ahead, nh - 1))

    assert nh % HEAD_PAIR == 0
    pair5 = lambda i, h: (i, h // HEAD_PAIR, 0, 0, 0)
    pair3 = lambda i, h: (i, 0, h // HEAD_PAIR)
    return pl.pallas_call(
        kern,
        grid=(b, nh),
        in_specs=[pl.BlockSpec((1, t, hd), colblk(0)),
                  pl.BlockSpec((1, t, hd), colblk(nh)),
                  pl.BlockSpec((1, t, hd), colblk(2 * nh)),
                  pl.BlockSpec((1, tc, hd), colblk(nh)),
                  pl.BlockSpec((1, tc, hd), colblk(2 * nh)),
                  pl.BlockSpec((1, tc, hd), colblk(nh, 1)),
                  pl.BlockSpec((1, tc, hd), colblk(2 * nh, 1)),
                  pl.BlockSpec((width, hd), cwblk(0)),
                  pl.BlockSpec((width, hd), cwblk(nh)),
                  pl.BlockSpec((width, hd), cwblk(2 * nh)),
                  pl.BlockSpec((width, hd), cwblk(nh, 1)),
                  pl.BlockSpec((width, hd), cwblk(2 * nh, 1)),
                  pl.BlockSpec((1, t, LANES), lambda i, h: (i, 0, 0)),
                  pl.BlockSpec((1, HEAD_PAIR, n_x, 4, CHUNK), pair5),
                  pl.BlockSpec((1, tc, LANES), lambda i, h: (i, 0, 0)),
                  pl.BlockSpec((1, HEAD_PAIR, n_c, 4, CHUNK), pair5),
                  pl.BlockSpec((1, t, HEAD_PAIR * hd), pair3),
                  pl.BlockSpec((1, hd), lambda i, h: (0, 0))],
        out_specs=pl.BlockSpec((1, t, HEAD_PAIR * hd), pair3),
        out_shape=jax.ShapeDtypeStruct((b, t, nh * hd), BF16),
        scratch_shapes=[pltpu.VMEM((2 * HEAD_PAIR * (n_x + n_c), 2 * hd, hd), BF16),
                        pltpu.VMEM((2 * HEAD_PAIR * (n_x + n_c), hd, hd), BF16),
                        pltpu.VMEM((2 * HEAD_PAIR, t, hd), F32),
                        pltpu.VMEM((2 * HEAD_PAIR, hd, hd), F32),
                        pltpu.VMEM((3 * LOCAL_UNROLL, CHUNK + 2 * CONV_HALO, hd), F32),
                        pltpu.VMEM((LOCAL_UNROLL, 3, CHUNK, hd), BF16),
                        pltpu.VMEM((LOCAL_UNROLL, 2, 4, CHUNK, hd), BF16),
                        pltpu.VMEM((LOCAL_UNROLL, 2, CHUNK, 2 * hd), BF16),
                        pltpu.VMEM((LOCAL_UNROLL, 3, CHUNK, hd), BF16),
                        pltpu.VMEM((LOCAL_UNROLL, 2, 4, CHUNK, hd), BF16),
                        pltpu.VMEM((LOCAL_UNROLL, 2, CHUNK, 2 * hd), BF16)],
        compiler_params=_params("parallel", "arbitrary"),
        name="delta_mixer",
    )(qkv, qkv, qkv, qkv_c, qkv_c, qkv_c, qkv_c, conv_qkv, conv_qkv, conv_qkv, conv_qkv, conv_qkv,
      gcol, grow, gcolc, growc, z, o_norm.reshape(1, hd))


POOL_UNROLL = 4


def _pool_kernel(u_ref, band_ref, icnt_ref, w_ref, sc_ref, out_ref, cs, *, rows, blk):
    g = pl.program_id(1)
    left = lax.shift_left(jnp.int32(1), g)
    right = left - 1
    gw = GRID_W
    gd = cs.shape[1]
    t = rows * gw

    cs[0:gw, :] = jnp.zeros((gw, gd), F32)

    def prefix(r, acc):
        acc = acc + u_ref[0, pl.ds(pl.multiple_of(r * gw, gw), gw), :].astype(F32)
        cs[pl.ds(pl.multiple_of((r + 1) * gw, gw), gw), :] = acc
        return acc

    lax.fori_loop(0, rows, prefix, jnp.zeros((gw, gd), F32))

    def row_mean(r):
        lo = jnp.maximum(r - left, 0)
        hi = jnp.minimum(r + right + 1, rows)
        tot = cs[pl.ds(pl.multiple_of(hi * gw, gw), gw), :] - cs[pl.ds(pl.multiple_of(lo * gw, gw), gw), :]
        return tot / jnp.full((gw, gd), hi - lo, jnp.int32).astype(F32)

    def col_body(i, carry):
        blocks = [i * POOL_UNROLL + j for j in range(POOL_UNROLL)]
        starts = [pl.multiple_of(bi * blk, blk) for bi in blocks]
        rs = [jnp.concatenate([row_mean(bi * (blk // gw) + rr) for rr in range(blk // gw)], axis=0)
              for bi in blocks]
        ms = [jnp.dot(band_ref[0], r.astype(BF16), preferred_element_type=F32) for r in rs]
        ds = [(m * icnt_ref[0] - u_ref[0, pl.ds(r0, blk), :].astype(F32)).astype(BF16)
              for m, r0 in zip(ms, starts)]
        ys = [jnp.dot(dlt, w_ref[0], preferred_element_type=F32) for dlt in ds]
        for y, r0 in zip(ys, starts):
            out_ref[0, pl.ds(r0, blk), :] = (y * sc_ref[0]).astype(out_ref.dtype)
        return carry

    lax.fori_loop(0, t // (blk * POOL_UNROLL), col_body, 0)


def _pool_consts(blk):
    band = np.zeros((len(POOL_WINDOWS), blk, blk), np.float32)
    icnt = np.zeros((len(POOL_WINDOWS), blk, HEAD_DIM), np.float32)
    for gi, w in enumerate(POOL_WINDOWS):
        left = w // 2
        right = w - 1 - left
        for i in range(blk):
            base, col = (i // GRID_W) * GRID_W, i % GRID_W
            lo, hi = max(col - left, 0), min(col + right, GRID_W - 1)
            band[gi, i, base + lo:base + hi + 1] = 1.0
            icnt[gi, i, :] = 1.0 / (hi - lo + 1)
    return jnp.asarray(band, BF16), jnp.asarray(icnt)


def _pool_mixer(u, pool_w, pool_scale):
    b, t, p = u.shape
    n_g, gd = pool_w.shape[0], pool_w.shape[1]
    rows = t // GRID_W
    blk = 4 * GRID_W
    assert t % (blk * POOL_UNROLL) == 0
    band, icnt = _pool_consts(blk)
    kern = functools.partial(_pool_kernel, rows=rows, blk=blk)
    grp = lambda i, g: (g, 0, 0)
    return pl.pallas_call(
        kern,
        grid=(b, n_g),
        in_specs=[pl.BlockSpec((1, t, gd), lambda i, g: (i, 0, g)),
                  pl.BlockSpec((1, blk, blk), grp),
                  pl.BlockSpec((1, blk, gd), grp),
                  pl.BlockSpec((1, gd, gd), grp),
                  pl.BlockSpec((1, 1, gd), grp)],
        out_specs=pl.BlockSpec((1, t, gd), lambda i, g: (i, 0, g)),
        out_shape=jax.ShapeDtypeStruct((b, t, p), BF16),
        scratch_shapes=[pltpu.VMEM((t + GRID_W, gd), F32)],
        compiler_params=_params("parallel", "parallel"),
        name="pool_mixer",
    )(u, band, icnt, pool_w.astype(BF16), pool_scale.reshape(n_g, 1, gd))


def _rms(a, g):
    return a * lax.rsqrt(jnp.mean(a * a, axis=-1, keepdims=True) + NORM_EPS) * g


def _outproj_kernel(dn_ref, pool_ref, x_ref, w1_ref, w2_ref, gt_ref, sc_ref, sh_ref,
                    gpost_ref, gpre_ref, x1_ref, h_ref):
    y = (jnp.dot(dn_ref[0], w1_ref[...], preferred_element_type=F32)
         + jnp.dot(pool_ref[0], w2_ref[...], preferred_element_type=F32))
    x1 = x_ref[0] + gt_ref[0] * _rms(y, gpost_ref[...])
    x1_ref[0] = x1
    h_ref[0] = (_rms(x1, gpre_ref[...]) * (1.0 + sc_ref[0]) + sh_ref[0]).astype(h_ref.dtype)


def _out_projection(dn, pool, x, w_out, gt1, sc2, sh2, g_post, g_pre, tm):
    b, t, d = x.shape
    wd = dn.shape[-1]
    w1 = w_out[:wd].astype(BF16)
    w2 = w_out[wd:].astype(BF16)
    row = lambda i, j: (i, j, 0)
    per_b = lambda i, j: (i, 0, 0)
    const = lambda i, j: (0, 0)
    return pl.pallas_call(
        _outproj_kernel,
        grid=(b, t // tm),
        in_specs=[pl.BlockSpec((1, tm, wd), row),
                  pl.BlockSpec((1, tm, pool.shape[-1]), row),
                  pl.BlockSpec((1, tm, d), row),
                  pl.BlockSpec(w1.shape, const),
                  pl.BlockSpec(w2.shape, const),
                  pl.BlockSpec((1, 1, d), per_b),
                  pl.BlockSpec((1, 1, d), per_b),
                  pl.BlockSpec((1, 1, d), per_b),
                  pl.BlockSpec((1, d), const),
                  pl.BlockSpec((1, d), const)],
        out_specs=[pl.BlockSpec((1, tm, d), row), pl.BlockSpec((1, tm, d), row)],
        out_shape=[jax.ShapeDtypeStruct((b, t, d), F32), jax.ShapeDtypeStruct((b, t, d), BF16)],
        compiler_params=_params("parallel", "parallel"),
        name="out_projection",
    )(dn, pool, x, w1, w2, gt1, sc2, sh2, g_post.reshape(1, d), g_pre.reshape(1, d))


FFN_HALO = 16


def _ffn_kernel(h_ref, hp_ref, hn_ref, x1_ref, wgu_ref, cw_ref, wd_ref, gt_ref, gpost_ref,
                out_ref, hext, act, *, fc, n_fc):
    j = pl.program_id(1)
    nj = pl.num_programs(1)
    tm = h_ref.shape[1]
    hal = FFN_HALO
    dff = fc * n_fc
    hext[0:hal, :] = jnp.where(j > 0, hp_ref[0], jnp.zeros_like(hp_ref[0]))
    hext[hal:hal + tm, :] = h_ref[0]
    hext[hal + tm:, :] = jnp.where(j < nj - 1, hn_ref[0], jnp.zeros_like(hn_ref[0]))
    for c in range(n_fc):
        gate = jnp.dot(hext[...], wgu_ref[:, c * fc:(c + 1) * fc], preferred_element_type=F32)
        up = jnp.dot(h_ref[0], wgu_ref[:, dff + c * fc:dff + (c + 1) * fc], preferred_element_type=F32)
        cw = cw_ref[:, c * fc:(c + 1) * fc]
        conv = (gate[hal - 1:hal - 1 + tm] * cw[0:1] + gate[hal:hal + tm] * cw[1:2]
                + gate[hal + 1:hal + 1 + tm] * cw[2:3])
        act[:, c * fc:(c + 1) * fc] = (_silu(conv) * up).astype(BF16)
    y = jnp.dot(act[...], wd_ref[...], preferred_element_type=F32)
    out_ref[0] = x1_ref[0] + gt_ref[0] * _rms(y, gpost_ref[...])


def _conv_ffn(h, x1, w_up, conv_ffn, w_down, gt2, g_post, tm, fc):
    b, t, d = x1.shape
    dff = w_down.shape[0]
    n_fc = dff // fc
    wgu = w_up.astype(BF16)
    hal = FFN_HALO
    nb = tm // hal
    last = t // hal - 1
    kern = functools.partial(_ffn_kernel, fc=fc, n_fc=n_fc)
    row = lambda i, j: (i, j, 0)
    per_b = lambda i, j: (i, 0, 0)
    const = lambda i, j: (0, 0)
    return pl.pallas_call(
        kern,
        grid=(b, t // tm),
        in_specs=[pl.BlockSpec((1, tm, d), row),
                  pl.BlockSpec((1, hal, d), lambda i, j: (i, jnp.maximum(j * nb - 1, 0), 0)),
                  pl.BlockSpec((1, hal, d), lambda i, j: (i, jnp.minimum((j + 1) * nb, last), 0)),
                  pl.BlockSpec((1, tm, d), row),
                  pl.BlockSpec(wgu.shape, const),
                  pl.BlockSpec(conv_ffn.shape, const),
                  pl.BlockSpec(w_down.shape, const),
                  pl.BlockSpec((1, 1, d), per_b),
                  pl.BlockSpec((1, d), const)],
        out_specs=pl.BlockSpec((1, tm, d), row),
        out_shape=jax.ShapeDtypeStruct((b, t, d), F32),
        scratch_shapes=[pltpu.VMEM((tm + 2 * hal, d), BF16), pltpu.VMEM((tm, dff), BF16)],
        compiler_params=_params("parallel", "parallel"),
        name="conv_ffn",
    )(h, h, h, x1, wgu, conv_ffn, w_down.astype(BF16), gt2, g_post.reshape(1, d))


def _layer(x, c, ctx, c_ctx, w_mod, b_mod, g_pre_mix, g_post_mix, g_pre_ffn, g_post_ffn,
           w_in, conv_qkv, a_log, dt_bias, o_norm, pool_w, pool_scale, w_out, w_up,
           conv_ffn, w_down):
    b, t, d = x.shape
    n_heads = a_log.shape[-1]
    w = n_heads * HEAD_DIM
    n_gate = 4 * n_heads

    rows = 16
    c_all = jnp.zeros((rows, d), F32).at[:b].set(c).at[b].set(c_ctx)
    m = _modulation(c_all, w_mod, b_mod)
    sh1, sc1, gt1, sh2, sc2, gt2 = [m[:b, None, i * d:(i + 1) * d] for i in range(N_MOD)]
    csh1 = jnp.broadcast_to(m[b, 0 * d:1 * d], (b, 1, d))
    csc1 = jnp.broadcast_to(m[b, 1 * d:2 * d], (b, 1, d))

    off_g = 4 * w
    w_gate = jnp.transpose(w_in[:, off_g:off_g + n_gate].reshape(d, 4, n_heads), (0, 2, 1)).reshape(d, n_gate)
    w_all = jnp.concatenate([w_in[:, :off_g], w_in[:, off_g + n_gate:],
                             jnp.pad(w_gate, ((0, 0), (0, LANES - n_gate)))], axis=1).astype(BF16)
    tm = min(512, t)
    qkv, z, pool_u, gcol, grow = _in_projection(x, sc1, sh1, g_pre_mix, w_all, a_log, dt_bias, n_heads, tm)
    qkv_c, _, _, gcolc, growc = _in_projection(ctx, csc1, csh1, g_pre_mix, w_all, a_log, dt_bias,
                                                n_heads, ctx.shape[1])
    dn = _delta_mixer(qkv, qkv_c, gcol, grow, gcolc, growc, z, conv_qkv, o_norm, n_heads)
    pool = _pool_mixer(pool_u, pool_w, pool_scale)
    x1, h2 = _out_projection(dn, pool, x, w_out, gt1, sc2, sh2, g_post_mix, g_pre_ffn, tm)
    return _conv_ffn(h2, x1, w_up, conv_ffn, w_down, gt2, g_post_ffn, tm, 256)


def kernel(x, c, ctx, c_ctx, w_mod, b_mod, g_pre_mix, g_post_mix, g_pre_ffn, g_post_ffn, w_in,
           conv_qkv, a_log, dt_bias, o_norm, pool_w, pool_scale, w_out, w_up, conv_ffn, w_down):
    depth = w_mod.shape[0]
    assert depth == 1, "context-stream update between layers is not implemented"
    return _layer(x, c, ctx, c_ctx, w_mod[0], b_mod[0], g_pre_mix[0], g_post_mix[0],
                  g_pre_ffn[0], g_post_ffn[0], w_in[0], conv_qkv[0], a_log[0], dt_bias[0],
                  o_norm[0], pool_w[0], pool_scale[0], w_out[0], w_up[0], conv_ffn[0], w_down[0])
```

```python
import functools

import numpy as np
import jax
import jax.numpy as jnp
from jax import lax
from jax.experimental import pallas as pl
from jax.experimental.pallas import tpu as pltpu

NORM_EPS = 1e-6
GRID_W = 64
POOL_WINDOWS = (2, 4, 8, 16)
N_MOD = 6
CHUNK = 128
HEAD_DIM = 128
VMEM_LIMIT = 56 * 1024 * 1024
SUBLANES = 8
ROW_TILE = 512
FFN_CHUNK = 256

F32 = jnp.float32
BF16 = jnp.bfloat16


def _silu(a):
    return a * (1.0 / (1.0 + jnp.exp(-a)))


def _dot(a, b):
    return jnp.dot(a.astype(BF16), b.astype(BF16), preferred_element_type=F32)


def _bdot(a, b):
    return jnp.dot(a, b, preferred_element_type=F32).astype(BF16)


def _split3(a):
    hi = a.astype(BF16)
    r1 = a - hi.astype(F32)
    mid = r1.astype(BF16)
    lo = (r1 - mid.astype(F32)).astype(BF16)
    return hi, mid, lo


def _dot_f32(a, b):
    ah, am, al = _split3(a)
    bh, bm, bl = _split3(b)
    d = lambda x, y: jnp.dot(x, y, preferred_element_type=F32)
    return ((d(al, bh) + d(ah, bl)) + d(am, bm)) + (d(am, bh) + d(ah, bm)) + d(ah, bh)


def _params(*sem):
    return pltpu.CompilerParams(dimension_semantics=sem, vmem_limit_bytes=VMEM_LIMIT)


def _mod_kernel(c_ref, w_ref, b_ref, o_ref):
    o_ref[...] = _dot_f32(_silu(c_ref[...]), w_ref[...]) + b_ref[...]


def _modulation(c_all, w_mod, b_mod):
    r, d = c_all.shape
    n = w_mod.shape[1]
    tn = 512
    return pl.pallas_call(
        _mod_kernel,
        grid=(n // tn,),
        in_specs=[pl.BlockSpec((r, d), lambda j: (0, 0)),
                  pl.BlockSpec((d, tn), lambda j: (0, j)),
                  pl.BlockSpec((1, tn), lambda j: (0, j))],
        out_specs=pl.BlockSpec((r, tn), lambda j: (0, j)),
        out_shape=jax.ShapeDtypeStruct((r, n), F32),
        compiler_params=_params("parallel"),
        name="modulation",
    )(c_all, w_mod, b_mod.reshape(1, n))


LANES = 128


def _inproj_kernel(x_ref, sc_ref, sh_ref, g_ref, w_ref, alog_ref, dtb_ref, tri_ref,
                   qkv_ref, z_ref, pool_ref, gcol_ref, grow_ref, *, n_qkv, n_z, n_pool, n_heads):
    x = x_ref[0]
    xn = x * lax.rsqrt(jnp.mean(x * x, axis=-1, keepdims=True) + NORM_EPS) * g_ref[...]
    h = (xn * (1.0 + sc_ref[0]) + sh_ref[0]).astype(BF16)
    o1 = n_qkv
    o2 = o1 + n_z
    o3 = o2 + n_pool
    pg = jnp.dot(h, w_ref[:, o3:], preferred_element_type=F32)
    qkv_ref[0] = jnp.dot(h, w_ref[:, 0:o1], preferred_element_type=F32).astype(qkv_ref.dtype)
    beta = 1.0 / (1.0 + jnp.exp(-pg))
    a = pg + dtb_ref[...]
    softplus = jnp.maximum(a, 0.0) + jnp.log(1.0 + jnp.exp(-jnp.abs(a)))
    g = -jnp.exp(alog_ref[...]) * softplus
    tm = g.shape[0]
    n_ch = tm // CHUNK
    rows = 4 * n_heads
    beta_t = [beta[c * CHUNK:(c + 1) * CHUNK, :].T[0:rows, :] for c in range(n_ch)]
    g_t = [g[c * CHUNK:(c + 1) * CHUNK, :].T[0:rows, :] for c in range(n_ch)]
    parts = _split3(jnp.concatenate(g_t, axis=0))
    r = jnp.dot(jnp.concatenate(parts, axis=0), tri_ref[...], preferred_element_type=F32)
    z_ref[0] = jnp.dot(h, w_ref[:, o1:o2], preferred_element_type=F32).astype(z_ref.dtype)
    pool_ref[0] = jnp.dot(h, w_ref[:, o2:o3], preferred_element_type=F32).astype(pool_ref.dtype)
    m = n_ch * rows
    pre_all = r[0:m] + r[m:2 * m] + r[2 * m:3 * m]
    kind = lax.broadcasted_iota(jnp.int32, (rows, CHUNK), 0) & 3
    zpad = jnp.zeros((LANES - rows, CHUNK), F32)
    for c in range(n_ch):
        pre = pre_all[c * rows:(c + 1) * rows, :]
        suf = pre[:, CHUNK - 1:CHUNK] - pre + g_t[c]
        row = jnp.where(kind < 2, beta_t[c], jnp.where(kind == 2, pre, suf))
        grow_ref[0, :, c * CHUNK:(c + 1) * CHUNK] = row
        gcol_ref[0, c * CHUNK:(c + 1) * CHUNK, :] = jnp.concatenate([row, zpad], axis=0).T


def _in_projection(x, sc, sh, g_pre, w_all, a_log, dt_bias, n_heads, tm):
    b, t, d = x.shape
    w = n_heads * HEAD_DIM
    n_qkv, n_z = 3 * w, w
    n_pool = w_all.shape[1] - n_qkv - n_z - LANES
    assert CHUNK == LANES
    tri = jnp.asarray(np.triu(np.ones((CHUNK, CHUNK), np.float32)), BF16)
    lane_par = jnp.zeros((2, n_heads, 4), F32)
    lane_par = lane_par.at[0, :, 2:].set(a_log.T).at[1, :, 2:].set(dt_bias.T)
    lane_par = jnp.pad(lane_par.reshape(2, 4 * n_heads), ((0, 0), (0, LANES - 4 * n_heads)))
    kern = functools.partial(_inproj_kernel, n_qkv=n_qkv, n_z=n_z, n_pool=n_pool, n_heads=n_heads)
    row = lambda i, j: (i, j, 0)
    per_b = lambda i, j: (i, 0, 0)
    const = lambda i, j: (0, 0)
    return pl.pallas_call(
        kern,
        grid=(b, t // tm),
        in_specs=[pl.BlockSpec((1, tm, d), row),
                  pl.BlockSpec((1, 1, d), per_b),
                  pl.BlockSpec((1, 1, d), per_b),
                  pl.BlockSpec((1, d), const),
                  pl.BlockSpec(w_all.shape, const),
                  pl.BlockSpec((1, LANES), const),
                  pl.BlockSpec((1, LANES), const),
                  pl.BlockSpec((CHUNK, CHUNK), const)],
        out_specs=[pl.BlockSpec((1, tm, n_qkv), row),
                   pl.BlockSpec((1, tm, n_z), row),
                   pl.BlockSpec((1, tm, n_pool), row),
                   pl.BlockSpec((1, tm, LANES), row),
                   pl.BlockSpec((1, 4 * n_heads, tm), lambda i, j: (i, 0, j))],
        out_shape=[jax.ShapeDtypeStruct((b, t, n_qkv), BF16),
                   jax.ShapeDtypeStruct((b, t, n_z), BF16),
                   jax.ShapeDtypeStruct((b, t, n_pool), BF16),
                   jax.ShapeDtypeStruct((b, t, LANES), F32),
                   jax.ShapeDtypeStruct((b, 4 * n_heads, t), F32)],
        compiler_params=_params("parallel", "parallel"),
        name="in_projection",
    )(x, sc, sh, g_pre.reshape(1, d), w_all, lane_par[0:1], lane_par[1:2], tri)


CONV_HALO = 16


def _conv_silu(ref, cw_ref, win, c, n_chunks, post=None):
    hal = CONV_HALO
    r0 = pl.multiple_of(c * CHUNK, CHUNK)
    p0 = pl.multiple_of(jnp.maximum(r0 - hal, 0), hal)
    n0 = pl.multiple_of(jnp.minimum(r0 + CHUNK, n_chunks * CHUNK - hal), hal)
    win[0:hal, :] = jnp.where(c > 0, ref[0, pl.ds(p0, hal), :].astype(F32), 0.0)
    win[hal:hal + CHUNK, :] = ref[0, pl.ds(r0, CHUNK), :].astype(F32)
    win[hal + CHUNK:, :] = jnp.where(c < n_chunks - 1, ref[0, pl.ds(n0, hal), :].astype(F32), 0.0)
    width = cw_ref.shape[0]
    pad = width // 2
    n_t = CHUNK // SUBLANES
    taps = [cw_ref[j:j + 1, :] for j in range(width)]
    xs = [win[pl.ds(hal - pad + s, n_t, stride=SUBLANES), :] for s in range(SUBLANES + width - 1)]
    outs = []
    for r in range(SUBLANES):
        acc = xs[r] * taps[0]
        for j in range(1, width):
            acc = acc + xs[r + j] * taps[j]
        acc = _silu(acc)
        outs.append(acc if post is None else post(acc))
    for r in range(SUBLANES):
        win[pl.ds(hal + r, n_t, stride=SUBLANES), :] = outs[r]
    return win[hal:hal + CHUNK, :]


def _l2n(a):
    return a * lax.rsqrt(jnp.sum(a * a, axis=-1, keepdims=True) + NORM_EPS)


INV_BASE = 8
LOCAL_UNROLL = 4
HEAD_PAIR = 2


def _inv_unit(mats):
    n = mats[0].shape[0]
    ii = lax.broadcasted_iota(jnp.int32, (n, n), 0)
    jj = lax.broadcasted_iota(jnp.int32, (n, n), 1)

    def same_block(bits):
        return lax.shift_right_logical(ii, bits) == lax.shift_right_logical(jj, bits)

    bits = int(np.log2(INV_BASE))
    base = same_block(bits)
    eye = jnp.where(ii == jj, 1.0, 0.0).astype(BF16)
    zero = jnp.zeros((), BF16)
    abf = [a.astype(BF16) for a in mats]
    ms = [jnp.where(base, -a, zero) for a in abf]
    ps = [eye + m for m in ms]
    ms = [_bdot(m, m) for m in ms]
    for _ in range(bits - 2):
        pm = [_bdot(jnp.concatenate([p, m], axis=0), m) for p, m in zip(ps, ms)]
        ps = [p + x[:n] for p, x in zip(ps, pm)]
        ms = [x[n:] for x in pm]
    ps = [p + _bdot(p, m) for p, m in zip(ps, ms)]
    while (1 << bits) < n:
        lvl = same_block(bits + 1) & jnp.logical_not(same_block(bits))
        tmp = [_bdot(jnp.where(lvl, a, zero), p) for a, p in zip(abf, ps)]
        ps = [p - _bdot(p, t) for p, t in zip(ps, tmp)]
        bits += 1
    return ps


def _chain_operands(q, k, kt, v, gcol, grow, d):
    c = k.shape[0]
    ii = lax.broadcasted_iota(jnp.int32, (c, c), 0)
    jj = lax.broadcasted_iota(jnp.int32, (c, c), 1)
    incl = (ii >= jj) if d == 0 else (ii <= jj)
    strict = (ii > jj) if d == 0 else (ii < jj)
    beta_c = gcol[:, d:d + 1]
    gc_c = gcol[:, 2 + d:3 + d]
    gc_r = grow[2 + d:3 + d, :]
    glast = gc_r[:, c - 1:c] if d == 0 else gc_r[:, 0:1]
    decay = jnp.where(incl, jnp.exp(jnp.where(incl, gc_c - gc_r, 0.0)), 0.0)
    e_c = jnp.exp(gc_c)
    db = jnp.where(strict, decay * beta_c, 0.0).astype(BF16)
    rhs = jnp.concatenate([k * (beta_c * e_c), v * beta_c], axis=1).astype(BF16)
    kdt = (kt * jnp.exp(glast - gc_r)).astype(BF16)
    if q is None:
        return db, None, rhs, kdt, None
    return db, decay.astype(BF16), rhs, kdt, (q * e_c).astype(BF16)


def _chain_solve(chains):
    hd = HEAD_DIM
    mats = [(kk * db.astype(F32)).astype(BF16) for kk, _, db, _, _, _, _ in chains]
    ts = _inv_unit(mats)
    wus = [_dot(t, ch[4]) for t, ch in zip(ts, chains)]
    out = []
    for (kk, qk, db, di, rhs, kdt, qe), wu in zip(chains, wus):
        if qk is None:
            kn_ = _dot(kdt, wu)
            out.append((kn_[:, :hd], kn_[:, hd:], None, None))
        else:
            attn = (qk * di.astype(F32)).astype(BF16)
            x = _dot(jnp.concatenate([kdt, attn], axis=0), wu)
            out.append((x[:hd, :hd], x[:hd, hd:], qe.astype(F32) - x[hd:, :hd], x[hd:, hd:]))
    return out


def _delta_kernel(q_ref, k_ref, v_ref, kc_ref, vc_ref, kc2_ref, vc2_ref, cwq_ref, cwk_ref, cwv_ref,
                  cwk2_ref, cwv2_ref,
                  gcol_ref, grow_ref, gcolc_ref, growc_ref, z_ref, onorm_ref,
                  out_ref, kq_scr, n_scr, o_scr, s_ref, win_scr, stm_a, stg_a, str_a, stm_b, stg_b, str_b,
                  *, n_x, n_c):
    n_tot = n_x + n_c
    hd = HEAD_DIM
    head_shift = lax.rem(LANES - 4 * pl.program_id(1), LANES)
    par = lax.rem(pl.program_id(1), HEAD_PAIR)

    def prep(src_q, src_k, src_v, gcol_r, grow_r, items, n_seq, stage, cwk=cwk_ref, cwv=cwv_ref):
        st_m, st_g, st_r = stage
        for sl, c, pp, shift in items:
            r0 = pl.multiple_of(c * CHUNK, CHUNK)
            k = _conv_silu(src_k, cwk, win_scr.at[3 * sl], c, n_seq, _l2n)
            v = _conv_silu(src_v, cwv, win_scr.at[3 * sl + 1], c, n_seq)
            q = None if src_q is None else _conv_silu(src_q, cwq_ref, win_scr.at[3 * sl + 2], c, n_seq,
                                                      lambda a: _l2n(a) * (hd ** -0.5))
            kt = k.T
            st_m[sl, 0] = k.astype(BF16)
            st_m[sl, 1] = kt.astype(BF16)
            if q is not None:
                st_m[sl, 2] = q.astype(BF16)
            gcol = pltpu.roll(gcol_r[0, pl.ds(r0, CHUNK), :], shift, axis=1)[:, 0:4]
            grow = grow_r[0, pp, c]
            for d in (0, 1):
                db, di, rhs, kdt, qe = _chain_operands(q, k, kt, v, gcol, grow, d)
                st_g[sl, d, 0] = db
                st_g[sl, d, 1] = kdt
                st_r[sl, d] = rhs
                if q is not None:
                    st_g[sl, d, 2] = di
                    st_g[sl, d, 3] = qe

    def local(stage, with_q, items):
        st_m, st_g, st_r = stage
        chains = []
        for sl, c, pp, _ in items:
            if with_q:
                kq = jnp.dot(jnp.concatenate([st_m[sl, 0], st_m[sl, 2]], axis=0), st_m[sl, 1],
                             preferred_element_type=F32)
                kk, qk = kq[:CHUNK], kq[CHUNK:]
            else:
                kk, qk = jnp.dot(st_m[sl, 0], st_m[sl, 1], preferred_element_type=F32), None
            for d in (0, 1):
                chains.append((kk, qk, st_g[sl, d, 0], st_g[sl, d, 2] if with_q else None, st_r[sl, d],
                               st_g[sl, d, 1], st_g[sl, d, 3] if with_q else None))
        res = _chain_solve(chains)
        for j, (_, c, pp, out_slot) in enumerate(items):
            r0 = pl.multiple_of(c * CHUNK, CHUNK)
            for d in (0, 1):
                kmat, nmat, qmat, omat = res[2 * j + d]
                idx = (pp * 2 + d) * n_tot + out_slot
                kq_scr[idx, 0:hd, :] = kmat.astype(BF16)
                n_scr[idx] = nmat.astype(BF16)
                if qmat is not None:
                    kq_scr[idx, hd:, :] = qmat.astype(BF16)
                    o_scr[pp * 2 + d, pl.ds(r0, CHUNK), :] = omat

    u_c = min(LOCAL_UNROLL, n_c)
    u_x = min(LOCAL_UNROLL, n_x)
    g_c, g_x = n_c // u_c, n_x // u_x
    stages = ((stm_a, stg_a, str_a), (stm_b, stg_b, str_b))

    def prep_x(g, stage):
        prep(q_ref, k_ref, v_ref, gcol_ref, grow_ref,
             [(j, g * u_x + j, par, head_shift) for j in range(u_x)], n_x, stage)

    def local_x(g, stage):
        local(stage, True, [(j, g * u_x + j, par, n_c + g * u_x + j) for j in range(u_x)])

    if g_c == 1 and HEAD_PAIR * n_c <= LOCAL_UNROLL:
        @pl.when(par == 0)
        def _context_pair():
            items = []
            for pp in range(HEAD_PAIR):
                src = (kc_ref, vc_ref, cwk_ref, cwv_ref) if pp == 0 else (kc2_ref, vc2_ref, cwk2_ref, cwv2_ref)
                shift = lax.rem(LANES - 4 * (pl.program_id(1) + pp), LANES)
                prep(None, src[0], src[1], gcolc_ref, growc_ref,
                     [(pp * n_c + j, j, pp, shift) for j in range(n_c)], n_c, stages[0], src[2], src[3])
                items += [(pp * n_c + j, j, pp, j) for j in range(n_c)]
            local(stages[0], False, items)
            prep_x(0, stages[1])

        @pl.when(par != 0)
        def _latent_only():
            prep_x(0, stages[1])
    else:
        def prep_c(g, stage):
            prep(None, kc_ref, vc_ref, gcolc_ref, growc_ref,
                 [(j, g * u_c + j, par, head_shift) for j in range(u_c)], n_c, stage)

        prep_c(0, stages[0])
        for g in range(g_c):
            local(stages[g % 2], False, [(j, g * u_c + j, par, g * u_c + j) for j in range(u_c)])
            if g + 1 < g_c:
                prep_c(g + 1, stages[(g + 1) % 2])
            else:
                prep_x(0, stages[(g + 1) % 2])

    if g_x % 2 == 0:
        def local_pair(p, carry):
            for h in (0, 1):
                g = 2 * p + h
                prep_x(jnp.minimum(g + 1, g_x - 1), stages[(g_c + h + 1) % 2])
                local_x(g, stages[(g_c + h) % 2])
            return carry

        lax.fori_loop(0, g_x // 2, local_pair, 0)
    else:
        for g in range(g_x):
            local_x(g, stages[(g_c + g) % 2])
            if g + 1 < g_x:
                prep_x(g + 1, stages[(g_c + g + 1) % 2])
    @pl.when(par == HEAD_PAIR - 1)
    def _sequential():
        s_ref[...] = jnp.zeros_like(s_ref)
        chains = [(pp, d) for pp in range(HEAD_PAIR) for d in (0, 1)]

        def state_decay(grow, d):
            tot = grow[2 + d:3 + d, CHUNK - 1:CHUNK] if d == 0 else grow[2 + d:3 + d, 0:1]
            return jnp.exp(tot)

        def seq_ctx(i, carry):
            for pp, d in chains:
                c = i if d == 0 else n_c - 1 - i
                idx = (pp * 2 + d) * n_tot + c
                s = s_ref[pp * 2 + d]
                ks = jnp.dot(kq_scr[idx, 0:hd, :], s.astype(BF16), preferred_element_type=F32)
                s_ref[pp * 2 + d] = (s * state_decay(growc_ref[0, pp, c], d) - ks
                                     + n_scr[idx].astype(F32))
            return carry

        def finish(pp, r0, o):
            lanes = slice(pp * hd, (pp + 1) * hd)
            on = o * lax.rsqrt(jnp.mean(o * o, axis=-1, keepdims=True) + NORM_EPS) * onorm_ref[...]
            zg = _silu(z_ref[0, pl.ds(r0, CHUNK), lanes].astype(F32))
            out_ref[0, pl.ds(r0, CHUNK), lanes] = (on * zg).astype(out_ref.dtype)

        def seq_x(i, carry, second_visit):
            for pp, d in chains:
                c = i if d == 0 else n_x - 1 - i
                r0 = pl.multiple_of(c * CHUNK, CHUNK)
                idx = (pp * 2 + d) * n_tot + n_c + c
                s = s_ref[pp * 2 + d]
                kqs = jnp.dot(kq_scr[idx], s.astype(BF16), preferred_element_type=F32)
                o = o_scr[pp * 2 + d, pl.ds(r0, CHUNK), :] + kqs[hd:, :]
                if second_visit:
                    finish(pp, r0, o + o_scr[pp * 2 + 1 - d, pl.ds(r0, CHUNK), :])
                else:
                    o_scr[pp * 2 + d, pl.ds(r0, CHUNK), :] = o
                s_ref[pp * 2 + d] = (s * state_decay(grow_ref[0, pp, c], d) - kqs[0:hd, :]
                                     + n_scr[idx].astype(F32))
            return carry

        lax.fori_loop(0, n_c, seq_ctx, 0)
        half = (n_x + 1) // 2
        lax.fori_loop(0, half, functools.partial(seq_x, second_visit=False), 0)
        if n_x % 2:
            r_mid = (half - 1) * CHUNK
            for pp in range(HEAD_PAIR):
                finish(pp, r_mid, o_scr[pp * 2, r_mid:r_mid + CHUNK, :] + o_scr[pp * 2 + 1, r_mid:r_mid + CHUNK, :])
        lax.fori_loop(half, n_x, functools.partial(seq_x, second_visit=True), 0)


def _row_gates(grow, n_heads):
    b, _, t = grow.shape
    return jnp.transpose(grow.reshape(b, n_heads, 4, t // CHUNK, CHUNK), (0, 1, 3, 2, 4))


def _delta_mixer(qkv, qkv_c, gcol, grow, gcolc, growc, z, conv_qkv, o_norm, n_heads):
    b, t, _ = qkv.shape
    tc = qkv_c.shape[1]
    n_x, n_c = t // CHUNK, tc // CHUNK
    hd = HEAD_DIM
    assert CHUNK == hd and t % CHUNK == 0 and tc % CHUNK == 0
    width = conv_qkv.shape[0]
    grow = _row_gates(grow, n_heads)
    growc = _row_gates(growc, n_heads)
    kern = functools.partial(_delta_kernel, n_x=n_x, n_c=n_c)
    nh = n_heads

    def colblk(off, ahead=0):
        return lambda i, h: (i, 0, off + jnp.minimum(h + ahead, nh - 1))

    def cwblk(off, ahead=0):
        return lambda i, h: (0, off + jnp.minimum(h + ahead, nh - 1))

    assert nh % HEAD_PAIR == 0
    pair5 = lambda i, h: (i, h // HEAD_PAIR, 0, 0, 0)
    pair3 = lambda i, h: (i, 0, h // HEAD_PAIR)
    return pl.pallas_call(
        kern,
        grid=(b, nh),
        in_specs=[pl.BlockSpec((1, t, hd), colblk(0)),
                  pl.BlockSpec((1, t, hd), colblk(nh)),
                  pl.BlockSpec((1, t, hd), colblk(2 * nh)),
                  pl.BlockSpec((1, tc, hd), colblk(nh)),
                  pl.BlockSpec((1, tc, hd), colblk(2 * nh)),
                  pl.BlockSpec((1, tc, hd), colblk(nh, 1)),
                  pl.BlockSpec((1, tc, hd), colblk(2 * nh, 1)),
                  pl.BlockSpec((width, hd), cwblk(0)),
                  pl.BlockSpec((width, hd), cwblk(nh)),
                  pl.BlockSpec((width, hd), cwblk(2 * nh)),
                  pl.BlockSpec((width, hd), cwblk(nh, 1)),
                  pl.BlockSpec((width, hd), cwblk(2 * nh, 1)),
                  pl.BlockSpec((1, t, LANES), lambda i, h: (i, 0, 0)),
                  pl.BlockSpec((1, HEAD_PAIR, n_x, 4, CHUNK), pair5),
                  pl.BlockSpec((1, tc, LANES), lambda i, h: (i, 0, 0)),
                  pl.BlockSpec((1, HEAD_PAIR, n_c, 4, CHUNK), pair5),
                  pl.BlockSpec((1, t, HEAD_PAIR * hd), pair3),
                  pl.BlockSpec((1, hd), lambda i, h: (0, 0))],
        out_specs=pl.BlockSpec((1, t, HEAD_PAIR * hd), pair3),
        out_shape=jax.ShapeDtypeStruct((b, t, nh * hd), BF16),
        scratch_shapes=[pltpu.VMEM((2 * HEAD_PAIR * (n_x + n_c), 2 * hd, hd), BF16),
                        pltpu.VMEM((2 * HEAD_PAIR * (n_x + n_c), hd, hd), BF16),
                        pltpu.VMEM((2 * HEAD_PAIR, t, hd), F32),
                        pltpu.VMEM((2 * HEAD_PAIR, hd, hd), F32),
                        pltpu.VMEM((3 * LOCAL_UNROLL, CHUNK + 2 * CONV_HALO, hd), F32),
                        pltpu.VMEM((LOCAL_UNROLL, 3, CHUNK, hd), BF16),
                        pltpu.VMEM((LOCAL_UNROLL, 2, 4, CHUNK, hd), BF16),
                        pltpu.VMEM((LOCAL_UNROLL, 2, CHUNK, 2 * hd), BF16),
                        pltpu.VMEM((LOCAL_UNROLL, 3, CHUNK, hd), BF16),
                        pltpu.VMEM((LOCAL_UNROLL, 2, 4, CHUNK, hd), BF16),
                        pltpu.VMEM((LOCAL_UNROLL, 2, CHUNK, 2 * hd), BF16)],
        compiler_params=_params("parallel", "arbitrary"),
        name="delta_mixer",
    )(qkv, qkv, qkv, qkv_c, qkv_c, qkv_c, qkv_c, conv_qkv, conv_qkv, conv_qkv, conv_qkv, conv_qkv,
      gcol, grow, gcolc, growc, z, o_norm.reshape(1, hd))


POOL_UNROLL = 4


def _pool_kernel(u_ref, band_ref, icnt_ref, w_ref, sc_ref, out_ref, cs, *, rows, blk):
    g = pl.program_id(1)
    left = lax.shift_left(jnp.int32(1), g)
    right = left - 1
    gw = GRID_W
    gd = cs.shape[1]
    t = rows * gw

    cs[0:gw, :] = jnp.zeros((gw, gd), F32)

    def prefix(r, acc):
        acc = acc + u_ref[0, pl.ds(pl.multiple_of(r * gw, gw), gw), :].astype(F32)
        cs[pl.ds(pl.multiple_of((r + 1) * gw, gw), gw), :] = acc
        return acc

    lax.fori_loop(0, rows, prefix, jnp.zeros((gw, gd), F32))

    def row_mean(r):
        lo = jnp.maximum(r - left, 0)
        hi = jnp.minimum(r + right + 1, rows)
        tot = cs[pl.ds(pl.multiple_of(hi * gw, gw), gw), :] - cs[pl.ds(pl.multiple_of(lo * gw, gw), gw), :]
        return tot / jnp.full((gw, gd), hi - lo, jnp.int32).astype(F32)

    def col_body(i, carry):
        blocks = [i * POOL_UNROLL + j for j in range(POOL_UNROLL)]
        starts = [pl.multiple_of(bi * blk, blk) for bi in blocks]
        rs = [jnp.concatenate([row_mean(bi * (blk // gw) + rr) for rr in range(blk // gw)], axis=0)
              for bi in blocks]
        ms = [jnp.dot(band_ref[0], r.astype(BF16), preferred_element_type=F32) for r in rs]
        ds = [(m * icnt_ref[0] - u_ref[0, pl.ds(r0, blk), :].astype(F32)).astype(BF16)
              for m, r0 in zip(ms, starts)]
        ys = [jnp.dot(dlt, w_ref[0], preferred_element_type=F32) for dlt in ds]
        for y, r0 in zip(ys, starts):
            out_ref[0, pl.ds(r0, blk), :] = (y * sc_ref[0]).astype(out_ref.dtype)
        return carry

    lax.fori_loop(0, t // (blk * POOL_UNROLL), col_body, 0)


def _pool_consts(blk):
    band = np.zeros((len(POOL_WINDOWS), blk, blk), np.float32)
    icnt = np.zeros((len(POOL_WINDOWS), blk, HEAD_DIM), np.float32)
    for gi, w in enumerate(POOL_WINDOWS):
        left = w // 2
        right = w - 1 - left
        for i in range(blk):
            base, col = (i // GRID_W) * GRID_W, i % GRID_W
            lo, hi = max(col - left, 0), min(col + right, GRID_W - 1)
            band[gi, i, base + lo:base + hi + 1] = 1.0
            icnt[gi, i, :] = 1.0 / (hi - lo + 1)
    return jnp.asarray(band, BF16), jnp.asarray(icnt)


def _pool_mixer(u, pool_w, pool_scale):
    b, t, p = u.shape
    n_g, gd = pool_w.shape[0], pool_w.shape[1]
    rows = t // GRID_W
    blk = 4 * GRID_W
    assert t % (blk * POOL_UNROLL) == 0
    band, icnt = _pool_consts(blk)
    kern = functools.partial(_pool_kernel, rows=rows, blk=blk)
    grp = lambda i, g: (g, 0, 0)
    return pl.pallas_call(
        kern,
        grid=(b, n_g),
        in_specs=[pl.BlockSpec((1, t, gd), lambda i, g: (i, 0, g)),
                  pl.BlockSpec((1, blk, blk), grp),
                  pl.BlockSpec((1, blk, gd), grp),
                  pl.BlockSpec((1, gd, gd), grp),
                  pl.BlockSpec((1, 1, gd), grp)],
        out_specs=pl.BlockSpec((1, t, gd), lambda i, g: (i, 0, g)),
        out_shape=jax.ShapeDtypeStruct((b, t, p), BF16),
        scratch_shapes=[pltpu.VMEM((t + GRID_W, gd), F32)],
        compiler_params=_params("parallel", "parallel"),
        name="pool_mixer",
    )(u, band, icnt, pool_w.astype(BF16), pool_scale.reshape(n_g, 1, gd))


def _rms(a, g):
    return a * lax.rsqrt(jnp.mean(a * a, axis=-1, keepdims=True) + NORM_EPS) * g


def _outproj_kernel(dn_ref, pool_ref, x_ref, w1_ref, w2_ref, gt_ref, sc_ref, sh_ref,
                    gpost_ref, gpre_ref, x1_ref, h_ref):
    y = (jnp.dot(dn_ref[0], w1_ref[...], preferred_element_type=F32)
         + jnp.dot(pool_ref[0], w2_ref[...], preferred_element_type=F32))
    x1 = x_ref[0] + gt_ref[0] * _rms(y, gpost_ref[...])
    x1_ref[0] = x1
    h_ref[0] = (_rms(x1, gpre_ref[...]) * (1.0 + sc_ref[0]) + sh_ref[0]).astype(h_ref.dtype)


def _out_projection(dn, pool, x, w_out, gt1, sc2, sh2, g_post, g_pre, tm):
    b, t, d = x.shape
    wd = dn.shape[-1]
    w1 = w_out[:wd].astype(BF16)
    w2 = w_out[wd:].astype(BF16)
    row = lambda i, j: (i, j, 0)
    per_b = lambda i, j: (i, 0, 0)
    const = lambda i, j: (0, 0)
    return pl.pallas_call(
        _outproj_kernel,
        grid=(b, t // tm),
        in_specs=[pl.BlockSpec((1, tm, wd), row),
                  pl.BlockSpec((1, tm, pool.shape[-1]), row),
                  pl.BlockSpec((1, tm, d), row),
                  pl.BlockSpec(w1.shape, const),
                  pl.BlockSpec(w2.shape, const),
                  pl.BlockSpec((1, 1, d), per_b),
                  pl.BlockSpec((1, 1, d), per_b),
                  pl.BlockSpec((1, 1, d), per_b),
                  pl.BlockSpec((1, d), const),
                  pl.BlockSpec((1, d), const)],
        out_specs=[pl.BlockSpec((1, tm, d), row), pl.BlockSpec((1, tm, d), row)],
        out_shape=[jax.ShapeDtypeStruct((b, t, d), F32), jax.ShapeDtypeStruct((b, t, d), BF16)],
        compiler_params=_params("parallel", "parallel"),
        name="out_projection",
    )(dn, pool, x, w1, w2, gt1, sc2, sh2, g_post.reshape(1, d), g_pre.reshape(1, d))


FFN_HALO = 16


def _ffn_kernel(h_ref, hp_ref, hn_ref, x1_ref, wgu_ref, cw_ref, wd_ref, gt_ref, gpost_ref,
                out_ref, hext, act, *, fc, n_fc):
    j = pl.program_id(1)
    nj = pl.num_programs(1)
    tm = h_ref.shape[1]
    hal = FFN_HALO
    dff = fc * n_fc
    hext[0:hal, :] = jnp.where(j > 0, hp_ref[0], jnp.zeros_like(hp_ref[0]))
    hext[hal:hal + tm, :] = h_ref[0]
    hext[hal + tm:, :] = jnp.where(j < nj - 1, hn_ref[0], jnp.zeros_like(hn_ref[0]))
    for c in range(n_fc):
        gate = jnp.dot(hext[...], wgu_ref[:, c * fc:(c + 1) * fc], preferred_element_type=F32)
        up = jnp.dot(h_ref[0], wgu_ref[:, dff + c * fc:dff + (c + 1) * fc], preferred_element_type=F32)
        cw = cw_ref[:, c * fc:(c + 1) * fc]
        conv = (gate[hal - 1:hal - 1 + tm] * cw[0:1] + gate[hal:hal + tm] * cw[1:2]
                + gate[hal + 1:hal + 1 + tm] * cw[2:3])
        act[:, c * fc:(c + 1) * fc] = (_silu(conv) * up).astype(BF16)
    y = jnp.dot(act[...], wd_ref[...], preferred_element_type=F32)
    out_ref[0] = x1_ref[0] + gt_ref[0] * _rms(y, gpost_ref[...])


def _conv_ffn(h, x1, w_up, conv_ffn, w_down, gt2, g_post, tm, fc):
    b, t, d = x1.shape
    dff = w_down.shape[0]
    n_fc = dff // fc
    wgu = w_up.astype(BF16)
    hal = FFN_HALO
    nb = tm // hal
    last = t // hal - 1
    kern = functools.partial(_ffn_kernel, fc=fc, n_fc=n_fc)
    row = lambda i, j: (i, j, 0)
    per_b = lambda i, j: (i, 0, 0)
    const = lambda i, j: (0, 0)
    return pl.pallas_call(
        kern,
        grid=(b, t // tm),
        in_specs=[pl.BlockSpec((1, tm, d), row),
                  pl.BlockSpec((1, hal, d), lambda i, j: (i, jnp.maximum(j * nb - 1, 0), 0)),
                  pl.BlockSpec((1, hal, d), lambda i, j: (i, jnp.minimum((j + 1) * nb, last), 0)),
                  pl.BlockSpec((1, tm, d), row),
                  pl.BlockSpec(wgu.shape, const),
                  pl.BlockSpec(conv_ffn.shape, const),
                  pl.BlockSpec(w_down.shape, const),
                  pl.BlockSpec((1, 1, d), per_b),
                  pl.BlockSpec((1, d), const)],
        out_specs=pl.BlockSpec((1, tm, d), row),
        out_shape=jax.ShapeDtypeStruct((b, t, d), F32),
        scratch_shapes=[pltpu.VMEM((tm + 2 * hal, d), BF16), pltpu.VMEM((tm, dff), BF16)],
        compiler_params=_params("parallel", "parallel"),
        name="conv_ffn",
    )(h, h, h, x1, wgu, conv_ffn, w_down.astype(BF16), gt2, g_post.reshape(1, d))


def _layer(x, c, ctx, c_ctx, w_mod, b_mod, g_pre_mix, g_post_mix, g_pre_ffn, g_post_ffn,
           w_in, conv_qkv, a_log, dt_bias, o_norm, pool_w, pool_scale, w_out, w_up,
           conv_ffn, w_down):
    b, t, d = x.shape
    n_heads = a_log.shape[-1]
    w = n_heads * HEAD_DIM
    n_gate = 4 * n_heads

    rows = pl.cdiv(b + 1, SUBLANES) * SUBLANES
    c_all = jnp.zeros((rows, d), F32).at[:b].set(c).at[b].set(c_ctx)
    m = _modulation(c_all, w_mod, b_mod)
    sh1, sc1, gt1, sh2, sc2, gt2 = [m[:b, None, i * d:(i + 1) * d] for i in range(N_MOD)]
    csh1 = jnp.broadcast_to(m[b, 0 * d:1 * d], (b, 1, d))
    csc1 = jnp.broadcast_to(m[b, 1 * d:2 * d], (b, 1, d))

    off_g = 4 * w
    w_gate = jnp.transpose(w_in[:, off_g:off_g + n_gate].reshape(d, 4, n_heads), (0, 2, 1)).reshape(d, n_gate)
    w_all = jnp.concatenate([w_in[:, :off_g], w_in[:, off_g + n_gate:],
                             jnp.pad(w_gate, ((0, 0), (0, LANES - n_gate)))], axis=1).astype(BF16)
    tm = min(ROW_TILE, t)
    qkv, z, pool_u, gcol, grow = _in_projection(x, sc1, sh1, g_pre_mix, w_all, a_log, dt_bias, n_heads, tm)
    qkv_c, _, _, gcolc, growc = _in_projection(ctx, csc1, csh1, g_pre_mix, w_all, a_log, dt_bias,
                                                n_heads, ctx.shape[1])
    dn = _delta_mixer(qkv, qkv_c, gcol, grow, gcolc, growc, z, conv_qkv, o_norm, n_heads)
    pool = _pool_mixer(pool_u, pool_w, pool_scale)
    x1, h2 = _out_projection(dn, pool, x, w_out, gt1, sc2, sh2, g_post_mix, g_pre_ffn, tm)
    return _conv_ffn(h2, x1, w_up, conv_ffn, w_down, gt2, g_post_ffn, tm, FFN_CHUNK)


def kernel(x, c, ctx, c_ctx, w_mod, b_mod, g_pre_mix, g_post_mix, g_pre_ffn, g_post_ffn, w_in,
           conv_qkv, a_log, dt_bias, o_norm, pool_w, pool_scale, w_out, w_up, conv_ffn, w_down):
    depth = w_mod.shape[0]
    assert depth == 1, "context-stream update between layers is not implemented"
    return _layer(x, c, ctx, c_ctx, w_mod[0], b_mod[0], g_pre_mix[0], g_post_mix[0],
                  g_pre_ffn[0], g_post_ffn[0], w_in[0], conv_qkv[0], a_log[0], dt_bias[0],
                  o_norm[0], pool_w[0], pool_scale[0], w_out[0], w_up[0], conv_ffn[0], w_down[0])
```

```python
import functools

import numpy as np
import jax
import jax.numpy as jnp
from jax import lax
from jax.experimental import pallas as pl
from jax.experimental.pallas import tpu as pltpu

NORM_EPS = 1e-6
GRID_W = 64
POOL_WINDOWS = (2, 4, 8, 16)
N_MOD = 6
CHUNK = 128
HEAD_DIM = 128
VMEM_LIMIT = 56 * 1024 * 1024
SUBLANES = 8
ROW_TILE = 512
PROJ_TILE = 1024
FFN_CHUNK = 256

F32 = jnp.float32
BF16 = jnp.bfloat16


def _silu(a):
    return a * (1.0 / (1.0 + jnp.exp(-a)))


def _dot(a, b):
    return jnp.dot(a.astype(BF16), b.astype(BF16), preferred_element_type=F32)


def _bdot(a, b):
    return jnp.dot(a, b, preferred_element_type=F32).astype(BF16)


def _split3(a):
    hi = a.astype(BF16)
    r1 = a - hi.astype(F32)
    mid = r1.astype(BF16)
    lo = (r1 - mid.astype(F32)).astype(BF16)
    return hi, mid, lo


def _dot_f32(a, b):
    ah, am, al = _split3(a)
    bh, bm, bl = _split3(b)
    d = lambda x, y: jnp.dot(x, y, preferred_element_type=F32)
    return ((d(al, bh) + d(ah, bl)) + d(am, bm)) + (d(am, bh) + d(ah, bm)) + d(ah, bh)


def _params(*sem):
    return pltpu.CompilerParams(dimension_semantics=sem, vmem_limit_bytes=VMEM_LIMIT)


def _mod_kernel(c_ref, w_ref, b_ref, o_ref):
    o_ref[...] = _dot_f32(_silu(c_ref[...]), w_ref[...]) + b_ref[...]


def _modulation(c_all, w_mod, b_mod):
    r, d = c_all.shape
    n = w_mod.shape[1]
    tn = 512
    return pl.pallas_call(
        _mod_kernel,
        grid=(n // tn,),
        in_specs=[pl.BlockSpec((r, d), lambda j: (0, 0)),
                  pl.BlockSpec((d, tn), lambda j: (0, j)),
                  pl.BlockSpec((1, tn), lambda j: (0, j))],
        out_specs=pl.BlockSpec((r, tn), lambda j: (0, j)),
        out_shape=jax.ShapeDtypeStruct((r, n), F32),
        compiler_params=_params("parallel"),
        name="modulation",
    )(c_all, w_mod, b_mod.reshape(1, n))


LANES = 128


def _inproj_kernel(x_ref, sc_ref, sh_ref, g_ref, w_ref, alog_ref, dtb_ref, tri_ref,
                   qkv_ref, z_ref, pool_ref, gcol_ref, grow_ref, *, n_qkv, n_z, n_pool, n_heads):
    x = x_ref[0]
    xn = x * lax.rsqrt(jnp.mean(x * x, axis=-1, keepdims=True) + NORM_EPS) * g_ref[...]
    h = (xn * (1.0 + sc_ref[0]) + sh_ref[0]).astype(BF16)
    o1 = n_qkv
    o2 = o1 + n_z
    o3 = o2 + n_pool
    pg = jnp.dot(h, w_ref[:, o3:], preferred_element_type=F32)
    qkv_ref[0] = jnp.dot(h, w_ref[:, 0:o1], preferred_element_type=F32).astype(qkv_ref.dtype)
    beta = 1.0 / (1.0 + jnp.exp(-pg))
    a = pg + dtb_ref[...]
    softplus = jnp.maximum(a, 0.0) + jnp.log(1.0 + jnp.exp(-jnp.abs(a)))
    g = -jnp.exp(alog_ref[...]) * softplus
    tm = g.shape[0]
    n_ch = tm // CHUNK
    rows = 4 * n_heads
    beta_t = [beta[c * CHUNK:(c + 1) * CHUNK, :].T[0:rows, :] for c in range(n_ch)]
    g_t = [g[c * CHUNK:(c + 1) * CHUNK, :].T[0:rows, :] for c in range(n_ch)]
    parts = _split3(jnp.concatenate(g_t, axis=0))
    r = jnp.dot(jnp.concatenate(parts, axis=0), tri_ref[...], preferred_element_type=F32)
    z_ref[0] = jnp.dot(h, w_ref[:, o1:o2], preferred_element_type=F32).astype(z_ref.dtype)
    pool_ref[0] = jnp.dot(h, w_ref[:, o2:o3], preferred_element_type=F32).astype(pool_ref.dtype)
    m = n_ch * rows
    pre_all = r[0:m] + r[m:2 * m] + r[2 * m:3 * m]
    kind = lax.broadcasted_iota(jnp.int32, (rows, CHUNK), 0) & 3
    zpad = jnp.zeros((LANES - rows, CHUNK), F32)
    for c in range(n_ch):
        pre = pre_all[c * rows:(c + 1) * rows, :]
        suf = pre[:, CHUNK - 1:CHUNK] - pre + g_t[c]
        row = jnp.where(kind < 2, beta_t[c], jnp.where(kind == 2, pre, suf))
        grow_ref[0, :, c * CHUNK:(c + 1) * CHUNK] = row
        gcol_ref[0, c * CHUNK:(c + 1) * CHUNK, :] = jnp.concatenate([row, zpad], axis=0).T


def _in_projection(x, sc, sh, g_pre, w_all, a_log, dt_bias, n_heads, tm):
    b, t, d = x.shape
    w = n_heads * HEAD_DIM
    n_qkv, n_z = 3 * w, w
    n_pool = w_all.shape[1] - n_qkv - n_z - LANES
    assert CHUNK == LANES
    tri = jnp.asarray(np.triu(np.ones((CHUNK, CHUNK), np.float32)), BF16)
    lane_par = jnp.zeros((2, n_heads, 4), F32)
    lane_par = lane_par.at[0, :, 2:].set(a_log.T).at[1, :, 2:].set(dt_bias.T)
    lane_par = jnp.pad(lane_par.reshape(2, 4 * n_heads), ((0, 0), (0, LANES - 4 * n_heads)))
    kern = functools.partial(_inproj_kernel, n_qkv=n_qkv, n_z=n_z, n_pool=n_pool, n_heads=n_heads)
    row = lambda i, j: (i, j, 0)
    per_b = lambda i, j: (i, 0, 0)
    const = lambda i, j: (0, 0)
    return pl.pallas_call(
        kern,
        grid=(b, t // tm),
        in_specs=[pl.BlockSpec((1, tm, d), row),
                  pl.BlockSpec((1, 1, d), per_b),
                  pl.BlockSpec((1, 1, d), per_b),
                  pl.BlockSpec((1, d), const),
                  pl.BlockSpec(w_all.shape, const),
                  pl.BlockSpec((1, LANES), const),
                  pl.BlockSpec((1, LANES), const),
                  pl.BlockSpec((CHUNK, CHUNK), const)],
        out_specs=[pl.BlockSpec((1, tm, n_qkv), row),
                   pl.BlockSpec((1, tm, n_z), row),
                   pl.BlockSpec((1, tm, n_pool), row),
                   pl.BlockSpec((1, tm, LANES), row),
                   pl.BlockSpec((1, 4 * n_heads, tm), lambda i, j: (i, 0, j))],
        out_shape=[jax.ShapeDtypeStruct((b, t, n_qkv), BF16),
                   jax.ShapeDtypeStruct((b, t, n_z), BF16),
                   jax.ShapeDtypeStruct((b, t, n_pool), BF16),
                   jax.ShapeDtypeStruct((b, t, LANES), F32),
                   jax.ShapeDtypeStruct((b, 4 * n_heads, t), F32)],
        compiler_params=_params("parallel", "parallel"),
        name="in_projection",
    )(x, sc, sh, g_pre.reshape(1, d), w_all, lane_par[0:1], lane_par[1:2], tri)


CONV_HALO = 16


def _conv_silu(ref, cw_ref, win, c, n_chunks, post=None):
    hal = CONV_HALO
    r0 = pl.multiple_of(c * CHUNK, CHUNK)
    p0 = pl.multiple_of(jnp.maximum(r0 - hal, 0), hal)
    n0 = pl.multiple_of(jnp.minimum(r0 + CHUNK, n_chunks * CHUNK - hal), hal)
    win[0:hal, :] = jnp.where(c > 0, ref[0, pl.ds(p0, hal), :].astype(F32), 0.0)
    win[hal:hal + CHUNK, :] = ref[0, pl.ds(r0, CHUNK), :].astype(F32)
    win[hal + CHUNK:, :] = jnp.where(c < n_chunks - 1, ref[0, pl.ds(n0, hal), :].astype(F32), 0.0)
    width = cw_ref.shape[0]
    pad = width // 2
    n_t = CHUNK // SUBLANES
    taps = [cw_ref[j:j + 1, :] for j in range(width)]
    xs = [win[pl.ds(hal - pad + s, n_t, stride=SUBLANES), :] for s in range(SUBLANES + width - 1)]
    outs = []
    for r in range(SUBLANES):
        acc = xs[r] * taps[0]
        for j in range(1, width):
            acc = acc + xs[r + j] * taps[j]
        acc = _silu(acc)
        outs.append(acc if post is None else post(acc))
    for r in range(SUBLANES):
        win[pl.ds(hal + r, n_t, stride=SUBLANES), :] = outs[r]
    return win[hal:hal + CHUNK, :]


def _l2n(a):
    return a * lax.rsqrt(jnp.sum(a * a, axis=-1, keepdims=True) + NORM_EPS)


INV_BASE = 8
LOCAL_UNROLL = 4
HEAD_PAIR = 2


def _inv_unit(mats):
    n = mats[0].shape[0]
    ii = lax.broadcasted_iota(jnp.int32, (n, n), 0)
    jj = lax.broadcasted_iota(jnp.int32, (n, n), 1)

    def same_block(bits):
        return lax.shift_right_logical(ii, bits) == lax.shift_right_logical(jj, bits)

    bits = int(np.log2(INV_BASE))
    base = same_block(bits)
    eye = jnp.where(ii == jj, 1.0, 0.0).astype(BF16)
    zero = jnp.zeros((), BF16)
    abf = [a.astype(BF16) for a in mats]
    ms = [jnp.where(base, -a, zero) for a in abf]
    ps = [eye + m for m in ms]
    ms = [_bdot(m, m) for m in ms]
    for _ in range(bits - 2):
        pm = [_bdot(jnp.concatenate([p, m], axis=0), m) for p, m in zip(ps, ms)]
        ps = [p + x[:n] for p, x in zip(ps, pm)]
        ms = [x[n:] for x in pm]
    ps = [p + _bdot(p, m) for p, m in zip(ps, ms)]
    while (1 << bits) < n:
        lvl = same_block(bits + 1) & jnp.logical_not(same_block(bits))
        tmp = [_bdot(jnp.where(lvl, a, zero), p) for a, p in zip(abf, ps)]
        ps = [p - _bdot(p, t) for p, t in zip(ps, tmp)]
        bits += 1
    return ps


def _chain_operands(q, k, kt, v, gcol, grow, d):
    c = k.shape[0]
    ii = lax.broadcasted_iota(jnp.int32, (c, c), 0)
    jj = lax.broadcasted_iota(jnp.int32, (c, c), 1)
    incl = (ii >= jj) if d == 0 else (ii <= jj)
    strict = (ii > jj) if d == 0 else (ii < jj)
    beta_c = gcol[:, d:d + 1]
    gc_c = gcol[:, 2 + d:3 + d]
    gc_r = grow[2 + d:3 + d, :]
    glast = gc_r[:, c - 1:c] if d == 0 else gc_r[:, 0:1]
    decay = jnp.where(incl, jnp.exp(jnp.where(incl, gc_c - gc_r, 0.0)), 0.0)
    e_c = jnp.exp(gc_c)
    db = jnp.where(strict, decay * beta_c, 0.0).astype(BF16)
    rhs = jnp.concatenate([k * (beta_c * e_c), v * beta_c], axis=1).astype(BF16)
    kdt = (kt * jnp.exp(glast - gc_r)).astype(BF16)
    if q is None:
        return db, None, rhs, kdt, None
    return db, decay.astype(BF16), rhs, kdt, (q * e_c).astype(BF16)


def _chain_solve(chains):
    hd = HEAD_DIM
    mats = [(kk * db.astype(F32)).astype(BF16) for kk, _, db, _, _, _, _ in chains]
    ts = _inv_unit(mats)
    wus = [_dot(t, ch[4]) for t, ch in zip(ts, chains)]
    out = []
    for (kk, qk, db, di, rhs, kdt, qe), wu in zip(chains, wus):
        if qk is None:
            kn_ = _dot(kdt, wu)
            out.append((kn_[:, :hd], kn_[:, hd:], None, None))
        else:
            attn = (qk * di.astype(F32)).astype(BF16)
            x = _dot(jnp.concatenate([kdt, attn], axis=0), wu)
            out.append((x[:hd, :hd], x[:hd, hd:], qe.astype(F32) - x[hd:, :hd], x[hd:, hd:]))
    return out


def _delta_kernel(q_ref, k_ref, v_ref, kc_ref, vc_ref, kc2_ref, vc2_ref, cwq_ref, cwk_ref, cwv_ref,
                  cwk2_ref, cwv2_ref,
                  gcol_ref, grow_ref, gcolc_ref, growc_ref, z_ref, onorm_ref,
                  out_ref, kq_scr, n_scr, o_scr, s_ref, win_scr, stm_a, stg_a, str_a, stm_b, stg_b, str_b,
                  *, n_x, n_c):
    n_tot = n_x + n_c
    hd = HEAD_DIM
    head_shift = lax.rem(LANES - 4 * pl.program_id(1), LANES)
    par = lax.rem(pl.program_id(1), HEAD_PAIR)

    def prep(src_q, src_k, src_v, gcol_r, grow_r, items, n_seq, stage, cwk=cwk_ref, cwv=cwv_ref):
        st_m, st_g, st_r = stage
        for sl, c, pp, shift in items:
            r0 = pl.multiple_of(c * CHUNK, CHUNK)
            k = _conv_silu(src_k, cwk, win_scr.at[3 * sl], c, n_seq, _l2n)
            v = _conv_silu(src_v, cwv, win_scr.at[3 * sl + 1], c, n_seq)
            q = None if src_q is None else _conv_silu(src_q, cwq_ref, win_scr.at[3 * sl + 2], c, n_seq,
                                                      lambda a: _l2n(a) * (hd ** -0.5))
            kt = k.T
            st_m[sl, 0] = k.astype(BF16)
            st_m[sl, 1] = kt.astype(BF16)
            if q is not None:
                st_m[sl, 2] = q.astype(BF16)
            gcol = pltpu.roll(gcol_r[0, pl.ds(r0, CHUNK), :], shift, axis=1)[:, 0:4]
            grow = grow_r[0, pp, c]
            for d in (0, 1):
                db, di, rhs, kdt, qe = _chain_operands(q, k, kt, v, gcol, grow, d)
                st_g[sl, d, 0] = db
                st_g[sl, d, 1] = kdt
                st_r[sl, d] = rhs
                if q is not None:
                    st_g[sl, d, 2] = di
                    st_g[sl, d, 3] = qe

    def local(stage, with_q, items):
        st_m, st_g, st_r = stage
        chains = []
        for sl, c, pp, _ in items:
            if with_q:
                kq = jnp.dot(jnp.concatenate([st_m[sl, 0], st_m[sl, 2]], axis=0), st_m[sl, 1],
                             preferred_element_type=F32)
                kk, qk = kq[:CHUNK], kq[CHUNK:]
            else:
                kk, qk = jnp.dot(st_m[sl, 0], st_m[sl, 1], preferred_element_type=F32), None
            for d in (0, 1):
                chains.append((kk, qk, st_g[sl, d, 0], st_g[sl, d, 2] if with_q else None, st_r[sl, d],
                               st_g[sl, d, 1], st_g[sl, d, 3] if with_q else None))
        res = _chain_solve(chains)
        for j, (_, c, pp, out_slot) in enumerate(items):
            r0 = pl.multiple_of(c * CHUNK, CHUNK)
            for d in (0, 1):
                kmat, nmat, qmat, omat = res[2 * j + d]
                idx = (pp * 2 + d) * n_tot + out_slot
                kq_scr[idx, 0:hd, :] = kmat.astype(BF16)
                n_scr[idx] = nmat.astype(BF16)
                if qmat is not None:
                    kq_scr[idx, hd:, :] = qmat.astype(BF16)
                    o_scr[pp * 2 + d, pl.ds(r0, CHUNK), :] = omat

    u_c = min(LOCAL_UNROLL, n_c)
    u_x = min(LOCAL_UNROLL, n_x)
    g_c, g_x = n_c // u_c, n_x // u_x
    stages = ((stm_a, stg_a, str_a), (stm_b, stg_b, str_b))

    def prep_x(g, stage):
        prep(q_ref, k_ref, v_ref, gcol_ref, grow_ref,
             [(j, g * u_x + j, par, head_shift) for j in range(u_x)], n_x, stage)

    def local_x(g, stage):
        local(stage, True, [(j, g * u_x + j, par, n_c + g * u_x + j) for j in range(u_x)])

    if g_c == 1 and HEAD_PAIR * n_c <= LOCAL_UNROLL:
        @pl.when(par == 0)
        def _context_pair():
            items = []
            for pp in range(HEAD_PAIR):
                src = (kc_ref, vc_ref, cwk_ref, cwv_ref) if pp == 0 else (kc2_ref, vc2_ref, cwk2_ref, cwv2_ref)
                shift = lax.rem(LANES - 4 * (pl.program_id(1) + pp), LANES)
                prep(None, src[0], src[1], gcolc_ref, growc_ref,
                     [(pp * n_c + j, j, pp, shift) for j in range(n_c)], n_c, stages[0], src[2], src[3])
                items += [(pp * n_c + j, j, pp, j) for j in range(n_c)]
            local(stages[0], False, items)
            prep_x(0, stages[1])

        @pl.when(par != 0)
        def _latent_only():
            prep_x(0, stages[1])
    else:
        def prep_c(g, stage):
            prep(None, kc_ref, vc_ref, gcolc_ref, growc_ref,
                 [(j, g * u_c + j, par, head_shift) for j in range(u_c)], n_c, stage)

        prep_c(0, stages[0])
        for g in range(g_c):
            local(stages[g % 2], False, [(j, g * u_c + j, par, g * u_c + j) for j in range(u_c)])
            if g + 1 < g_c:
                prep_c(g + 1, stages[(g + 1) % 2])
            else:
                prep_x(0, stages[(g + 1) % 2])

    if g_x % 2 == 0:
        def local_pair(p, carry):
            for h in (0, 1):
                g = 2 * p + h
                prep_x(jnp.minimum(g + 1, g_x - 1), stages[(g_c + h + 1) % 2])
                local_x(g, stages[(g_c + h) % 2])
            return carry

        lax.fori_loop(0, g_x // 2, local_pair, 0)
    else:
        for g in range(g_x):
            local_x(g, stages[(g_c + g) % 2])
            if g + 1 < g_x:
                prep_x(g + 1, stages[(g_c + g + 1) % 2])
    @pl.when(par == HEAD_PAIR - 1)
    def _sequential():
        s_ref[...] = jnp.zeros_like(s_ref)
        chains = [(pp, d) for pp in range(HEAD_PAIR) for d in (0, 1)]

        def state_decay(grow, d):
            tot = grow[2 + d:3 + d, CHUNK - 1:CHUNK] if d == 0 else grow[2 + d:3 + d, 0:1]
            return jnp.exp(tot)

        def seq_ctx(i, carry):
            for pp, d in chains:
                c = i if d == 0 else n_c - 1 - i
                idx = (pp * 2 + d) * n_tot + c
                s = s_ref[pp * 2 + d]
                ks = jnp.dot(kq_scr[idx, 0:hd, :], s.astype(BF16), preferred_element_type=F32)
                s_ref[pp * 2 + d] = (s * state_decay(growc_ref[0, pp, c], d) - ks
                                     + n_scr[idx].astype(F32))
            return carry

        def finish(pp, r0, o):
            lanes = slice(pp * hd, (pp + 1) * hd)
            on = o * lax.rsqrt(jnp.mean(o * o, axis=-1, keepdims=True) + NORM_EPS) * onorm_ref[...]
            zg = _silu(z_ref[0, pl.ds(r0, CHUNK), lanes].astype(F32))
            out_ref[0, pl.ds(r0, CHUNK), lanes] = (on * zg).astype(out_ref.dtype)

        def seq_x(i, carry, second_visit):
            for pp, d in chains:
                c = i if d == 0 else n_x - 1 - i
                r0 = pl.multiple_of(c * CHUNK, CHUNK)
                idx = (pp * 2 + d) * n_tot + n_c + c
                s = s_ref[pp * 2 + d]
                kqs = jnp.dot(kq_scr[idx], s.astype(BF16), preferred_element_type=F32)
                o = o_scr[pp * 2 + d, pl.ds(r0, CHUNK), :] + kqs[hd:, :]
                if second_visit:
                    finish(pp, r0, o + o_scr[pp * 2 + 1 - d, pl.ds(r0, CHUNK), :])
                else:
                    o_scr[pp * 2 + d, pl.ds(r0, CHUNK), :] = o
                s_ref[pp * 2 + d] = (s * state_decay(grow_ref[0, pp, c], d) - kqs[0:hd, :]
                                     + n_scr[idx].astype(F32))
            return carry

        lax.fori_loop(0, n_c, seq_ctx, 0)
        half = (n_x + 1) // 2
        lax.fori_loop(0, half, functools.partial(seq_x, second_visit=False), 0)
        if n_x % 2:
            r_mid = (half - 1) * CHUNK
            for pp in range(HEAD_PAIR):
                finish(pp, r_mid, o_scr[pp * 2, r_mid:r_mid + CHUNK, :] + o_scr[pp * 2 + 1, r_mid:r_mid + CHUNK, :])
        lax.fori_loop(half, n_x, functools.partial(seq_x, second_visit=True), 0)


def _row_gates(grow, n_heads):
    b, _, t = grow.shape
    return jnp.transpose(grow.reshape(b, n_heads, 4, t // CHUNK, CHUNK), (0, 1, 3, 2, 4))


def _delta_mixer(qkv, qkv_c, gcol, grow, gcolc, growc, z, conv_qkv, o_norm, n_heads):
    b, t, _ = qkv.shape
    tc = qkv_c.shape[1]
    n_x, n_c = t // CHUNK, tc // CHUNK
    hd = HEAD_DIM
    assert CHUNK == hd and t % CHUNK == 0 and tc % CHUNK == 0
    width = conv_qkv.shape[0]
    grow = _row_gates(grow, n_heads)
    growc = _row_gates(growc, n_heads)
    kern = functools.partial(_delta_kernel, n_x=n_x, n_c=n_c)
    nh = n_heads

    def colblk(off, ahead=0):
        return lambda i, h: (i, 0, off + jnp.minimum(h + ahead, nh - 1))

    def cwblk(off, ahead=0):
        return lambda i, h: (0, off + jnp.minimum(h + ahead, nh - 1))

    assert nh % HEAD_PAIR == 0
    pair5 = lambda i, h: (i, h // HEAD_PAIR, 0, 0, 0)
    pair3 = lambda i, h: (i, 0, h // HEAD_PAIR)
    return pl.pallas_call(
        kern,
        grid=(b, nh),
        in_specs=[pl.BlockSpec((1, t, hd), colblk(0)),
                  pl.BlockSpec((1, t, hd), colblk(nh)),
                  pl.BlockSpec((1, t, hd), colblk(2 * nh)),
                  pl.BlockSpec((1, tc, hd), colblk(nh)),
                  pl.BlockSpec((1, tc, hd), colblk(2 * nh)),
                  pl.BlockSpec((1, tc, hd), colblk(nh, 1)),
                  pl.BlockSpec((1, tc, hd), colblk(2 * nh, 1)),
                  pl.BlockSpec((width, hd), cwblk(0)),
                  pl.BlockSpec((width, hd), cwblk(nh)),
                  pl.BlockSpec((width, hd), cwblk(2 * nh)),
                  pl.BlockSpec((width, hd), cwblk(nh, 1)),
                  pl.BlockSpec((width, hd), cwblk(2 * nh, 1)),
                  pl.BlockSpec((1, t, LANES), lambda i, h: (i, 0, 0)),
                  pl.BlockSpec((1, HEAD_PAIR, n_x, 4, CHUNK), pair5),
                  pl.BlockSpec((1, tc, LANES), lambda i, h: (i, 0, 0)),
                  pl.BlockSpec((1, HEAD_PAIR, n_c, 4, CHUNK), pair5),
                  pl.BlockSpec((1, t, HEAD_PAIR * hd), pair3),
                  pl.BlockSpec((1, hd), lambda i, h: (0, 0))],
        out_specs=pl.BlockSpec((1, t, HEAD_PAIR * hd), pair3),
        out_shape=jax.ShapeDtypeStruct((b, t, nh * hd), BF16),
        scratch_shapes=[pltpu.VMEM((2 * HEAD_PAIR * (n_x + n_c), 2 * hd, hd), BF16),
                        pltpu.VMEM((2 * HEAD_PAIR * (n_x + n_c), hd, hd), BF16),
                        pltpu.VMEM((2 * HEAD_PAIR, t, hd), F32),
                        pltpu.VMEM((2 * HEAD_PAIR, hd, hd), F32),
                        pltpu.VMEM((3 * LOCAL_UNROLL, CHUNK + 2 * CONV_HALO, hd), F32),
                        pltpu.VMEM((LOCAL_UNROLL, 3, CHUNK, hd), BF16),
                        pltpu.VMEM((LOCAL_UNROLL, 2, 4, CHUNK, hd), BF16),
                        pltpu.VMEM((LOCAL_UNROLL, 2, CHUNK, 2 * hd), BF16),
                        pltpu.VMEM((LOCAL_UNROLL, 3, CHUNK, hd), BF16),
                        pltpu.VMEM((LOCAL_UNROLL, 2, 4, CHUNK, hd), BF16),
                        pltpu.VMEM((LOCAL_UNROLL, 2, CHUNK, 2 * hd), BF16)],
        compiler_params=_params("parallel", "arbitrary"),
        name="delta_mixer",
    )(qkv, qkv, qkv, qkv_c, qkv_c, qkv_c, qkv_c, conv_qkv, conv_qkv, conv_qkv, conv_qkv, conv_qkv,
      gcol, grow, gcolc, growc, z, o_norm.reshape(1, hd))


POOL_UNROLL = 4


def _pool_kernel(u_ref, band_ref, icnt_ref, w_ref, sc_ref, out_ref, cs, *, rows, blk):
    g = pl.program_id(1)
    left = lax.shift_left(jnp.int32(1), g)
    right = left - 1
    gw = GRID_W
    gd = cs.shape[1]
    t = rows * gw

    cs[0:gw, :] = jnp.zeros((gw, gd), F32)

    def prefix(r, acc):
        acc = acc + u_ref[0, pl.ds(pl.multiple_of(r * gw, gw), gw), :].astype(F32)
        cs[pl.ds(pl.multiple_of((r + 1) * gw, gw), gw), :] = acc
        return acc

    lax.fori_loop(0, rows, prefix, jnp.zeros((gw, gd), F32))

    def row_mean(r):
        lo = jnp.maximum(r - left, 0)
        hi = jnp.minimum(r + right + 1, rows)
        tot = cs[pl.ds(pl.multiple_of(hi * gw, gw), gw), :] - cs[pl.ds(pl.multiple_of(lo * gw, gw), gw), :]
        return tot / jnp.full((gw, gd), hi - lo, jnp.int32).astype(F32)

    def col_body(i, carry):
        blocks = [i * POOL_UNROLL + j for j in range(POOL_UNROLL)]
        starts = [pl.multiple_of(bi * blk, blk) for bi in blocks]
        rs = [jnp.concatenate([row_mean(bi * (blk // gw) + rr) for rr in range(blk // gw)], axis=0)
              for bi in blocks]
        ms = [jnp.dot(band_ref[0], r.astype(BF16), preferred_element_type=F32) for r in rs]
        ds = [(m * icnt_ref[0] - u_ref[0, pl.ds(r0, blk), :].astype(F32)).astype(BF16)
              for m, r0 in zip(ms, starts)]
        ys = [jnp.dot(dlt, w_ref[0], preferred_element_type=F32) for dlt in ds]
        for y, r0 in zip(ys, starts):
            out_ref[0, pl.ds(r0, blk), :] = (y * sc_ref[0]).astype(out_ref.dtype)
        return carry

    lax.fori_loop(0, t // (blk * POOL_UNROLL), col_body, 0)


def _pool_consts(blk):
    band = np.zeros((len(POOL_WINDOWS), blk, blk), np.float32)
    icnt = np.zeros((len(POOL_WINDOWS), blk, HEAD_DIM), np.float32)
    for gi, w in enumerate(POOL_WINDOWS):
        left = w // 2
        right = w - 1 - left
        for i in range(blk):
            base, col = (i // GRID_W) * GRID_W, i % GRID_W
            lo, hi = max(col - left, 0), min(col + right, GRID_W - 1)
            band[gi, i, base + lo:base + hi + 1] = 1.0
            icnt[gi, i, :] = 1.0 / (hi - lo + 1)
    return jnp.asarray(band, BF16), jnp.asarray(icnt)


def _pool_mixer(u, pool_w, pool_scale):
    b, t, p = u.shape
    n_g, gd = pool_w.shape[0], pool_w.shape[1]
    rows = t // GRID_W
    blk = 4 * GRID_W
    assert t % (blk * POOL_UNROLL) == 0
    band, icnt = _pool_consts(blk)
    kern = functools.partial(_pool_kernel, rows=rows, blk=blk)
    grp = lambda i, g: (g, 0, 0)
    return pl.pallas_call(
        kern,
        grid=(b, n_g),
        in_specs=[pl.BlockSpec((1, t, gd), lambda i, g: (i, 0, g)),
                  pl.BlockSpec((1, blk, blk), grp),
                  pl.BlockSpec((1, blk, gd), grp),
                  pl.BlockSpec((1, gd, gd), grp),
                  pl.BlockSpec((1, 1, gd), grp)],
        out_specs=pl.BlockSpec((1, t, gd), lambda i, g: (i, 0, g)),
        out_shape=jax.ShapeDtypeStruct((b, t, p), BF16),
        scratch_shapes=[pltpu.VMEM((t + GRID_W, gd), F32)],
        compiler_params=_params("parallel", "parallel"),
        name="pool_mixer",
    )(u, band, icnt, pool_w.astype(BF16), pool_scale.reshape(n_g, 1, gd))


def _rms(a, g):
    return a * lax.rsqrt(jnp.mean(a * a, axis=-1, keepdims=True) + NORM_EPS) * g


def _outproj_kernel(dn_ref, pool_ref, x_ref, w1_ref, w2_ref, gt_ref, sc_ref, sh_ref,
                    gpost_ref, gpre_ref, x1_ref, h_ref):
    y = (jnp.dot(dn_ref[0], w1_ref[...], preferred_element_type=F32)
         + jnp.dot(pool_ref[0], w2_ref[...], preferred_element_type=F32))
    x1 = x_ref[0] + gt_ref[0] * _rms(y, gpost_ref[...])
    x1_ref[0] = x1
    h_ref[0] = (_rms(x1, gpre_ref[...]) * (1.0 + sc_ref[0]) + sh_ref[0]).astype(h_ref.dtype)


def _out_projection(dn, pool, x, w_out, gt1, sc2, sh2, g_post, g_pre, tm):
    b, t, d = x.shape
    wd = dn.shape[-1]
    w1 = w_out[:wd].astype(BF16)
    w2 = w_out[wd:].astype(BF16)
    row = lambda i, j: (i, j, 0)
    per_b = lambda i, j: (i, 0, 0)
    const = lambda i, j: (0, 0)
    return pl.pallas_call(
        _outproj_kernel,
        grid=(b, t // tm),
        in_specs=[pl.BlockSpec((1, tm, wd), row),
                  pl.BlockSpec((1, tm, pool.shape[-1]), row),
                  pl.BlockSpec((1, tm, d), row),
                  pl.BlockSpec(w1.shape, const),
                  pl.BlockSpec(w2.shape, const),
                  pl.BlockSpec((1, 1, d), per_b),
                  pl.BlockSpec((1, 1, d), per_b),
                  pl.BlockSpec((1, 1, d), per_b),
                  pl.BlockSpec((1, d), const),
                  pl.BlockSpec((1, d), const)],
        out_specs=[pl.BlockSpec((1, tm, d), row), pl.BlockSpec((1, tm, d), row)],
        out_shape=[jax.ShapeDtypeStruct((b, t, d), F32), jax.ShapeDtypeStruct((b, t, d), BF16)],
        compiler_params=_params("parallel", "parallel"),
        name="out_projection",
    )(dn, pool, x, w1, w2, gt1, sc2, sh2, g_post.reshape(1, d), g_pre.reshape(1, d))


FFN_HALO = 16


def _ffn_kernel(h_ref, hp_ref, hn_ref, x1_ref, wgu_ref, cw_ref, wd_ref, gt_ref, gpost_ref,
                out_ref, hext, act, *, fc, n_fc):
    j = pl.program_id(1)
    nj = pl.num_programs(1)
    tm = h_ref.shape[1]
    hal = FFN_HALO
    dff = fc * n_fc
    hext[0:hal, :] = jnp.where(j > 0, hp_ref[0], jnp.zeros_like(hp_ref[0]))
    hext[hal:hal + tm, :] = h_ref[0]
    hext[hal + tm:, :] = jnp.where(j < nj - 1, hn_ref[0], jnp.zeros_like(hn_ref[0]))
    for c in range(n_fc):
        gate = jnp.dot(hext[...], wgu_ref[:, c * fc:(c + 1) * fc], preferred_element_type=F32)
        up = jnp.dot(h_ref[0], wgu_ref[:, dff + c * fc:dff + (c + 1) * fc], preferred_element_type=F32)
        cw = cw_ref[:, c * fc:(c + 1) * fc]
        conv = (gate[hal - 1:hal - 1 + tm] * cw[0:1] + gate[hal:hal + tm] * cw[1:2]
                + gate[hal + 1:hal + 1 + tm] * cw[2:3])
        act[:, c * fc:(c + 1) * fc] = (_silu(conv) * up).astype(BF16)
    y = jnp.dot(act[...], wd_ref[...], preferred_element_type=F32)
    out_ref[0] = x1_ref[0] + gt_ref[0] * _rms(y, gpost_ref[...])


def _conv_ffn(h, x1, w_up, conv_ffn, w_down, gt2, g_post, tm, fc):
    b, t, d = x1.shape
    dff = w_down.shape[0]
    n_fc = dff // fc
    wgu = w_up.astype(BF16)
    hal = FFN_HALO
    nb = tm // hal
    last = t // hal - 1
    kern = functools.partial(_ffn_kernel, fc=fc, n_fc=n_fc)
    row = lambda i, j: (i, j, 0)
    per_b = lambda i, j: (i, 0, 0)
    const = lambda i, j: (0, 0)
    return pl.pallas_call(
        kern,
        grid=(b, t // tm),
        in_specs=[pl.BlockSpec((1, tm, d), row),
                  pl.BlockSpec((1, hal, d), lambda i, j: (i, jnp.maximum(j * nb - 1, 0), 0)),
                  pl.BlockSpec((1, hal, d), lambda i, j: (i, jnp.minimum((j + 1) * nb, last), 0)),
                  pl.BlockSpec((1, tm, d), row),
                  pl.BlockSpec(wgu.shape, const),
                  pl.BlockSpec(conv_ffn.shape, const),
                  pl.BlockSpec(w_down.shape, const),
                  pl.BlockSpec((1, 1, d), per_b),
                  pl.BlockSpec((1, d), const)],
        out_specs=pl.BlockSpec((1, tm, d), row),
        out_shape=jax.ShapeDtypeStruct((b, t, d), F32),
        scratch_shapes=[pltpu.VMEM((tm + 2 * hal, d), BF16), pltpu.VMEM((tm, dff), BF16)],
        compiler_params=_params("parallel", "parallel"),
        name="conv_ffn",
    )(h, h, h, x1, wgu, conv_ffn, w_down.astype(BF16), gt2, g_post.reshape(1, d))


def _layer(x, c, ctx, c_ctx, w_mod, b_mod, g_pre_mix, g_post_mix, g_pre_ffn, g_post_ffn,
           w_in, conv_qkv, a_log, dt_bias, o_norm, pool_w, pool_scale, w_out, w_up,
           conv_ffn, w_down):
    b, t, d = x.shape
    n_heads = a_log.shape[-1]
    w = n_heads * HEAD_DIM
    n_gate = 4 * n_heads

    rows = pl.cdiv(b + 1, SUBLANES) * SUBLANES
    c_all = jnp.zeros((rows, d), F32).at[:b].set(c).at[b].set(c_ctx)
    m = _modulation(c_all, w_mod, b_mod)
    sh1, sc1, gt1, sh2, sc2, gt2 = [m[:b, None, i * d:(i + 1) * d] for i in range(N_MOD)]
    csh1 = jnp.broadcast_to(m[b, 0 * d:1 * d], (b, 1, d))
    csc1 = jnp.broadcast_to(m[b, 1 * d:2 * d], (b, 1, d))

    off_g = 4 * w
    w_gate = jnp.transpose(w_in[:, off_g:off_g + n_gate].reshape(d, 4, n_heads), (0, 2, 1)).reshape(d, n_gate)
    w_all = jnp.concatenate([w_in[:, :off_g], w_in[:, off_g + n_gate:],
                             jnp.pad(w_gate, ((0, 0), (0, LANES - n_gate)))], axis=1).astype(BF16)
    tm = min(ROW_TILE, t)
    tp = min(PROJ_TILE, t)
    qkv, z, pool_u, gcol, grow = _in_projection(x, sc1, sh1, g_pre_mix, w_all, a_log, dt_bias, n_heads, tp)
    qkv_c, _, _, gcolc, growc = _in_projection(ctx, csc1, csh1, g_pre_mix, w_all, a_log, dt_bias,
                                                n_heads, ctx.shape[1])
    dn = _delta_mixer(qkv, qkv_c, gcol, grow, gcolc, growc, z, conv_qkv, o_norm, n_heads)
    pool = _pool_mixer(pool_u, pool_w, pool_scale)
    x1, h2 = _out_projection(dn, pool, x, w_out, gt1, sc2, sh2, g_post_mix, g_pre_ffn, tp)
    return _conv_ffn(h2, x1, w_up, conv_ffn, w_down, gt2, g_post_ffn, tm, FFN_CHUNK)


def kernel(x, c, ctx, c_ctx, w_mod, b_mod, g_pre_mix, g_post_mix, g_pre_ffn, g_post_ffn, w_in,
           conv_qkv, a_log, dt_bias, o_norm, pool_w, pool_scale, w_out, w_up, conv_ffn, w_down):
    depth = w_mod.shape[0]
    assert depth == 1, "context-stream update between layers is not implemented"
    return _layer(x, c, ctx, c_ctx, w_mod[0], b_mod[0], g_pre_mix[0], g_post_mix[0],
                  g_pre_ffn[0], g_post_ffn[0], w_in[0], conv_qkv[0], a_log[0], dt_bias[0],
                  o_norm[0], pool_w[0], pool_scale[0], w_out[0], w_up[0], conv_ffn[0], w_down[0])
```

```python
import functools

import numpy as np
import jax
import jax.numpy as jnp
from jax import lax
from jax.experimental import pallas as pl
from jax.experimental.pallas import tpu as pltpu

NORM_EPS = 1e-6
GRID_W = 64
POOL_WINDOWS = (2, 4, 8, 16)
N_MOD = 6
CHUNK = 128
HEAD_DIM = 128
VMEM_LIMIT = 56 * 1024 * 1024
SUBLANES = 8
ROW_TILE = 512
PROJ_TILE = 1024
FFN_CHUNK = 256

F32 = jnp.float32
BF16 = jnp.bfloat16


def _silu(a):
    return a * (1.0 / (1.0 + jnp.exp(-a)))


def _dot(a, b):
    return jnp.dot(a.astype(BF16), b.astype(BF16), preferred_element_type=F32)


def _bdot(a, b):
    return jnp.dot(a, b, preferred_element_type=F32).astype(BF16)


def _split3(a):
    hi = a.astype(BF16)
    r1 = a - hi.astype(F32)
    mid = r1.astype(BF16)
    lo = (r1 - mid.astype(F32)).astype(BF16)
    return hi, mid, lo


def _dot_f32(a, b):
    ah, am, al = _split3(a)
    bh, bm, bl = _split3(b)
    d = lambda x, y: jnp.dot(x, y, preferred_element_type=F32)
    return ((d(al, bh) + d(ah, bl)) + d(am, bm)) + (d(am, bh) + d(ah, bm)) + d(ah, bh)


def _params(*sem):
    return pltpu.CompilerParams(dimension_semantics=sem, vmem_limit_bytes=VMEM_LIMIT)


def _mod_kernel(c_ref, w_ref, b_ref, o_ref):
    o_ref[...] = _dot_f32(_silu(c_ref[...]), w_ref[...]) + b_ref[...]


def _modulation(c_all, w_mod, b_mod):
    r, d = c_all.shape
    n = w_mod.shape[1]
    tn = 512
    return pl.pallas_call(
        _mod_kernel,
        grid=(n // tn,),
        in_specs=[pl.BlockSpec((r, d), lambda j: (0, 0)),
                  pl.BlockSpec((d, tn), lambda j: (0, j)),
                  pl.BlockSpec((1, tn), lambda j: (0, j))],
        out_specs=pl.BlockSpec((r, tn), lambda j: (0, j)),
        out_shape=jax.ShapeDtypeStruct((r, n), F32),
        compiler_params=_params("parallel"),
        name="modulation",
    )(c_all, w_mod, b_mod.reshape(1, n))


LANES = 128


def _inproj_kernel(x_ref, sc_ref, sh_ref, g_ref, w_ref, alog_ref, dtb_ref, tri_ref,
                   qkv_ref, z_ref, pool_ref, gcol_ref, grow_ref, *, n_qkv, n_z, n_pool, n_heads):
    x = x_ref[0]
    xn = x * lax.rsqrt(jnp.mean(x * x, axis=-1, keepdims=True) + NORM_EPS) * g_ref[...]
    h = (xn * (1.0 + sc_ref[0]) + sh_ref[0]).astype(BF16)
    o1 = n_qkv
    o2 = o1 + n_z
    o3 = o2 + n_pool
    pg = jnp.dot(h, w_ref[:, o3:], preferred_element_type=F32)
    qkv_ref[0] = jnp.dot(h, w_ref[:, 0:o1], preferred_element_type=F32).astype(qkv_ref.dtype)
    beta = 1.0 / (1.0 + jnp.exp(-pg))
    a = pg + dtb_ref[...]
    softplus = jnp.maximum(a, 0.0) + jnp.log(1.0 + jnp.exp(-jnp.abs(a)))
    g = -jnp.exp(alog_ref[...]) * softplus
    tm = g.shape[0]
    n_ch = tm // CHUNK
    rows = 4 * n_heads
    beta_t = [beta[c * CHUNK:(c + 1) * CHUNK, :].T[0:rows, :] for c in range(n_ch)]
    g_t = [g[c * CHUNK:(c + 1) * CHUNK, :].T[0:rows, :] for c in range(n_ch)]
    parts = _split3(jnp.concatenate(g_t, axis=0))
    r = jnp.dot(jnp.concatenate(parts, axis=0), tri_ref[...], preferred_element_type=F32)
    z_ref[0] = jnp.dot(h, w_ref[:, o1:o2], preferred_element_type=F32).astype(z_ref.dtype)
    pool_ref[0] = jnp.dot(h, w_ref[:, o2:o3], preferred_element_type=F32).astype(pool_ref.dtype)
    m = n_ch * rows
    pre_all = r[0:m] + r[m:2 * m] + r[2 * m:3 * m]
    kind = lax.broadcasted_iota(jnp.int32, (rows, CHUNK), 0) & 3
    zpad = jnp.zeros((LANES - rows, CHUNK), F32)
    for c in range(n_ch):
        pre = pre_all[c * rows:(c + 1) * rows, :]
        suf = pre[:, CHUNK - 1:CHUNK] - pre + g_t[c]
        row = jnp.where(kind < 2, beta_t[c], jnp.where(kind == 2, pre, suf))
        grow_ref[0, :, c * CHUNK:(c + 1) * CHUNK] = row
        gcol_ref[0, c * CHUNK:(c + 1) * CHUNK, :] = jnp.concatenate([row, zpad], axis=0).T


def _in_projection(x, sc, sh, g_pre, w_all, a_log, dt_bias, n_heads, tm):
    b, t, d = x.shape
    w = n_heads * HEAD_DIM
    n_qkv, n_z = 3 * w, w
    n_pool = w_all.shape[1] - n_qkv - n_z - LANES
    assert CHUNK == LANES
    tri = jnp.asarray(np.triu(np.ones((CHUNK, CHUNK), np.float32)), BF16)
    lane_par = jnp.zeros((2, n_heads, 4), F32)
    lane_par = lane_par.at[0, :, 2:].set(a_log.T).at[1, :, 2:].set(dt_bias.T)
    lane_par = jnp.pad(lane_par.reshape(2, 4 * n_heads), ((0, 0), (0, LANES - 4 * n_heads)))
    kern = functools.partial(_inproj_kernel, n_qkv=n_qkv, n_z=n_z, n_pool=n_pool, n_heads=n_heads)
    row = lambda i, j: (i, j, 0)
    per_b = lambda i, j: (i, 0, 0)
    const = lambda i, j: (0, 0)
    return pl.pallas_call(
        kern,
        grid=(b, t // tm),
        in_specs=[pl.BlockSpec((1, tm, d), row),
                  pl.BlockSpec((1, 1, d), per_b),
                  pl.BlockSpec((1, 1, d), per_b),
                  pl.BlockSpec((1, d), const),
                  pl.BlockSpec(w_all.shape, const),
                  pl.BlockSpec((1, LANES), const),
                  pl.BlockSpec((1, LANES), const),
                  pl.BlockSpec((CHUNK, CHUNK), const)],
        out_specs=[pl.BlockSpec((1, tm, n_qkv), row),
                   pl.BlockSpec((1, tm, n_z), row),
                   pl.BlockSpec((1, tm, n_pool), row),
                   pl.BlockSpec((1, tm, LANES), row),
                   pl.BlockSpec((1, 4 * n_heads, tm), lambda i, j: (i, 0, j))],
        out_shape=[jax.ShapeDtypeStruct((b, t, n_qkv), BF16),
                   jax.ShapeDtypeStruct((b, t, n_z), BF16),
                   jax.ShapeDtypeStruct((b, t, n_pool), BF16),
                   jax.ShapeDtypeStruct((b, t, LANES), F32),
                   jax.ShapeDtypeStruct((b, 4 * n_heads, t), F32)],
        compiler_params=_params("parallel", "parallel"),
        name="in_projection",
    )(x, sc, sh, g_pre.reshape(1, d), w_all, lane_par[0:1], lane_par[1:2], tri)


CONV_HALO = 16


def _conv_silu(ref, cw_ref, win, c, n_chunks, post=None):
    hal = CONV_HALO
    r0 = pl.multiple_of(c * CHUNK, CHUNK)
    p0 = pl.multiple_of(jnp.maximum(r0 - hal, 0), hal)
    n0 = pl.multiple_of(jnp.minimum(r0 + CHUNK, n_chunks * CHUNK - hal), hal)
    win[0:hal, :] = jnp.where(c > 0, ref[0, pl.ds(p0, hal), :].astype(F32), 0.0)
    win[hal:hal + CHUNK, :] = ref[0, pl.ds(r0, CHUNK), :].astype(F32)
    win[hal + CHUNK:, :] = jnp.where(c < n_chunks - 1, ref[0, pl.ds(n0, hal), :].astype(F32), 0.0)
    width = cw_ref.shape[0]
    pad = width // 2
    n_t = CHUNK // SUBLANES
    taps = [cw_ref[j:j + 1, :] for j in range(width)]
    xs = [win[pl.ds(hal - pad + s, n_t, stride=SUBLANES), :] for s in range(SUBLANES + width - 1)]
    outs = []
    for r in range(SUBLANES):
        acc = xs[r] * taps[0]
        for j in range(1, width):
            acc = acc + xs[r + j] * taps[j]
        acc = _silu(acc)
        outs.append(acc if post is None else post(acc))
    for r in range(SUBLANES):
        win[pl.ds(hal + r, n_t, stride=SUBLANES), :] = outs[r]
    return win[hal:hal + CHUNK, :]


def _l2n(a):
    return a * lax.rsqrt(jnp.sum(a * a, axis=-1, keepdims=True) + NORM_EPS)


INV_BASE = 8
LOCAL_UNROLL = 4
HEAD_PAIR = 2


def _inv_unit(mats):
    n = mats[0].shape[0]
    ii = lax.broadcasted_iota(jnp.int32, (n, n), 0)
    jj = lax.broadcasted_iota(jnp.int32, (n, n), 1)

    def same_block(bits):
        return lax.shift_right_logical(ii, bits) == lax.shift_right_logical(jj, bits)

    bits = int(np.log2(INV_BASE))
    base = same_block(bits)
    eye = jnp.where(ii == jj, 1.0, 0.0).astype(BF16)
    zero = jnp.zeros((), BF16)
    abf = [a.astype(BF16) for a in mats]
    ms = [jnp.where(base, -a, zero) for a in abf]
    ps = [eye + m for m in ms]
    ms = [_bdot(m, m) for m in ms]
    for _ in range(bits - 2):
        pm = [_bdot(jnp.concatenate([p, m], axis=0), m) for p, m in zip(ps, ms)]
        ps = [p + x[:n] for p, x in zip(ps, pm)]
        ms = [x[n:] for x in pm]
    ps = [p + _bdot(p, m) for p, m in zip(ps, ms)]
    while (1 << bits) < n:
        lvl = same_block(bits + 1) & jnp.logical_not(same_block(bits))
        tmp = [_bdot(jnp.where(lvl, a, zero), p) for a, p in zip(abf, ps)]
        ps = [p - _bdot(p, t) for p, t in zip(ps, tmp)]
        bits += 1
    return ps


def _chain_operands(q, k, kt, v, gcol, grow, d):
    c = k.shape[0]
    ii = lax.broadcasted_iota(jnp.int32, (c, c), 0)
    jj = lax.broadcasted_iota(jnp.int32, (c, c), 1)
    incl = (ii >= jj) if d == 0 else (ii <= jj)
    strict = (ii > jj) if d == 0 else (ii < jj)
    beta_c = gcol[:, d:d + 1]
    gc_c = gcol[:, 2 + d:3 + d]
    gc_r = grow[2 + d:3 + d, :]
    glast = gc_r[:, c - 1:c] if d == 0 else gc_r[:, 0:1]
    decay = jnp.where(incl, jnp.exp(jnp.where(incl, gc_c - gc_r, 0.0)), 0.0)
    e_c = jnp.exp(gc_c)
    db = jnp.where(strict, decay * beta_c, 0.0).astype(BF16)
    rhs = jnp.concatenate([k * (beta_c * e_c), v * beta_c], axis=1).astype(BF16)
    kdt = (kt * jnp.exp(glast - gc_r)).astype(BF16)
    if q is None:
        return db, None, rhs, kdt, None
    return db, decay.astype(BF16), rhs, kdt, (q * e_c).astype(BF16)


def _chain_solve(chains):
    hd = HEAD_DIM
    mats = [(kk * db.astype(F32)).astype(BF16) for kk, _, db, _, _, _, _ in chains]
    ts = _inv_unit(mats)
    wus = [_dot(t, ch[4]) for t, ch in zip(ts, chains)]
    out = []
    for (kk, qk, db, di, rhs, kdt, qe), wu in zip(chains, wus):
        if qk is None:
            kn_ = _dot(kdt, wu)
            out.append((kn_[:, :hd], kn_[:, hd:], None, None))
        else:
            attn = (qk * di.astype(F32)).astype(BF16)
            x = _dot(jnp.concatenate([kdt, attn], axis=0), wu)
            out.append((x[:hd, :hd], x[:hd, hd:], qe.astype(F32) - x[hd:, :hd], x[hd:, hd:]))
    return out


def _delta_kernel(q_ref, k_ref, v_ref, kc_ref, vc_ref, kc2_ref, vc2_ref, cwq_ref, cwk_ref, cwv_ref,
                  cwk2_ref, cwv2_ref,
                  gcol_ref, grow_ref, gcolc_ref, growc_ref, z_ref, onorm_ref,
                  out_ref, kq_scr, n_scr, o_scr, s_ref, win_scr, stm_a, stg_a, str_a, stm_b, stg_b, str_b,
                  *, n_x, n_c):
    n_tot = n_x + n_c
    hd = HEAD_DIM
    head_shift = lax.rem(LANES - 4 * pl.program_id(1), LANES)
    par = lax.rem(pl.program_id(1), HEAD_PAIR)

    def prep(src_q, src_k, src_v, gcol_r, grow_r, items, n_seq, stage, cwk=cwk_ref, cwv=cwv_ref):
        st_m, st_g, st_r = stage
        for sl, c, pp, shift in items:
            r0 = pl.multiple_of(c * CHUNK, CHUNK)
            k = _conv_silu(src_k, cwk, win_scr.at[3 * sl], c, n_seq, _l2n)
            v = _conv_silu(src_v, cwv, win_scr.at[3 * sl + 1], c, n_seq)
            q = None if src_q is None else _conv_silu(src_q, cwq_ref, win_scr.at[3 * sl + 2], c, n_seq,
                                                      lambda a: _l2n(a) * (hd ** -0.5))
            kt = k.T
            st_m[sl, 0] = k.astype(BF16)
            st_m[sl, 1] = kt.astype(BF16)
            if q is not None:
                st_m[sl, 2] = q.astype(BF16)
            gcol = pltpu.roll(gcol_r[0, pl.ds(r0, CHUNK), :], shift, axis=1)[:, 0:4]
            grow = grow_r[0, pp, c]
            for d in (0, 1):
                db, di, rhs, kdt, qe = _chain_operands(q, k, kt, v, gcol, grow, d)
                st_g[sl, d, 0] = db
                st_g[sl, d, 1] = kdt
                st_r[sl, d] = rhs
                if q is not None:
                    st_g[sl, d, 2] = di
                    st_g[sl, d, 3] = qe

    def local(stage, with_q, items):
        st_m, st_g, st_r = stage
        chains = []
        for sl, c, pp, _ in items:
            if with_q:
                kq = jnp.dot(jnp.concatenate([st_m[sl, 0], st_m[sl, 2]], axis=0), st_m[sl, 1],
                             preferred_element_type=F32)
                kk, qk = kq[:CHUNK], kq[CHUNK:]
            else:
                kk, qk = jnp.dot(st_m[sl, 0], st_m[sl, 1], preferred_element_type=F32), None
            for d in (0, 1):
                chains.append((kk, qk, st_g[sl, d, 0], st_g[sl, d, 2] if with_q else None, st_r[sl, d],
                               st_g[sl, d, 1], st_g[sl, d, 3] if with_q else None))
        res = _chain_solve(chains)
        for j, (_, c, pp, out_slot) in enumerate(items):
            r0 = pl.multiple_of(c * CHUNK, CHUNK)
            for d in (0, 1):
                kmat, nmat, qmat, omat = res[2 * j + d]
                idx = (pp * 2 + d) * n_tot + out_slot
                kq_scr[idx, 0:hd, :] = kmat.astype(BF16)
                n_scr[idx] = nmat.astype(BF16)
                if qmat is not None:
                    kq_scr[idx, hd:, :] = qmat.astype(BF16)
                    o_scr[pp * 2 + d, pl.ds(r0, CHUNK), :] = omat

    u_c = min(LOCAL_UNROLL, n_c)
    u_x = min(LOCAL_UNROLL, n_x)
    g_c, g_x = n_c // u_c, n_x // u_x
    stages = ((stm_a, stg_a, str_a), (stm_b, stg_b, str_b))

    def prep_x(g, stage):
        prep(q_ref, k_ref, v_ref, gcol_ref, grow_ref,
             [(j, g * u_x + j, par, head_shift) for j in range(u_x)], n_x, stage)

    def local_x(g, stage):
        local(stage, True, [(j, g * u_x + j, par, n_c + g * u_x + j) for j in range(u_x)])

    if g_c == 1 and HEAD_PAIR * n_c <= LOCAL_UNROLL:
        @pl.when(par == 0)
        def _context_pair():
            items = []
            for pp in range(HEAD_PAIR):
                src = (kc_ref, vc_ref, cwk_ref, cwv_ref) if pp == 0 else (kc2_ref, vc2_ref, cwk2_ref, cwv2_ref)
                shift = lax.rem(LANES - 4 * (pl.program_id(1) + pp), LANES)
                prep(None, src[0], src[1], gcolc_ref, growc_ref,
                     [(pp * n_c + j, j, pp, shift) for j in range(n_c)], n_c, stages[0], src[2], src[3])
                items += [(pp * n_c + j, j, pp, j) for j in range(n_c)]
            local(stages[0], False, items)
            prep_x(0, stages[1])

        @pl.when(par != 0)
        def _latent_only():
            prep_x(0, stages[1])
    else:
        def prep_c(g, stage):
            prep(None, kc_ref, vc_ref, gcolc_ref, growc_ref,
                 [(j, g * u_c + j, par, head_shift) for j in range(u_c)], n_c, stage)

        prep_c(0, stages[0])
        for g in range(g_c):
            local(stages[g % 2], False, [(j, g * u_c + j, par, g * u_c + j) for j in range(u_c)])
            if g + 1 < g_c:
                prep_c(g + 1, stages[(g + 1) % 2])
            else:
                prep_x(0, stages[(g + 1) % 2])

    if g_x % 2 == 0:
        def local_pair(p, carry):
            for h in (0, 1):
                g = 2 * p + h
                prep_x(jnp.minimum(g + 1, g_x - 1), stages[(g_c + h + 1) % 2])
                local_x(g, stages[(g_c + h) % 2])
            return carry

        lax.fori_loop(0, g_x // 2, local_pair, 0)
    else:
        for g in range(g_x):
            local_x(g, stages[(g_c + g) % 2])
            if g + 1 < g_x:
                prep_x(g + 1, stages[(g_c + g + 1) % 2])
    @pl.when(par == HEAD_PAIR - 1)
    def _sequential():
        s_ref[...] = jnp.zeros_like(s_ref)
        chains = [(pp, d) for pp in range(HEAD_PAIR) for d in (0, 1)]

        def state_decay(grow, d):
            tot = grow[2 + d:3 + d, CHUNK - 1:CHUNK] if d == 0 else grow[2 + d:3 + d, 0:1]
            return jnp.exp(tot)

        def seq_ctx(i, carry):
            for pp, d in chains:
                c = i if d == 0 else n_c - 1 - i
                idx = (pp * 2 + d) * n_tot + c
                s = s_ref[pp * 2 + d]
                ks = jnp.dot(kq_scr[idx, 0:hd, :], s.astype(BF16), preferred_element_type=F32)
                s_ref[pp * 2 + d] = (s * state_decay(growc_ref[0, pp, c], d) - ks
                                     + n_scr[idx].astype(F32))
            return carry

        def finish(pp, r0, o):
            lanes = slice(pp * hd, (pp + 1) * hd)
            on = o * lax.rsqrt(jnp.mean(o * o, axis=-1, keepdims=True) + NORM_EPS) * onorm_ref[...]
            zg = _silu(z_ref[0, pl.ds(r0, CHUNK), lanes].astype(F32))
            out_ref[0, pl.ds(r0, CHUNK), lanes] = (on * zg).astype(out_ref.dtype)

        def seq_x(i, carry, second_visit):
            for pp, d in chains:
                c = i if d == 0 else n_x - 1 - i
                r0 = pl.multiple_of(c * CHUNK, CHUNK)
                idx = (pp * 2 + d) * n_tot + n_c + c
                s = s_ref[pp * 2 + d]
                kqs = jnp.dot(kq_scr[idx], s.astype(BF16), preferred_element_type=F32)
                o = o_scr[pp * 2 + d, pl.ds(r0, CHUNK), :] + kqs[hd:, :]
                if second_visit:
                    finish(pp, r0, o + o_scr[pp * 2 + 1 - d, pl.ds(r0, CHUNK), :])
                else:
                    o_scr[pp * 2 + d, pl.ds(r0, CHUNK), :] = o
                s_ref[pp * 2 + d] = (s * state_decay(grow_ref[0, pp, c], d) - kqs[0:hd, :]
                                     + n_scr[idx].astype(F32))
            return carry

        lax.fori_loop(0, n_c, seq_ctx, 0)
        half = (n_x + 1) // 2
        lax.fori_loop(0, half, functools.partial(seq_x, second_visit=False), 0)
        if n_x % 2:
            r_mid = (half - 1) * CHUNK
            for pp in range(HEAD_PAIR):
                finish(pp, r_mid, o_scr[pp * 2, r_mid:r_mid + CHUNK, :] + o_scr[pp * 2 + 1, r_mid:r_mid + CHUNK, :])
        lax.fori_loop(half, n_x, functools.partial(seq_x, second_visit=True), 0)


def _row_gates(grow, n_heads):
    b, _, t = grow.shape
    return jnp.transpose(grow.reshape(b, n_heads, 4, t // CHUNK, CHUNK), (0, 1, 3, 2, 4))


def _delta_mixer(qkv, qkv_c, gcol, grow, gcolc, growc, z, conv_qkv, o_norm, n_heads):
    b, t, _ = qkv.shape
    tc = qkv_c.shape[1]
    n_x, n_c = t // CHUNK, tc // CHUNK
    hd = HEAD_DIM
    assert CHUNK == hd and t % CHUNK == 0 and tc % CHUNK == 0
    width = conv_qkv.shape[0]
    grow = _row_gates(grow, n_heads)
    growc = _row_gates(growc, n_heads)
    kern = functools.partial(_delta_kernel, n_x=n_x, n_c=n_c)
    nh = n_heads

    def colblk(off, ahead=0):
        return lambda i, h: (i, 0, off + jnp.minimum(h + ahead, nh - 1))

    def cwblk(off, ahead=0):
        return lambda i, h: (0, off + jnp.minimum(h + ahead, nh - 1))

    assert nh % HEAD_PAIR == 0
    pair5 = lambda i, h: (i, h // HEAD_PAIR, 0, 0, 0)
    pair3 = lambda i, h: (i, 0, h // HEAD_PAIR)
    return pl.pallas_call(
        kern,
        grid=(b, nh),
        in_specs=[pl.BlockSpec((1, t, hd), colblk(0)),
                  pl.BlockSpec((1, t, hd), colblk(nh)),
                  pl.BlockSpec((1, t, hd), colblk(2 * nh)),
                  pl.BlockSpec((1, tc, hd), colblk(nh)),
                  pl.BlockSpec((1, tc, hd), colblk(2 * nh)),
                  pl.BlockSpec((1, tc, hd), colblk(nh, 1)),
                  pl.BlockSpec((1, tc, hd), colblk(2 * nh, 1)),
                  pl.BlockSpec((width, hd), cwblk(0)),
                  pl.BlockSpec((width, hd), cwblk(nh)),
                  pl.BlockSpec((width, hd), cwblk(2 * nh)),
                  pl.BlockSpec((width, hd), cwblk(nh, 1)),
                  pl.BlockSpec((width, hd), cwblk(2 * nh, 1)),
                  pl.BlockSpec((1, t, LANES), lambda i, h: (i, 0, 0)),
                  pl.BlockSpec((1, HEAD_PAIR, n_x, 4, CHUNK), pair5),
                  pl.BlockSpec((1, tc, LANES), lambda i, h: (i, 0, 0)),
                  pl.BlockSpec((1, HEAD_PAIR, n_c, 4, CHUNK), pair5),
                  pl.BlockSpec((1, t, HEAD_PAIR * hd), pair3),
                  pl.BlockSpec((1, hd), lambda i, h: (0, 0))],
        out_specs=pl.BlockSpec((1, t, HEAD_PAIR * hd), pair3),
        out_shape=jax.ShapeDtypeStruct((b, t, nh * hd), BF16),
        scratch_shapes=[pltpu.VMEM((2 * HEAD_PAIR * (n_x + n_c), 2 * hd, hd), BF16),
                        pltpu.VMEM((2 * HEAD_PAIR * (n_x + n_c), hd, hd), BF16),
                        pltpu.VMEM((2 * HEAD_PAIR, t, hd), F32),
                        pltpu.VMEM((2 * HEAD_PAIR, hd, hd), F32),
                        pltpu.VMEM((3 * LOCAL_UNROLL, CHUNK + 2 * CONV_HALO, hd), F32),
                        pltpu.VMEM((LOCAL_UNROLL, 3, CHUNK, hd), BF16),
                        pltpu.VMEM((LOCAL_UNROLL, 2, 4, CHUNK, hd), BF16),
                        pltpu.VMEM((LOCAL_UNROLL, 2, CHUNK, 2 * hd), BF16),
                        pltpu.VMEM((LOCAL_UNROLL, 3, CHUNK, hd), BF16),
                        pltpu.VMEM((LOCAL_UNROLL, 2, 4, CHUNK, hd), BF16),
                        pltpu.VMEM((LOCAL_UNROLL, 2, CHUNK, 2 * hd), BF16)],
        compiler_params=_params("parallel", "arbitrary"),
        name="delta_mixer",
    )(qkv, qkv, qkv, qkv_c, qkv_c, qkv_c, qkv_c, conv_qkv, conv_qkv, conv_qkv, conv_qkv, conv_qkv,
      gcol, grow, gcolc, growc, z, o_norm.reshape(1, hd))


POOL_UNROLL = 4


def _pool_kernel(u_ref, band_ref, icnt_ref, w_ref, sc_ref, out_ref, cs, *, rows, blk):
    g = pl.program_id(1)
    left = lax.shift_left(jnp.int32(1), g)
    right = left - 1
    gw = GRID_W
    gd = cs.shape[1]
    t = rows * gw

    cs[0:gw, :] = jnp.zeros((gw, gd), F32)

    def prefix(r, acc):
        acc = acc + u_ref[0, pl.ds(pl.multiple_of(r * gw, gw), gw), :].astype(F32)
        cs[pl.ds(pl.multiple_of((r + 1) * gw, gw), gw), :] = acc
        return acc

    lax.fori_loop(0, rows, prefix, jnp.zeros((gw, gd), F32))

    def row_mean(r):
        lo = jnp.maximum(r - left, 0)
        hi = jnp.minimum(r + right + 1, rows)
        tot = cs[pl.ds(pl.multiple_of(hi * gw, gw), gw), :] - cs[pl.ds(pl.multiple_of(lo * gw, gw), gw), :]
        return tot / jnp.full((gw, gd), hi - lo, jnp.int32).astype(F32)

    def col_body(i, carry):
        blocks = [i * POOL_UNROLL + j for j in range(POOL_UNROLL)]
        starts = [pl.multiple_of(bi * blk, blk) for bi in blocks]
        rs = [jnp.concatenate([row_mean(bi * (blk // gw) + rr) for rr in range(blk // gw)], axis=0)
              for bi in blocks]
        ms = [jnp.dot(band_ref[0], r.astype(BF16), preferred_element_type=F32) for r in rs]
        ds = [(m * icnt_ref[0] - u_ref[0, pl.ds(r0, blk), :].astype(F32)).astype(BF16)
              for m, r0 in zip(ms, starts)]
        ys = [jnp.dot(dlt, w_ref[0], preferred_element_type=F32) for dlt in ds]
        for y, r0 in zip(ys, starts):
            out_ref[0, pl.ds(r0, blk), :] = (y * sc_ref[0]).astype(out_ref.dtype)
        return carry

    lax.fori_loop(0, t // (blk * POOL_UNROLL), col_body, 0)


def _pool_consts(blk):
    band = np.zeros((len(POOL_WINDOWS), blk, blk), np.float32)
    icnt = np.zeros((len(POOL_WINDOWS), blk, HEAD_DIM), np.float32)
    for gi, w in enumerate(POOL_WINDOWS):
        left = w // 2
        right = w - 1 - left
        for i in range(blk):
            base, col = (i // GRID_W) * GRID_W, i % GRID_W
            lo, hi = max(col - left, 0), min(col + right, GRID_W - 1)
            band[gi, i, base + lo:base + hi + 1] = 1.0
            icnt[gi, i, :] = 1.0 / (hi - lo + 1)
    return jnp.asarray(band, BF16), jnp.asarray(icnt)


def _pool_mixer(u, pool_w, pool_scale):
    b, t, p = u.shape
    n_g, gd = pool_w.shape[0], pool_w.shape[1]
    rows = t // GRID_W
    blk = 4 * GRID_W
    assert t % (blk * POOL_UNROLL) == 0
    band, icnt = _pool_consts(blk)
    kern = functools.partial(_pool_kernel, rows=rows, blk=blk)
    grp = lambda i, g: (g, 0, 0)
    return pl.pallas_call(
        kern,
        grid=(b, n_g),
        in_specs=[pl.BlockSpec((1, t, gd), lambda i, g: (i, 0, g)),
                  pl.BlockSpec((1, blk, blk), grp),
                  pl.BlockSpec((1, blk, gd), grp),
                  pl.BlockSpec((1, gd, gd), grp),
                  pl.BlockSpec((1, 1, gd), grp)],
        out_specs=pl.BlockSpec((1, t, gd), lambda i, g: (i, 0, g)),
        out_shape=jax.ShapeDtypeStruct((b, t, p), BF16),
        scratch_shapes=[pltpu.VMEM((t + GRID_W, gd), F32)],
        compiler_params=_params("parallel", "parallel"),
        name="pool_mixer",
    )(u, band, icnt, pool_w.astype(BF16), pool_scale.reshape(n_g, 1, gd))


def _rms(a, g):
    return a * lax.rsqrt(jnp.mean(a * a, axis=-1, keepdims=True) + NORM_EPS) * g


def _outproj_kernel(dn_ref, pool_ref, x_ref, w1_ref, w2_ref, gt_ref, sc_ref, sh_ref,
                    gpost_ref, gpre_ref, x1_ref, h_ref):
    y = (jnp.dot(dn_ref[0], w1_ref[...], preferred_element_type=F32)
         + jnp.dot(pool_ref[0], w2_ref[...], preferred_element_type=F32))
    x1 = x_ref[0] + gt_ref[0] * _rms(y, gpost_ref[...])
    x1_ref[0] = x1
    h_ref[0] = (_rms(x1, gpre_ref[...]) * (1.0 + sc_ref[0]) + sh_ref[0]).astype(h_ref.dtype)


def _out_projection(dn, pool, x, w_out, gt1, sc2, sh2, g_post, g_pre, tm):
    b, t, d = x.shape
    wd = dn.shape[-1]
    w1 = w_out[:wd].astype(BF16)
    w2 = w_out[wd:].astype(BF16)
    row = lambda i, j: (i, j, 0)
    per_b = lambda i, j: (i, 0, 0)
    const = lambda i, j: (0, 0)
    return pl.pallas_call(
        _outproj_kernel,
        grid=(b, t // tm),
        in_specs=[pl.BlockSpec((1, tm, wd), row),
                  pl.BlockSpec((1, tm, pool.shape[-1]), row),
                  pl.BlockSpec((1, tm, d), row),
                  pl.BlockSpec(w1.shape, const),
                  pl.BlockSpec(w2.shape, const),
                  pl.BlockSpec((1, 1, d), per_b),
                  pl.BlockSpec((1, 1, d), per_b),
                  pl.BlockSpec((1, 1, d), per_b),
                  pl.BlockSpec((1, d), const),
                  pl.BlockSpec((1, d), const)],
        out_specs=[pl.BlockSpec((1, tm, d), row), pl.BlockSpec((1, tm, d), row)],
        out_shape=[jax.ShapeDtypeStruct((b, t, d), F32), jax.ShapeDtypeStruct((b, t, d), BF16)],
        compiler_params=_params("parallel", "parallel"),
        name="out_projection",
    )(dn, pool, x, w1, w2, gt1, sc2, sh2, g_post.reshape(1, d), g_pre.reshape(1, d))


FFN_HALO = 16


def _ffn_kernel(h_ref, hp_ref, hn_ref, x1_ref, wgu_ref, cw_ref, wd_ref, gt_ref, gpost_ref,
                out_ref, hext, act, *, fc, n_fc):
    j = pl.program_id(1)
    nj = pl.num_programs(1)
    tm = h_ref.shape[1]
    hal = FFN_HALO
    dff = fc * n_fc
    hext[0:hal, :] = jnp.where(j > 0, hp_ref[0], jnp.zeros_like(hp_ref[0]))
    hext[hal:hal + tm, :] = h_ref[0]
    hext[hal + tm:, :] = jnp.where(j < nj - 1, hn_ref[0], jnp.zeros_like(hn_ref[0]))
    for c in range(n_fc):
        up = jnp.dot(h_ref[0], wgu_ref[:, dff + c * fc:dff + (c + 1) * fc], preferred_element_type=F32)
        gate = jnp.dot(hext[...], wgu_ref[:, c * fc:(c + 1) * fc], preferred_element_type=F32)
        cw = cw_ref[:, c * fc:(c + 1) * fc]
        conv = (gate[hal - 1:hal - 1 + tm] * cw[0:1] + gate[hal:hal + tm] * cw[1:2]
                + gate[hal + 1:hal + 1 + tm] * cw[2:3])
        act[:, c * fc:(c + 1) * fc] = (_silu(conv) * up).astype(BF16)
    y = jnp.dot(act[...], wd_ref[...], preferred_element_type=F32)
    out_ref[0] = x1_ref[0] + gt_ref[0] * _rms(y, gpost_ref[...])


def _conv_ffn(h, x1, w_up, conv_ffn, w_down, gt2, g_post, tm, fc):
    b, t, d = x1.shape
    dff = w_down.shape[0]
    n_fc = dff // fc
    wgu = w_up.astype(BF16)
    hal = FFN_HALO
    nb = tm // hal
    last = t // hal - 1
    kern = functools.partial(_ffn_kernel, fc=fc, n_fc=n_fc)
    row = lambda i, j: (i, j, 0)
    per_b = lambda i, j: (i, 0, 0)
    const = lambda i, j: (0, 0)
    return pl.pallas_call(
        kern,
        grid=(b, t // tm),
        in_specs=[pl.BlockSpec((1, tm, d), row),
                  pl.BlockSpec((1, hal, d), lambda i, j: (i, jnp.maximum(j * nb - 1, 0), 0)),
                  pl.BlockSpec((1, hal, d), lambda i, j: (i, jnp.minimum((j + 1) * nb, last), 0)),
                  pl.BlockSpec((1, tm, d), row),
                  pl.BlockSpec(wgu.shape, const),
                  pl.BlockSpec(conv_ffn.shape, const),
                  pl.BlockSpec(w_down.shape, const),
                  pl.BlockSpec((1, 1, d), per_b),
                  pl.BlockSpec((1, d), const)],
        out_specs=pl.BlockSpec((1, tm, d), row),
        out_shape=jax.ShapeDtypeStruct((b, t, d), F32),
        scratch_shapes=[pltpu.VMEM((tm + 2 * hal, d), BF16), pltpu.VMEM((tm, dff), BF16)],
        compiler_params=_params("parallel", "parallel"),
        name="conv_ffn",
    )(h, h, h, x1, wgu, conv_ffn, w_down.astype(BF16), gt2, g_post.reshape(1, d))


def _layer(x, c, ctx, c_ctx, w_mod, b_mod, g_pre_mix, g_post_mix, g_pre_ffn, g_post_ffn,
           w_in, conv_qkv, a_log, dt_bias, o_norm, pool_w, pool_scale, w_out, w_up,
           conv_ffn, w_down):
    b, t, d = x.shape
    n_heads = a_log.shape[-1]
    w = n_heads * HEAD_DIM
    n_gate = 4 * n_heads

    rows = pl.cdiv(b + 1, SUBLANES) * SUBLANES
    c_all = jnp.zeros((rows, d), F32).at[:b].set(c).at[b].set(c_ctx)
    m = _modulation(c_all, w_mod, b_mod)
    sh1, sc1, gt1, sh2, sc2, gt2 = [m[:b, None, i * d:(i + 1) * d] for i in range(N_MOD)]
    csh1 = jnp.broadcast_to(m[b, 0 * d:1 * d], (b, 1, d))
    csc1 = jnp.broadcast_to(m[b, 1 * d:2 * d], (b, 1, d))

    off_g = 4 * w
    w_gate = jnp.transpose(w_in[:, off_g:off_g + n_gate].reshape(d, 4, n_heads), (0, 2, 1)).reshape(d, n_gate)
    w_all = jnp.concatenate([w_in[:, :off_g], w_in[:, off_g + n_gate:],
                             jnp.pad(w_gate, ((0, 0), (0, LANES - n_gate)))], axis=1).astype(BF16)
    tm = min(ROW_TILE, t)
    tp = min(PROJ_TILE, t)
    qkv, z, pool_u, gcol, grow = _in_projection(x, sc1, sh1, g_pre_mix, w_all, a_log, dt_bias, n_heads, tp)
    qkv_c, _, _, gcolc, growc = _in_projection(ctx, csc1, csh1, g_pre_mix, w_all, a_log, dt_bias,
                                                n_heads, ctx.shape[1])
    dn = _delta_mixer(qkv, qkv_c, gcol, grow, gcolc, growc, z, conv_qkv, o_norm, n_heads)
    pool = _pool_mixer(pool_u, pool_w, pool_scale)
    x1, h2 = _out_projection(dn, pool, x, w_out, gt1, sc2, sh2, g_post_mix, g_pre_ffn, tp)
    return _conv_ffn(h2, x1, w_up, conv_ffn, w_down, gt2, g_post_ffn, tm, FFN_CHUNK)


def kernel(x, c, ctx, c_ctx, w_mod, b_mod, g_pre_mix, g_post_mix, g_pre_ffn, g_post_ffn, w_in,
           conv_qkv, a_log, dt_bias, o_norm, pool_w, pool_scale, w_out, w_up, conv_ffn, w_down):
    depth = w_mod.shape[0]
    assert depth == 1, "context-stream update between layers is not implemented"
    return _layer(x, c, ctx, c_ctx, w_mod[0], b_mod[0], g_pre_mix[0], g_post_mix[0],
                  g_pre_ffn[0], g_post_ffn[0], w_in[0], conv_qkv[0], a_log[0], dt_bias[0],
                  o_norm[0], pool_w[0], pool_scale[0], w_out[0], w_up[0], conv_ffn[0], w_down[0])
```
